```python
import math
import jax
import jax.numpy as jnp
from jax import lax
import numpy as np

D_MODEL = 2048
BATCH = 1
SEQ = 8192
DEPTH = 4

CHUNK = 64
Q_BLOCK = 128
HEAD_DIM = 64
N_HEADS_DIFF = D_MODEL // (4 * HEAD_DIM)
N_HEADS_SB = D_MODEL // (4 * HEAD_DIM)
N_HEADS_CH = D_MODEL // (4 * HEAD_DIM)
DIFF_WIDTH = N_HEADS_DIFF * 2 * HEAD_DIM
SB_WIDTH = N_HEADS_SB * HEAD_DIM
CH_WIDTH = N_HEADS_CH * HEAD_DIM
MIX_WIDTH = DIFF_WIDTH + SB_WIDTH + CH_WIDTH
QKV_WIDTH = 3 * MIX_WIDTH
LEFT_CHUNKS = 8
BAND = (LEFT_CHUNKS + 1) * CHUNK
MAX_REL = 128
N_REL = (CHUNK - 1) + MAX_REL + 1
D_FF = 5632
CONV_WIDTH = 3
NORM_EPS = 1e-6
SUBLN_EPS = 1e-5

kernel_name = 'hybrid_chunk_causal_diff_sb_band_encoder'


def rmsnorm(x, g, eps=NORM_EPS):
    xf = x.astype(jnp.float32)
    y = xf * lax.rsqrt(jnp.mean(xf * xf, axis=-1, keepdims=True) + eps)
    return (y * g.astype(jnp.float32)).astype(x.dtype)


def alibi_slopes(n):
    return 2.0 ** (-8.0 * jnp.arange(1, n + 1, dtype=jnp.float32) / n)


def to_query_blocks(t):
    b, h, s = t.shape[:3]
    t = t.reshape((b, h, s // Q_BLOCK, Q_BLOCK) + t.shape[3:])
    return jnp.moveaxis(t, 2, 0)


def from_query_blocks(o):
    nb, b, h, qb, dv = o.shape
    return o.transpose(1, 0, 3, 2, 4).reshape(b, nb * qb, h * dv)


def split_columns(qkv):
    widths = [DIFF_WIDTH] * 3 + [SB_WIDTH] * 3 + [CH_WIDTH] * 3
    outs, start = [], 0
    for w in widths:
        outs.append(qkv[..., start:start + w])
        start += w
    return outs


def diff_attention(q, k, v, lam, lam_init, subln_g):
    b, s, _ = q.shape
    h = N_HEADS_DIFF
    q = q.reshape(b, s, h, 2, HEAD_DIM).transpose(0, 2, 1, 3, 4)
    k = k.reshape(b, s, h, 2, HEAD_DIM).transpose(0, 2, 1, 3, 4)
    v = v.reshape(b, s, h, 2 * HEAD_DIM).transpose(0, 2, 1, 3)
    slopes = alibi_slopes(h)
    k_pos = jnp.arange(s)
    k_chunk = k_pos // CHUNK
    scale = HEAD_DIM ** -0.5

    def block(args):
        q_blk, i = args
        q_pos = i * Q_BLOCK + jnp.arange(Q_BLOCK)
        sc = jnp.einsum('bhqcd,bhkcd->bhcqk', q_blk, k).astype(jnp.float32) * scale
        dist = jnp.abs(q_pos[:, None] - k_pos[None, :]).astype(jnp.float32)
        bias = -slopes[:, None, None] * dist
        allowed = k_chunk[None, :] <= (q_pos // CHUNK)[:, None]
        sc = jnp.where(allowed, sc + bias[None, :, None], -jnp.inf)
        p = jax.nn.softmax(sc, axis=-1)
        w = (p[:, :, 0] - lam * p[:, :, 1]).astype(v.dtype)
        return jnp.einsum('bhqk,bhkv->bhqv', w, v)

    o = lax.map(block, (to_query_blocks(q), jnp.arange(s // Q_BLOCK)))
    o = rmsnorm(o, subln_g, SUBLN_EPS) * (1.0 - lam_init)
    return from_query_blocks(o)


def stick_breaking_attention(q, k, v, head_g):
    b, s, _ = q.shape
    h = N_HEADS_SB
    q = q.reshape(b, s, h, HEAD_DIM).transpose(0, 2, 1, 3)
    k = k.reshape(b, s, h, HEAD_DIM).transpose(0, 2, 1, 3)
    v = v.reshape(b, s, h, HEAD_DIM).transpose(0, 2, 1, 3)
    k_pos = jnp.arange(s)
    scale = HEAD_DIM ** -0.5

    def block(args):
        q_blk, i = args
        q_pos = i * Q_BLOCK + jnp.arange(Q_BLOCK)
        z = jnp.einsum('bhqd,bhkd->bhqk', q_blk, k).astype(jnp.float32) * scale
        causal = k_pos[None, :] < q_pos[:, None]
        log_1m_beta = jnp.where(causal, -jax.nn.softplus(z), 0.0)
        suffix = lax.cumsum(log_1m_beta, axis=3, reverse=True) - log_1m_beta
        log_a = jax.nn.log_sigmoid(z) + suffix
        a = jnp.where(causal, jnp.exp(log_a), 0.0).astype(v.dtype)
        return jnp.einsum('bhqk,bhkd->bhqd', a, v)

    o = lax.map(block, (to_query_blocks(q), jnp.arange(s // Q_BLOCK)))
    o = rmsnorm(o, head_g[:, None, :])
    return from_query_blocks(o)


def chunk_band_attention(q, k, v, rel_table, head_g):
    b, s, _ = q.shape
    h = N_HEADS_CH
    nc = s // CHUNK
    pad = LEFT_CHUNKS * CHUNK
    q = q.reshape(b, s, h, HEAD_DIM).transpose(0, 2, 1, 3)
    k = k.reshape(b, s, h, HEAD_DIM).transpose(0, 2, 1, 3)
    v = v.reshape(b, s, h, HEAD_DIM).transpose(0, 2, 1, 3)
    qc = q.reshape(b, h, nc, CHUNK, HEAD_DIM)

    def band(t):
        tp = jnp.pad(t, ((0, 0), (0, 0), (pad, 0), (0, 0)))
        tp = tp.reshape(b, h, nc + LEFT_CHUNKS, CHUNK, HEAD_DIM)
        return jnp.concatenate([tp[:, :, o:o + nc] for o in range(LEFT_CHUNKS + 1)], axis=3)

    kb, vb = band(k), band(v)
    sc = jnp.einsum('bhcqd,bhckd->bhcqk', qc, kb).astype(jnp.float32) * HEAD_DIM ** -0.5
    q_loc = jnp.arange(CHUNK)
    k_loc = jnp.arange(BAND) - pad
    rel_idx = jnp.clip(q_loc[:, None] - k_loc[None, :], -(CHUNK - 1), MAX_REL) + (CHUNK - 1)
    bias = rel_table[:, rel_idx].astype(jnp.float32)
    valid = (jnp.arange(nc)[:, None] * CHUNK + k_loc[None, :]) >= 0
    sc = jnp.where(valid[:, None, :], sc + bias[:, None], -jnp.inf)
    p = jax.nn.softmax(sc, axis=-1).astype(vb.dtype)
    o = jnp.einsum('bhcqk,bhckd->bhcqd', p, vb).reshape(b, h, s, HEAD_DIM)
    o = rmsnorm(o, head_g[:, None, :])
    return o.transpose(0, 2, 1, 3).reshape(b, s, h * HEAD_DIM)


def causal_dwconv(hid, w, bias):
    s = hid.shape[1]
    hp = jnp.pad(hid, ((0, 0), (CONV_WIDTH - 1, 0), (0, 0)))
    out = bias
    for j in range(CONV_WIDTH):
        out = out + hp[:, j:j + s] * w[j]
    return out


def setup_inputs(seed: int = 0) -> dict:
    key = jax.random.key(seed)
    ks = jax.random.split(key, 20)

    def nrm(k, shape, scale):
        return jax.random.normal(k, shape, jnp.float32) * scale

    def gain(k, shape):
        return 1.0 + nrm(k, shape, 0.05)

    return {
        'x': nrm(ks[0], (BATCH, SEQ, D_MODEL), 1.0),
        'attn_norm': gain(ks[1], (DEPTH, D_MODEL)),
        'w_qkv': nrm(ks[2], (DEPTH, D_MODEL, QKV_WIDTH), D_MODEL ** -0.5),
        'lambda_q1': nrm(ks[3], (DEPTH, HEAD_DIM), 0.1),
        'lambda_k1': nrm(ks[4], (DEPTH, HEAD_DIM), 0.1),
        'lambda_q2': nrm(ks[5], (DEPTH, HEAD_DIM), 0.1),
        'lambda_k2': nrm(ks[6], (DEPTH, HEAD_DIM), 0.1),
        'diff_subln': gain(ks[7], (DEPTH, 2 * HEAD_DIM)),
        'sb_norm': gain(ks[8], (DEPTH, N_HEADS_SB, HEAD_DIM)),
        'rel_bias': nrm(ks[9], (DEPTH, N_HEADS_CH, N_REL), 0.2),
        'ch_norm': gain(ks[10], (DEPTH, N_HEADS_CH, HEAD_DIM)),
        'w_o': nrm(ks[11], (DEPTH, MIX_WIDTH, D_MODEL), MIX_WIDTH ** -0.5),
        'ffn_norm': gain(ks[12], (DEPTH, D_MODEL)),
        'w_gate': nrm(ks[13], (DEPTH, D_MODEL, D_FF), D_MODEL ** -0.5),
        'w_up': nrm(ks[14], (DEPTH, D_MODEL, D_FF), D_MODEL ** -0.5),
        'conv_w': nrm(ks[15], (DEPTH, CONV_WIDTH, D_FF), CONV_WIDTH ** -0.5),
        'conv_b': nrm(ks[16], (DEPTH, D_FF), 0.02),
        'w_down': nrm(ks[17], (DEPTH, D_FF, D_MODEL), D_FF ** -0.5),
        'final_norm': gain(ks[18], (D_MODEL,)),
    }


def reference(x, attn_norm, w_qkv, lambda_q1, lambda_k1, lambda_q2, lambda_k2,
              diff_subln, sb_norm, rel_bias, ch_norm, w_o, ffn_norm, w_gate, w_up,
              conv_w, conv_b, w_down, final_norm):
    for l in range(DEPTH):
        lam_init = 0.8 - 0.6 * math.exp(-0.3 * l)
        h = rmsnorm(x, attn_norm[l])
        qkv = jnp.einsum('bsd,de->bse', h, w_qkv[l])
        q_a, k_a, v_a, q_b, k_b, v_b, q_c, k_c, v_c = split_columns(qkv)
        lam = (jnp.exp(jnp.sum(lambda_q1[l] * lambda_k1[l]).astype(jnp.float32))
               - jnp.exp(jnp.sum(lambda_q2[l] * lambda_k2[l]).astype(jnp.float32))
               + lam_init)
        y_a = diff_attention(q_a, k_a, v_a, lam, lam_init, diff_subln[l])
        y_b = stick_breaking_attention(q_b, k_b, v_b, sb_norm[l])
        y_c = chunk_band_attention(q_c, k_c, v_c, rel_bias[l], ch_norm[l])
        y = jnp.concatenate([y_a, y_b, y_c], axis=-1)
        x = x + jnp.einsum('bse,ed->bsd', y, w_o[l])
        h = rmsnorm(x, ffn_norm[l])
        g = causal_dwconv(jnp.einsum('bsd,df->bsf', h, w_gate[l]), conv_w[l], conv_b[l])
        u = jnp.einsum('bsd,df->bsf', h, w_up[l])
        x = x + jnp.einsum('bsf,fd->bsd', jax.nn.silu(g) * u, w_down[l])
    return rmsnorm(x, final_norm)
```

```python
import functools
import math

import jax
import jax.numpy as jnp
import numpy as np
from jax import lax
from jax.experimental import pallas as pl
from jax.experimental.pallas import tpu as pltpu

D_MODEL = 2048
SEQ = 8192
DEPTH = 4
CHUNK = 64
HEAD_DIM = 64
N_HEADS = 8
DIFF_WIDTH = 1024
SB_WIDTH = 512
CH_WIDTH = 512
QKV_WIDTH = 6144
LEFT_CHUNKS = 8
MAX_REL = 128
D_FF = 5632
CONV_WIDTH = 3
NORM_EPS = 1e-6
SUBLN_EPS = 1e-5
SCALE = HEAD_DIM ** -0.5

LANES = 128
ATT_T = 256
N_QBLK = SEQ // ATT_T
HALO = 16
VMEM_LIMIT = 48 * 1024 * 1024

QA, KA, VA = 0, 8, 16
QB, KB, VB = 24, 28, 32
QC, KC, VC = 36, 40, 44

_NT = (((1,), (1,)), ((), ()))
_TN = (((0,), (0,)), ((), ()))
_F32 = jnp.float32
_BF16 = jnp.bfloat16
_NEG = -1e30


def _cparams(n_axes):
    return pltpu.CompilerParams(
        dimension_semantics=("arbitrary",) * n_axes,
        vmem_limit_bytes=VMEM_LIMIT)


def _qkv_kernel(x_ref, g_ref, w_ref, o_ref, h_scr):
    @pl.when(pl.program_id(1) == 0)
    def _():
        x = x_ref[...]
        ms = jnp.mean(x * x, axis=-1, keepdims=True)
        h_scr[...] = (x * lax.rsqrt(ms + NORM_EPS) * g_ref[...]).astype(_BF16)

    o_ref[...] = jnp.dot(h_scr[...], w_ref[...],
                         preferred_element_type=_F32).astype(_BF16)


def _qkv_proj(x, g_all, w_all, layer, tm=1024, tn=1024):
    return pl.pallas_call(
        _qkv_kernel,
        grid=(SEQ // tm, QKV_WIDTH // tn),
        in_specs=[
            pl.BlockSpec((tm, D_MODEL), lambda i, j: (i, 0)),
            pl.BlockSpec((None, 1, D_MODEL), lambda i, j: (layer, 0, 0)),
            pl.BlockSpec((None, D_MODEL, tn), lambda i, j: (layer, 0, j)),
        ],
        out_specs=pl.BlockSpec((tm, tn), lambda i, j: (i, j)),
        out_shape=jax.ShapeDtypeStruct((SEQ, QKV_WIDTH), _BF16),
        scratch_shapes=[pltpu.VMEM((tm, D_MODEL), _BF16)],
        compiler_params=_cparams(2),
        name="qkv_proj",
    )(x, g_all, w_all)


def _split_heads(q_ref):
    q = q_ref[...] * jnp.asarray(SCALE, _BF16)
    lane = lax.broadcasted_iota(jnp.int32, q.shape, 1)
    zero = jnp.zeros_like(q)
    return jnp.where(lane < HEAD_DIM, q, zero), jnp.where(lane >= HEAD_DIM, q, zero)


def _key_block(ref, j):
    return ref[pl.ds(pl.multiple_of(j * ATT_T, ATT_T), ATT_T), :]


def _pair_rmsnorm_store(o_t, gain_col, o_ref):
    top, bot = o_t[:HEAD_DIM], o_t[HEAD_DIM:]
    top = top * lax.rsqrt(jnp.mean(top * top, axis=0, keepdims=True) + NORM_EPS)
    bot = bot * lax.rsqrt(jnp.mean(bot * bot, axis=0, keepdims=True) + NORM_EPS)
    y = jnp.concatenate([top, bot], axis=0) * gain_col
    o_ref[...] = y.T.astype(o_ref.dtype)


def _diff_kernel(slope_ref, lam_ref, gain_ref, q_ref, k_ref, v_ref, o_ref,
                 rel_scr, diag_scr, m_scr, l_scr, acc_scr, *, lam_init):
    h = pl.program_id(0)
    i = pl.program_id(1)
    slope = slope_ref[h]

    @pl.when(i == 0)
    def _():
        row = lax.broadcasted_iota(jnp.int32, (ATT_T, ATT_T), 0)
        col = lax.broadcasted_iota(jnp.int32, (ATT_T, ATT_T), 1)
        dist = (col - row).astype(_F32)
        rel_scr[...] = -slope * dist
        allowed = (row // CHUNK) <= (col // CHUNK)
        diag_scr[...] = jnp.where(allowed, -slope * jnp.abs(dist), -jnp.inf)

    q_maps = _split_heads(q_ref)
    m_scr[...] = jnp.full(m_scr.shape, _NEG, _F32)
    l_scr[...] = jnp.zeros(l_scr.shape, _F32)
    acc_scr[...] = jnp.zeros(acc_scr.shape, _F32)

    def step(j, bias_ref, offset):
        kb = _key_block(k_ref, j)
        vb = _key_block(v_ref, j)
        bias = bias_ref[...]
        for c in range(2):
            s = lax.dot_general(kb, q_maps[c], _NT, preferred_element_type=_F32) + bias
            m_old = m_scr[c]
            m_new = jnp.maximum(m_old, jnp.max(s, axis=0, keepdims=True) + offset)
            alpha = jnp.exp(m_old - m_new)
            p = jnp.exp(s - (m_new - offset))
            l_scr[c] = alpha * l_scr[c] + jnp.sum(p, axis=0, keepdims=True)
            pv = lax.dot_general(vb, p.astype(_BF16), _TN, preferred_element_type=_F32)
            acc_scr[c] = alpha * acc_scr[c] + pv
            m_scr[c] = m_new

    def body(j, carry):
        step(j, rel_scr, -slope * ((i - j) * ATT_T).astype(_F32))
        return carry

    lax.fori_loop(0, i, body, 0)
    step(i, diag_scr, jnp.zeros((), _F32))

    lam_v = lam_ref[...]
    lam = (jnp.exp(jnp.sum(lam_v[0:1] * lam_v[1:2], axis=-1, keepdims=True))
           - jnp.exp(jnp.sum(lam_v[2:3] * lam_v[3:4], axis=-1, keepdims=True))
           + lam_init)
    o_t = acc_scr[0] / l_scr[0] - lam * (acc_scr[1] / l_scr[1])
    ms = jnp.mean(o_t * o_t, axis=0, keepdims=True)
    y = o_t * lax.rsqrt(ms + SUBLN_EPS) * gain_ref[...] * (1.0 - lam_init)
    o_ref[...] = y.T.astype(o_ref.dtype)


def _diff_attention(qkv, slopes, lam_vecs, gain_col, layer, lam_init):
    return pl.pallas_call(
        functools.partial(_diff_kernel, lam_init=lam_init),
        grid=(N_HEADS, N_QBLK),
        in_specs=[
            pl.BlockSpec(memory_space=pltpu.SMEM),
            pl.BlockSpec((None, 4, HEAD_DIM), lambda h, i: (layer, 0, 0)),
            pl.BlockSpec((None, LANES, 1), lambda h, i: (layer, 0, 0)),
            pl.BlockSpec((ATT_T, LANES), lambda h, i: (i, QA + h)),
            pl.BlockSpec((SEQ, LANES), lambda h, i: (0, KA + h)),
            pl.BlockSpec((SEQ, LANES), lambda h, i: (0, VA + h)),
        ],
        out_specs=pl.BlockSpec((ATT_T, LANES), lambda h, i: (i, h)),
        out_shape=jax.ShapeDtypeStruct((SEQ, DIFF_WIDTH), _BF16),
        scratch_shapes=[
            pltpu.VMEM((ATT_T, ATT_T), _F32),
            pltpu.VMEM((ATT_T, ATT_T), _F32),
            pltpu.VMEM((2, 1, ATT_T), _F32),
            pltpu.VMEM((2, 1, ATT_T), _F32),
            pltpu.VMEM((2, LANES, ATT_T), _F32),
        ],
        compiler_params=_cparams(2),
        name="diff_attention",
    )(slopes, lam_vecs, gain_col, qkv, qkv, qkv)


def _sb_kernel(gain_ref, q_ref, k_ref, v_ref, o_ref, tri_scr, r_scr, acc_scr):
    i = pl.program_id(1)

    @pl.when(i == 0)
    def _():
        row = lax.broadcasted_iota(jnp.int32, (ATT_T, ATT_T), 0)
        col = lax.broadcasted_iota(jnp.int32, (ATT_T, ATT_T), 1)
        tri_scr[...] = jnp.where(col > row, 1.0, 0.0).astype(_BF16)

    q_heads = _split_heads(q_ref)
    r_scr[...] = jnp.zeros(r_scr.shape, _F32)
    acc_scr[...] = jnp.zeros(acc_scr.shape, _F32)

    def step(j, diagonal):
        kb = _key_block(k_ref, j)
        vb = _key_block(v_ref, j)
        tri = tri_scr[...]
        if diagonal:
            row = lax.broadcasted_iota(jnp.int32, (ATT_T, ATT_T), 0)
            col = lax.broadcasted_iota(jnp.int32, (ATT_T, ATT_T), 1)
            causal = row < col
        for hh in range(2):
            z = lax.dot_general(kb, q_heads[hh], _NT, preferred_element_type=_F32)
            sp = jnp.maximum(z, 0.0) + jnp.log(1.0 + jnp.exp(-jnp.abs(z)))
            nl = jnp.where(causal, sp, 0.0) if diagonal else sp
            hi = nl.astype(_BF16)
            lo = (nl - hi.astype(_F32)).astype(_BF16)
            suffix = (jnp.dot(tri, hi, preferred_element_type=_F32)
                      + jnp.dot(tri, lo, preferred_element_type=_F32))
            r_old = r_scr[hh]
            a = jnp.exp((z - sp) - suffix - r_old)
            if diagonal:
                a = jnp.where(causal, a, 0.0)
            acc_scr[hh] += lax.dot_general(vb, a.astype(_BF16), _TN,
                                           preferred_element_type=_F32)
            r_scr[hh] = r_old + jnp.sum(nl, axis=0, keepdims=True)

    step(i, True)

    def body(t, carry):
        step(i - 1 - t, False)
        return carry

    lax.fori_loop(0, i, body, 0)

    rows = lax.broadcasted_iota(jnp.int32, (LANES, ATT_T), 0)
    o_t = jnp.where(rows < HEAD_DIM, acc_scr[0], acc_scr[1])
    _pair_rmsnorm_store(o_t, gain_ref[...], o_ref)


def _sb_attention(qkv, gain_col, layer):
    return pl.pallas_call(
        _sb_kernel,
        grid=(N_HEADS // 2, N_QBLK),
        in_specs=[
            pl.BlockSpec((None, None, LANES, 1), lambda p, i: (layer, p, 0, 0)),
            pl.BlockSpec((ATT_T, LANES), lambda p, i: (i, QB + p)),
            pl.BlockSpec((SEQ, LANES), lambda p, i: (0, KB + p)),
            pl.BlockSpec((SEQ, LANES), lambda p, i: (0, VB + p)),
        ],
        out_specs=pl.BlockSpec((ATT_T, LANES), lambda p, i: (i, p)),
        out_shape=jax.ShapeDtypeStruct((SEQ, SB_WIDTH), _BF16),
        scratch_shapes=[
            pltpu.VMEM((ATT_T, ATT_T), _BF16),
            pltpu.VMEM((2, 1, ATT_T), _F32),
            pltpu.VMEM((2, LANES, ATT_T), _F32),
        ],
        compiler_params=_cparams(2),
        name="sb_attention",
    )(gain_col, qkv, qkv, qkv)


BAND_TILES = 1 + (LEFT_CHUNKS * CHUNK) // ATT_T


def _band_kernel(bias_ref, gain_ref, q_ref, k_ref, v_ref, o_ref):
    i = pl.program_id(1)
    q_heads = _split_heads(q_ref)
    rows = lax.broadcasted_iota(jnp.int32, (LANES, ATT_T), 0)
    outs = []
    for hh in range(2):
        scores, vals = [], []
        for kk in range(BAND_TILES):
            jb = i - (BAND_TILES - 1) + kk
            jc = jnp.maximum(jb, 0)
            s = lax.dot_general(_key_block(k_ref, jc), q_heads[hh], _NT,
                                preferred_element_type=_F32) + bias_ref[hh, kk]
            scores.append(jnp.where(jb >= 0, s, -jnp.inf))
            vals.append(_key_block(v_ref, jc))
        m = functools.reduce(jnp.maximum,
                             [jnp.max(s, axis=0, keepdims=True) for s in scores])
        probs = [jnp.exp(s - m) for s in scores]
        denom = functools.reduce(jnp.add,
                                 [jnp.sum(p, axis=0, keepdims=True) for p in probs])
        o_t = functools.reduce(jnp.add, [
            lax.dot_general(v, p.astype(_BF16), _TN, preferred_element_type=_F32)
            for v, p in zip(vals, probs)])
        outs.append(o_t / denom)
    o_t = jnp.where(rows < HEAD_DIM, outs[0], outs[1])
    _pair_rmsnorm_store(o_t, gain_ref[...], o_ref)


def _band_bias_index():
    key = np.arange(ATT_T)[:, None]
    qry = np.arange(ATT_T)[None, :]
    idx, mask = [], []
    for kk in range(BAND_TILES):
        back = (BAND_TILES - 1 - kk) * ATT_T
        d = back + qry - key
        dc = back // CHUNK + qry // CHUNK - key // CHUNK
        idx.append(np.clip(d, -(CHUNK - 1), MAX_REL) + (CHUNK - 1))
        mask.append((dc >= 0) & (dc <= LEFT_CHUNKS))
    return np.stack(idx).astype(np.int32), np.stack(mask)


def _band_attention(qkv, bias_tiles, gain_col, layer):
    return pl.pallas_call(
        _band_kernel,
        grid=(N_HEADS // 2, N_QBLK),
        in_specs=[
            pl.BlockSpec((None, 2, BAND_TILES, ATT_T, ATT_T), lambda p, i: (p, 0, 0, 0, 0)),
            pl.BlockSpec((None, None, LANES, 1), lambda p, i: (layer, p, 0, 0)),
            pl.BlockSpec((ATT_T, LANES), lambda p, i: (i, QC + p)),
            pl.BlockSpec((SEQ, LANES), lambda p, i: (0, KC + p)),
            pl.BlockSpec((SEQ, LANES), lambda p, i: (0, VC + p)),
        ],
        out_specs=pl.BlockSpec((ATT_T, LANES), lambda p, i: (i, p)),
        out_shape=jax.ShapeDtypeStruct((SEQ, CH_WIDTH), _BF16),
        compiler_params=_cparams(2),
        name="band_attention",
    )(bias_tiles, gain_col, qkv, qkv, qkv)


def _oproj_kernel(y_ref, w_ref, x_ref, o_ref):
    o_ref[...] = x_ref[...] + jnp.dot(y_ref[...], w_ref[...], preferred_element_type=_F32)


def _out_proj(y, w_all, x, layer, tm=1024, tn=512):
    return pl.pallas_call(
        _oproj_kernel,
        grid=(SEQ // tm, D_MODEL // tn),
        in_specs=[
            pl.BlockSpec((tm, D_MODEL), lambda i, j: (i, 0)),
            pl.BlockSpec((None, D_MODEL, tn), lambda i, j: (layer, 0, j)),
            pl.BlockSpec((tm, tn), lambda i, j: (i, j)),
        ],
        out_specs=pl.BlockSpec((tm, tn), lambda i, j: (i, j)),
        out_shape=jax.ShapeDtypeStruct((SEQ, D_MODEL), _F32),
        compiler_params=_cparams(2),
        name="out_proj",
    )(y, w_all, x)


def _ffn_kernel(x_ref, halo_ref, g_ref, wg_ref, wu_ref, cw_ref, cb_ref, wd_ref, o_ref,
                h_scr, gate_scr, *, tm):
    i = pl.program_id(0)
    j = pl.program_id(1)

    def norm(x):
        ms = jnp.mean(x * x, axis=-1, keepdims=True)
        return (x * lax.rsqrt(ms + NORM_EPS) * g_ref[...]).astype(_BF16)

    @pl.when(j == 0)
    def _():
        x = x_ref[...]
        h_scr[HALO:, :] = norm(x)
        halo = jnp.where(i > 0, halo_ref[...], 0.0)
        h_scr[:HALO, :] = norm(halo)
        o_ref[...] = x

    gate_scr[...] = jnp.dot(h_scr[...], wg_ref[...], preferred_element_type=_F32)
    up = jnp.dot(h_scr[HALO:, :], wu_ref[...], preferred_element_type=_F32)
    cw = cw_ref[...]
    g = cb_ref[...]
    for t in range(CONV_WIDTH):
        lo = HALO - (CONV_WIDTH - 1) + t
        g = g + gate_scr[lo:lo + tm, :] * cw[t:t + 1]
    act = (g * jax.nn.sigmoid(g) * up).astype(_BF16)
    o_ref[...] += jnp.dot(act, wd_ref[...], preferred_element_type=_F32)


def _ffn(x, g_all, wg_all, wu_all, cw_all, cb_all, wd_all, layer, tm=512, tf=512):
    halo_blocks = tm // HALO
    return pl.pallas_call(
        functools.partial(_ffn_kernel, tm=tm),
        grid=(SEQ // tm, D_FF // tf),
        in_specs=[
            pl.BlockSpec((tm, D_MODEL), lambda i, j: (i, 0)),
            pl.BlockSpec((HALO, D_MODEL),
                         lambda i, j: (jnp.maximum(i * halo_blocks - 1, 0), 0)),
            pl.BlockSpec((None, 1, D_MODEL), lambda i, j: (layer, 0, 0)),
            pl.BlockSpec((None, D_MODEL, tf), lambda i, j: (layer, 0, j)),
            pl.BlockSpec((None, D_MODEL, tf), lambda i, j: (layer, 0, j)),
            pl.BlockSpec((None, CONV_WIDTH, tf), lambda i, j: (layer, 0, j)),
            pl.BlockSpec((None, 1, tf), lambda i, j: (layer, 0, j)),
            pl.BlockSpec((None, tf, D_MODEL), lambda i, j: (layer, j, 0)),
        ],
        out_specs=pl.BlockSpec((tm, D_MODEL), lambda i, j: (i, 0)),
        out_shape=jax.ShapeDtypeStruct((SEQ, D_MODEL), _F32),
        scratch_shapes=[
            pltpu.VMEM((tm + HALO, D_MODEL), _BF16),
            pltpu.VMEM((tm + HALO, tf), _F32),
        ],
        compiler_params=_cparams(2),
        name="gated_mlp",
    )(x, x, g_all, wg_all, wu_all, cw_all, cb_all, wd_all)


def _final_norm_kernel(x_ref, g_ref, o_ref):
    x = x_ref[...]
    ms = jnp.mean(x * x, axis=-1, keepdims=True)
    o_ref[...] = x * lax.rsqrt(ms + NORM_EPS) * g_ref[...]


def _final_norm(x, g, tm=512):
    return pl.pallas_call(
        _final_norm_kernel,
        grid=(SEQ // tm,),
        in_specs=[
            pl.BlockSpec((tm, D_MODEL), lambda i: (i, 0)),
            pl.BlockSpec((1, D_MODEL), lambda i: (0, 0)),
        ],
        out_specs=pl.BlockSpec((tm, D_MODEL), lambda i: (i, 0)),
        out_shape=jax.ShapeDtypeStruct((SEQ, D_MODEL), _F32),
        compiler_params=_cparams(1),
        name="final_norm",
    )(x, g)


def kernel(x, attn_norm, w_qkv, lambda_q1, lambda_k1, lambda_q2, lambda_k2, diff_subln,
           sb_norm, rel_bias, ch_norm, w_o, ffn_norm, w_gate, w_up, conv_w, conv_b, w_down,
           final_norm):
    batch, seq, d_model = x.shape
    assert (batch, seq, d_model) == (1, SEQ, D_MODEL)
    x2 = x.reshape(SEQ, D_MODEL)

    w_qkv_b = w_qkv.astype(_BF16)
    w_o_b = w_o.astype(_BF16)
    w_gate_b = w_gate.astype(_BF16)
    w_up_b = w_up.astype(_BF16)
    w_down_b = w_down.astype(_BF16)

    attn_g = attn_norm.reshape(DEPTH, 1, D_MODEL)
    ffn_g = ffn_norm.reshape(DEPTH, 1, D_MODEL)
    conv_b3 = conv_b.reshape(DEPTH, 1, D_FF)
    lam_vecs = jnp.stack([lambda_q1, lambda_k1, lambda_q2, lambda_k2], axis=1)
    subln_col = diff_subln.reshape(DEPTH, 2 * HEAD_DIM, 1)
    sb_col = sb_norm.reshape(DEPTH, N_HEADS // 2, 2 * HEAD_DIM, 1)
    ch_col = ch_norm.reshape(DEPTH, N_HEADS // 2, 2 * HEAD_DIM, 1)
    slopes = 2.0 ** (-8.0 * jnp.arange(1, N_HEADS + 1, dtype=_F32) / N_HEADS)
    band_idx, band_mask = _band_bias_index()

    for layer in range(DEPTH):
        lam_init = 0.8 - 0.6 * math.exp(-0.3 * layer)
        qkv = _qkv_proj(x2, attn_g, w_qkv_b, layer)
        y_a = _diff_attention(qkv, slopes, lam_vecs, subln_col, layer, lam_init)
        y_b = _sb_attention(qkv, sb_col, layer)
        bias_tiles = jnp.where(band_mask, rel_bias[layer][:, band_idx], -jnp.inf)
        bias_tiles = bias_tiles.reshape(N_HEADS // 2, 2, BAND_TILES, ATT_T, ATT_T)
        y_c = _band_attention(qkv, bias_tiles, ch_col, layer)
        y = jnp.concatenate([y_a, y_b, y_c], axis=-1)
        x2 = _out_proj(y, w_o_b, x2, layer)
        x2 = _ffn(x2, ffn_g, w_gate_b, w_up_b, conv_w, conv_b3, w_down_b, layer)
    out = _final_norm(x2, final_norm.reshape(1, D_MODEL))
    return out.reshape(1, SEQ, D_MODEL)
```

```python
import functools
import math

import jax
import jax.numpy as jnp
import numpy as np
from jax import lax
from jax.experimental import pallas as pl
from jax.experimental.pallas import tpu as pltpu

D_MODEL = 2048
SEQ = 8192
DEPTH = 4
CHUNK = 64
HEAD_DIM = 64
N_HEADS = 8
DIFF_WIDTH = 1024
SB_WIDTH = 512
CH_WIDTH = 512
QKV_WIDTH = 6144
LEFT_CHUNKS = 8
MAX_REL = 128
D_FF = 5632
CONV_WIDTH = 3
NORM_EPS = 1e-6
SUBLN_EPS = 1e-5
SCALE = HEAD_DIM ** -0.5
LOG2E = math.log2(math.e)

LANES = 128
ATT_T = 256
N_QBLK = SEQ // ATT_T
HALO = 16
VMEM_LIMIT = 48 * 1024 * 1024

QA, KA, VA = 0, 8, 16
QB, KB, VB = 24, 28, 32
QC, KC, VC = 36, 40, 44

_NT = (((1,), (1,)), ((), ()))
_TN = (((0,), (0,)), ((), ()))
_F32 = jnp.float32
_BF16 = jnp.bfloat16
_NEG = -1e30


def _cparams(n_axes):
    return pltpu.CompilerParams(
        dimension_semantics=("arbitrary",) * n_axes,
        vmem_limit_bytes=VMEM_LIMIT)


def _qkv_kernel(x_ref, g_ref, w_ref, cs_ref, o_ref, h_scr):
    @pl.when(pl.program_id(1) == 0)
    def _():
        x = x_ref[...]
        ms = jnp.mean(x * x, axis=-1, keepdims=True)
        h_scr[...] = (x * lax.rsqrt(ms + NORM_EPS) * g_ref[...]).astype(_BF16)

    acc = jnp.dot(h_scr[...], w_ref[...], preferred_element_type=_F32)
    o_ref[...] = (acc * cs_ref[...]).astype(_BF16)


def _qkv_col_scale():
    cs = np.ones((1, QKV_WIDTH), np.float32)
    cs[:, QA * LANES:KA * LANES] = SCALE * LOG2E
    cs[:, QB * LANES:KB * LANES] = SCALE
    cs[:, QC * LANES:KC * LANES] = SCALE * LOG2E
    return cs


def _qkv_proj(x, g_all, w_all, col_scale, layer, tm=1024, tn=1024):
    return pl.pallas_call(
        _qkv_kernel,
        grid=(SEQ // tm, QKV_WIDTH // tn),
        in_specs=[
            pl.BlockSpec((tm, D_MODEL), lambda i, j: (i, 0)),
            pl.BlockSpec((None, 1, D_MODEL), lambda i, j: (layer, 0, 0)),
            pl.BlockSpec((None, D_MODEL, tn), lambda i, j: (layer, 0, j)),
            pl.BlockSpec((1, tn), lambda i, j: (0, j)),
        ],
        out_specs=pl.BlockSpec((tm, tn), lambda i, j: (i, j)),
        out_shape=jax.ShapeDtypeStruct((SEQ, QKV_WIDTH), _BF16),
        scratch_shapes=[pltpu.VMEM((tm, D_MODEL), _BF16)],
        compiler_params=_cparams(2),
        name="qkv_proj",
    )(x, g_all, w_all, col_scale)


def _split_heads(q_ref):
    q = q_ref[...]
    lane = lax.broadcasted_iota(jnp.int32, q.shape, 1)
    zero = jnp.zeros_like(q)
    return jnp.where(lane < HEAD_DIM, q, zero), jnp.where(lane >= HEAD_DIM, q, zero)


def _key_block(ref, j):
    return ref[pl.ds(pl.multiple_of(j * ATT_T, ATT_T), ATT_T), :]


def _pair_rmsnorm_store(o_t, gain_col, o_ref):
    top, bot = o_t[:HEAD_DIM], o_t[HEAD_DIM:]
    top = top * lax.rsqrt(jnp.mean(top * top, axis=0, keepdims=True) + NORM_EPS)
    bot = bot * lax.rsqrt(jnp.mean(bot * bot, axis=0, keepdims=True) + NORM_EPS)
    y = jnp.concatenate([top, bot], axis=0) * gain_col
    o_ref[...] = y.T.astype(o_ref.dtype)


DIFF_TQ = 512
DIFF_TK = 256
DIFF_DIAG = DIFF_TQ // DIFF_TK
DIFF_MASKED = DIFF_DIAG + 1


def _diff_kernel(slope_ref, lam_ref, gain_ref, q_ref, k_ref, v_ref, o_ref,
                 bias_scr, vt_scr, s_scr, mx_scr, p_scr, al_scr, m_scr, l_scr, acc_scr,
                 *, lam_init):
    h = pl.program_id(0)
    i = pl.program_id(1)
    slope2 = slope_ref[h] * LOG2E
    n_off = i * DIFF_DIAG
    nb = n_off + DIFF_DIAG

    @pl.when(i == 0)
    def _():
        row = lax.broadcasted_iota(jnp.int32, (DIFF_TK, DIFF_TQ), 0)
        col = lax.broadcasted_iota(jnp.int32, (DIFF_TK, DIFF_TQ), 1)
        bias_scr[0] = -slope2 * (col - row).astype(_F32)
        for d in range(DIFF_DIAG):
            key = row + d * DIFF_TK
            allowed = (key // CHUNK) <= (col // CHUNK)
            dist = jnp.abs(col - key).astype(_F32)
            bias_scr[1 + d] = jnp.where(allowed, -slope2 * dist, -jnp.inf)
        bias_scr[DIFF_MASKED] = jnp.full((DIFF_TK, DIFF_TQ), -jnp.inf, _F32)

        def transpose_block(b, carry):
            vt_scr[b] = v_ref[pl.ds(pl.multiple_of(b * DIFF_TK, DIFF_TK), DIFF_TK), :].T
            return carry

        lax.fori_loop(0, SEQ // DIFF_TK, transpose_block, 0)

    q_t = q_ref[...].T
    dim = lax.broadcasted_iota(jnp.int32, q_t.shape, 0)
    zero = jnp.zeros_like(q_t)
    q_maps = (jnp.where(dim < HEAD_DIM, q_t, zero), jnp.where(dim >= HEAD_DIM, q_t, zero))

    m_scr[...] = jnp.full(m_scr.shape, _NEG, _F32)
    l_scr[...] = jnp.zeros(l_scr.shape, _F32)
    acc_scr[...] = jnp.zeros(acc_scr.shape, _F32)

    def tile_offset(t):
        gap = (i * DIFF_TQ - t * DIFF_TK).astype(_F32)
        return jnp.where(t < n_off, -slope2 * gap, 0.0)

    def stage_a(t, slot):
        tc = jnp.minimum(t, nb - 1)
        kb = k_ref[pl.ds(pl.multiple_of(tc * DIFF_TK, DIFF_TK), DIFF_TK), :]
        sel = jnp.where(t < n_off, 0, jnp.minimum(t - n_off + 1, DIFF_MASKED))
        bias = bias_scr[sel]
        offset = tile_offset(t)
        for c in range(2):
            s = jnp.dot(kb, q_maps[c], preferred_element_type=_F32) + bias
            s_scr[slot, c] = s
            mx_scr[slot, c] = jnp.max(s, axis=0, keepdims=True) + offset

    def stage_b(t, slot):
        offset = tile_offset(t)
        for c in range(2):
            m_old = m_scr[c]
            m_new = jnp.maximum(m_old, mx_scr[slot, c])
            alpha = jnp.exp2(m_old - m_new)
            p = jnp.exp2(s_scr[slot, c] - (m_new - offset))
            l_scr[c] = alpha * l_scr[c] + jnp.sum(p, axis=0, keepdims=True)
            p_scr[slot, c] = p.astype(_BF16)
            al_scr[slot, c] = alpha
            m_scr[c] = m_new

    def stage_c(t, slot):
        vt = vt_scr[jnp.minimum(t, nb - 1)]
        for c in range(2):
            pv = jnp.dot(vt, p_scr[slot, c], preferred_element_type=_F32)
            acc_scr[c] = al_scr[slot, c] * acc_scr[c] + pv

    stage_a(0, 0)
    stage_b(0, 0)
    stage_a(1, 1)

    def body(u, carry):
        t = 2 * u
        stage_c(t - 2, 0)
        stage_b(t - 1, 1)
        stage_a(t, 0)
        stage_c(t - 1, 1)
        stage_b(t, 0)
        stage_a(t + 1, 1)
        return carry

    lax.fori_loop(1, nb // 2 + 1, body, 0)

    lam_v = lam_ref[...]
    lam = (jnp.exp(jnp.sum(lam_v[0:1] * lam_v[1:2], axis=-1, keepdims=True))
           - jnp.exp(jnp.sum(lam_v[2:3] * lam_v[3:4], axis=-1, keepdims=True))
           + lam_init)
    o_t = acc_scr[0] / l_scr[0] - lam * (acc_scr[1] / l_scr[1])
    ms = jnp.mean(o_t * o_t, axis=0, keepdims=True)
    y = o_t * lax.rsqrt(ms + SUBLN_EPS) * gain_ref[...] * (1.0 - lam_init)
    o_ref[...] = y.T.astype(o_ref.dtype)


def _diff_attention(qkv, slopes, lam_vecs, gain_col, layer, lam_init):
    tile = (DIFF_TK, DIFF_TQ)
    return pl.pallas_call(
        functools.partial(_diff_kernel, lam_init=lam_init),
        grid=(N_HEADS, SEQ // DIFF_TQ),
        in_specs=[
            pl.BlockSpec(memory_space=pltpu.SMEM),
            pl.BlockSpec((None, 4, HEAD_DIM), lambda h, i: (layer, 0, 0)),
            pl.BlockSpec((None, LANES, 1), lambda h, i: (layer, 0, 0)),
            pl.BlockSpec((DIFF_TQ, LANES), lambda h, i: (i, QA + h)),
            pl.BlockSpec((SEQ, LANES), lambda h, i: (0, KA + h)),
            pl.BlockSpec((SEQ, LANES), lambda h, i: (0, VA + h)),
        ],
        out_specs=pl.BlockSpec((DIFF_TQ, LANES), lambda h, i: (i, h)),
        out_shape=jax.ShapeDtypeStruct((SEQ, DIFF_WIDTH), _BF16),
        scratch_shapes=[
            pltpu.VMEM((DIFF_MASKED + 1,) + tile, _F32),
            pltpu.VMEM((SEQ // DIFF_TK, LANES, DIFF_TK), _BF16),
            pltpu.VMEM((2, 2) + tile, _F32),
            pltpu.VMEM((2, 2, 1, DIFF_TQ), _F32),
            pltpu.VMEM((2, 2) + tile, _BF16),
            pltpu.VMEM((2, 2, 1, DIFF_TQ), _F32),
            pltpu.VMEM((2, 1, DIFF_TQ), _F32),
            pltpu.VMEM((2, 1, DIFF_TQ), _F32),
            pltpu.VMEM((2, LANES, DIFF_TQ), _F32),
        ],
        compiler_params=_cparams(2),
        name="diff_attention",
    )(slopes, lam_vecs, gain_col, qkv, qkv, qkv)


SB_TQ = 512
SB_TK = 256
SB_DIAG = SB_TQ // SB_TK
SB_MASKED = SB_DIAG + 1


def _sb_kernel(gain_ref, q_ref, k_ref, v_ref, o_ref,
               mask_scr, tri_scr, vt_scr, zs_scr, hl_scr, ns_scr, a_scr, r_scr, acc_scr):
    i = pl.program_id(1)
    nb = (i + 1) * SB_DIAG

    @pl.when(i == 0)
    def _():
        row = lax.broadcasted_iota(jnp.int32, (SB_TK, SB_TQ), 0)
        col = lax.broadcasted_iota(jnp.int32, (SB_TK, SB_TQ), 1)
        mask_scr[0] = jnp.zeros((SB_TK, SB_TQ), _F32)
        for d in range(SB_DIAG):
            mask_scr[1 + d] = jnp.where(row + d * SB_TK < col, 0.0, -jnp.inf)
        mask_scr[SB_MASKED] = jnp.full((SB_TK, SB_TQ), -jnp.inf, _F32)
        r2 = lax.broadcasted_iota(jnp.int32, (SB_TK, 2 * SB_TK), 0)
        c2 = lax.broadcasted_iota(jnp.int32, (SB_TK, 2 * SB_TK), 1) % SB_TK
        tri_scr[...] = jnp.where(c2 > r2, 1.0, 0.0).astype(_BF16)

        def transpose_block(b, carry):
            vt_scr[b] = v_ref[pl.ds(pl.multiple_of(b * SB_TK, SB_TK), SB_TK), :].T
            return carry

        lax.fori_loop(0, SEQ // SB_TK, transpose_block, 0)

    q_t = q_ref[...].T
    dim = lax.broadcasted_iota(jnp.int32, q_t.shape, 0)
    zero = jnp.zeros_like(q_t)
    q_heads = (jnp.where(dim < HEAD_DIM, q_t, zero), jnp.where(dim >= HEAD_DIM, q_t, zero))

    r_scr[...] = jnp.zeros(r_scr.shape, _F32)
    acc_scr[...] = jnp.zeros(acc_scr.shape, _F32)

    def key_block(t):
        return jnp.maximum(nb - 1 - t, 0)

    def stage_a(t, slot):
        kb = k_ref[pl.ds(pl.multiple_of(key_block(t) * SB_TK, SB_TK), SB_TK), :]
        sel = jnp.where(t >= nb, SB_MASKED, jnp.maximum(SB_DIAG - t, 0))
        mask = mask_scr[sel]
        for hh in range(2):
            z = jnp.dot(kb, q_heads[hh], preferred_element_type=_F32) + mask
            nl = jnp.maximum(z, 0.0) + jnp.log(1.0 + jnp.exp(-jnp.abs(z)))
            hi = nl.astype(_BF16)
            zs_scr[slot, hh] = z - nl
            hl_scr[slot, hh, :SB_TK] = hi
            hl_scr[slot, hh, SB_TK:] = (nl - hi.astype(_F32)).astype(_BF16)
            ns_scr[slot, hh] = jnp.sum(nl, axis=0, keepdims=True)

    def stage_b(slot):
        tri = tri_scr[...]
        for hh in range(2):
            suffix = jnp.dot(tri, hl_scr[slot, hh], preferred_element_type=_F32)
            r_old = r_scr[hh]
            a_scr[slot, hh] = jnp.exp(zs_scr[slot, hh] - suffix - r_old).astype(_BF16)
            r_scr[hh] = r_old + ns_scr[slot, hh]

    def stage_c(t, slot):
        vt = vt_scr[key_block(t)]
        for hh in range(2):
            acc_scr[hh] += jnp.dot(vt[hh * HEAD_DIM:(hh + 1) * HEAD_DIM], a_scr[slot, hh],
                                   preferred_element_type=_F32)

    stage_a(0, 0)
    stage_b(0)
    stage_a(1, 1)

    def body(u, carry):
        t = 2 * u
        stage_c(t - 2, 0)
        stage_b(1)
        stage_a(t, 0)
        stage_c(t - 1, 1)
        stage_b(0)
        stage_a(t + 1, 1)
        return carry

    lax.fori_loop(1, nb // 2 + 1, body, 0)

    o_t = jnp.concatenate([acc_scr[0], acc_scr[1]], axis=0)
    _pair_rmsnorm_store(o_t, gain_ref[...], o_ref)


def _sb_attention(qkv, gain_col, layer):
    tile = (SB_TK, SB_TQ)
    return pl.pallas_call(
        _sb_kernel,
        grid=(N_HEADS // 2, SEQ // SB_TQ),
        in_specs=[
            pl.BlockSpec((None, None, LANES, 1), lambda p, i: (layer, p, 0, 0)),
            pl.BlockSpec((SB_TQ, LANES), lambda p, i: (i, QB + p)),
            pl.BlockSpec((SEQ, LANES), lambda p, i: (0, KB + p)),
            pl.BlockSpec((SEQ, LANES), lambda p, i: (0, VB + p)),
        ],
        out_specs=pl.BlockSpec((SB_TQ, LANES), lambda p, i: (i, p)),
        out_shape=jax.ShapeDtypeStruct((SEQ, SB_WIDTH), _BF16),
        scratch_shapes=[
            pltpu.VMEM((SB_MASKED + 1,) + tile, _F32),
            pltpu.VMEM((SB_TK, 2 * SB_TK), _BF16),
            pltpu.VMEM((SEQ // SB_TK, LANES, SB_TK), _BF16),
            pltpu.VMEM((2, 2) + tile, _F32),
            pltpu.VMEM((2, 2, 2 * SB_TK, SB_TQ), _BF16),
            pltpu.VMEM((2, 2, 1, SB_TQ), _F32),
            pltpu.VMEM((2, 2) + tile, _BF16),
            pltpu.VMEM((2, 1, SB_TQ), _F32),
            pltpu.VMEM((2, HEAD_DIM, SB_TQ), _F32),
        ],
        compiler_params=_cparams(2),
        name="sb_attention",
    )(gain_col, qkv, qkv, qkv)


BAND_TILES = 1 + (LEFT_CHUNKS * CHUNK) // ATT_T


def _band_kernel(bias_ref, gain_ref, q_ref, k_ref, v_ref, o_ref):
    i = pl.program_id(1)
    q_heads = _split_heads(q_ref)
    rows = lax.broadcasted_iota(jnp.int32, (LANES, ATT_T), 0)
    outs = []
    for hh in range(2):
        scores, vals = [], []
        for kk in range(BAND_TILES):
            jb = i - (BAND_TILES - 1) + kk
            jc = jnp.maximum(jb, 0)
            s = lax.dot_general(_key_block(k_ref, jc), q_heads[hh], _NT,
                                preferred_element_type=_F32) + bias_ref[hh, kk]
            scores.append(jnp.where(jb >= 0, s, -jnp.inf))
            vals.append(_key_block(v_ref, jc))
        m = functools.reduce(jnp.maximum,
                             [jnp.max(s, axis=0, keepdims=True) for s in scores])
        probs = [jnp.exp2(s - m) for s in scores]
        denom = functools.reduce(jnp.add,
                                 [jnp.sum(p, axis=0, keepdims=True) for p in probs])
        o_t = functools.reduce(jnp.add, [
            lax.dot_general(v, p.astype(_BF16), _TN, preferred_element_type=_F32)
            for v, p in zip(vals, probs)])
        outs.append(o_t / denom)
    o_t = jnp.where(rows < HEAD_DIM, outs[0], outs[1])
    _pair_rmsnorm_store(o_t, gain_ref[...], o_ref)


def _band_bias_tiles(rel_table):
    t = ATT_T
    key = np.arange(t)[:, None]
    qry = np.arange(t)[None, :]
    shift = np.arange(2 * t + 1) - t
    vec_idx, mask = [], []
    for kk in range(BAND_TILES):
        back = (BAND_TILES - 1 - kk) * t
        vec_idx.append(np.clip(back + shift, -(CHUNK - 1), MAX_REL) + (CHUNK - 1))
        dc = back // CHUNK + qry // CHUNK - key // CHUNK
        mask.append((dc >= 0) & (dc <= LEFT_CHUNKS))
    vec = rel_table[:, np.stack(vec_idx)] * LOG2E
    toe = jnp.tile(vec, (1, 1, t))[..., :t * 2 * t].reshape(N_HEADS, BAND_TILES, t, 2 * t)[..., t:]
    tiles = jnp.where(np.stack(mask), toe, -jnp.inf)
    return tiles.reshape(N_HEADS // 2, 2, BAND_TILES, t, t)


def _band_attention(qkv, bias_tiles, gain_col, layer):
    return pl.pallas_call(
        _band_kernel,
        grid=(N_HEADS // 2, N_QBLK),
        in_specs=[
            pl.BlockSpec((None, 2, BAND_TILES, ATT_T, ATT_T), lambda p, i: (p, 0, 0, 0, 0)),
            pl.BlockSpec((None, None, LANES, 1), lambda p, i: (layer, p, 0, 0)),
            pl.BlockSpec((ATT_T, LANES), lambda p, i: (i, QC + p)),
            pl.BlockSpec((SEQ, LANES), lambda p, i: (0, KC + p)),
            pl.BlockSpec((SEQ, LANES), lambda p, i: (0, VC + p)),
        ],
        out_specs=pl.BlockSpec((ATT_T, LANES), lambda p, i: (i, p)),
        out_shape=jax.ShapeDtypeStruct((SEQ, CH_WIDTH), _BF16),
        compiler_params=_cparams(2),
        name="band_attention",
    )(bias_tiles, gain_col, qkv, qkv, qkv)


def _oproj_kernel(y_ref, w_ref, x_ref, o_ref):
    o_ref[...] = x_ref[...] + jnp.dot(y_ref[...], w_ref[...], preferred_element_type=_F32)


def _out_proj(y, w_all, x, layer, tm=1024, tn=512):
    return pl.pallas_call(
        _oproj_kernel,
        grid=(SEQ // tm, D_MODEL // tn),
        in_specs=[
            pl.BlockSpec((tm, D_MODEL), lambda i, j: (i, 0)),
            pl.BlockSpec((None, D_MODEL, tn), lambda i, j: (layer, 0, j)),
            pl.BlockSpec((tm, tn), lambda i, j: (i, j)),
        ],
        out_specs=pl.BlockSpec((tm, tn), lambda i, j: (i, j)),
        out_shape=jax.ShapeDtypeStruct((SEQ, D_MODEL), _F32),
        compiler_params=_cparams(2),
        name="out_proj",
    )(y, w_all, x)


def _ffn_kernel(x_ref, halo_ref, g_ref, wg_ref, wu_ref, cw_ref, cb_ref, wd_ref, o_ref,
                h_scr, gate_scr, *, tm):
    i = pl.program_id(0)
    j = pl.program_id(1)

    def norm(x):
        ms = jnp.mean(x * x, axis=-1, keepdims=True)
        return (x * lax.rsqrt(ms + NORM_EPS) * g_ref[...]).astype(_BF16)

    @pl.when(j == 0)
    def _():
        x = x_ref[...]
        h_scr[HALO:, :] = norm(x)
        halo = jnp.where(i > 0, halo_ref[...], 0.0)
        h_scr[:HALO, :] = norm(halo)
        o_ref[...] = x

    gate_scr[...] = jnp.dot(h_scr[...], wg_ref[...], preferred_element_type=_F32)
    up = jnp.dot(h_scr[HALO:, :], wu_ref[...], preferred_element_type=_F32)
    cw = cw_ref[...]
    g = cb_ref[...]
    for t in range(CONV_WIDTH):
        lo = HALO - (CONV_WIDTH - 1) + t
        g = g + gate_scr[lo:lo + tm, :] * cw[t:t + 1]
    act = (g * jax.nn.sigmoid(g) * up).astype(_BF16)
    o_ref[...] += jnp.dot(act, wd_ref[...], preferred_element_type=_F32)


def _ffn(x, g_all, wg_all, wu_all, cw_all, cb_all, wd_all, layer, tm=512, tf=512):
    halo_blocks = tm // HALO
    return pl.pallas_call(
        functools.partial(_ffn_kernel, tm=tm),
        grid=(SEQ // tm, D_FF // tf),
        in_specs=[
            pl.BlockSpec((tm, D_MODEL), lambda i, j: (i, 0)),
            pl.BlockSpec((HALO, D_MODEL),
                         lambda i, j: (jnp.maximum(i * halo_blocks - 1, 0), 0)),
            pl.BlockSpec((None, 1, D_MODEL), lambda i, j: (layer, 0, 0)),
            pl.BlockSpec((None, D_MODEL, tf), lambda i, j: (layer, 0, j)),
            pl.BlockSpec((None, D_MODEL, tf), lambda i, j: (layer, 0, j)),
            pl.BlockSpec((None, CONV_WIDTH, tf), lambda i, j: (layer, 0, j)),
            pl.BlockSpec((None, 1, tf), lambda i, j: (layer, 0, j)),
            pl.BlockSpec((None, tf, D_MODEL), lambda i, j: (layer, j, 0)),
        ],
        out_specs=pl.BlockSpec((tm, D_MODEL), lambda i, j: (i, 0)),
        out_shape=jax.ShapeDtypeStruct((SEQ, D_MODEL), _F32),
        scratch_shapes=[
            pltpu.VMEM((tm + HALO, D_MODEL), _BF16),
            pltpu.VMEM((tm + HALO, tf), _F32),
        ],
        compiler_params=_cparams(2),
        name="gated_mlp",
    )(x, x, g_all, wg_all, wu_all, cw_all, cb_all, wd_all)


def _final_norm_kernel(x_ref, g_ref, o_ref):
    x = x_ref[...]
    ms = jnp.mean(x * x, axis=-1, keepdims=True)
    o_ref[...] = x * lax.rsqrt(ms + NORM_EPS) * g_ref[...]


def _final_norm(x, g, tm=512):
    return pl.pallas_call(
        _final_norm_kernel,
        grid=(SEQ // tm,),
        in_specs=[
            pl.BlockSpec((tm, D_MODEL), lambda i: (i, 0)),
            pl.BlockSpec((1, D_MODEL), lambda i: (0, 0)),
        ],
        out_specs=pl.BlockSpec((tm, D_MODEL), lambda i: (i, 0)),
        out_shape=jax.ShapeDtypeStruct((SEQ, D_MODEL), _F32),
        compiler_params=_cparams(1),
        name="final_norm",
    )(x, g)


def kernel(x, attn_norm, w_qkv, lambda_q1, lambda_k1, lambda_q2, lambda_k2, diff_subln,
           sb_norm, rel_bias, ch_norm, w_o, ffn_norm, w_gate, w_up, conv_w, conv_b, w_down,
           final_norm):
    batch, seq, d_model = x.shape
    assert (batch, seq, d_model) == (1, SEQ, D_MODEL)
    x2 = x.reshape(SEQ, D_MODEL)

    w_qkv_b = w_qkv.astype(_BF16)
    w_o_b = w_o.astype(_BF16)
    w_gate_b = w_gate.astype(_BF16)
    w_up_b = w_up.astype(_BF16)
    w_down_b = w_down.astype(_BF16)

    attn_g = attn_norm.reshape(DEPTH, 1, D_MODEL)
    ffn_g = ffn_norm.reshape(DEPTH, 1, D_MODEL)
    conv_b3 = conv_b.reshape(DEPTH, 1, D_FF)
    lam_vecs = jnp.stack([lambda_q1, lambda_k1, lambda_q2, lambda_k2], axis=1)
    subln_col = diff_subln.reshape(DEPTH, 2 * HEAD_DIM, 1)
    sb_col = sb_norm.reshape(DEPTH, N_HEADS // 2, 2 * HEAD_DIM, 1)
    ch_col = ch_norm.reshape(DEPTH, N_HEADS // 2, 2 * HEAD_DIM, 1)
    slopes = 2.0 ** (-8.0 * jnp.arange(1, N_HEADS + 1, dtype=_F32) / N_HEADS)
    col_scale = jnp.asarray(_qkv_col_scale())

    for layer in range(DEPTH):
        lam_init = 0.8 - 0.6 * math.exp(-0.3 * layer)
        qkv = _qkv_proj(x2, attn_g, w_qkv_b, col_scale, layer)
        y_a = _diff_attention(qkv, slopes, lam_vecs, subln_col, layer, lam_init)
        y_b = _sb_attention(qkv, sb_col, layer)
        y_c = _band_attention(qkv, _band_bias_tiles(rel_bias[layer]), ch_col, layer)
        y = jnp.concatenate([y_a, y_b, y_c], axis=-1)
        x2 = _out_proj(y, w_o_b, x2, layer)
        x2 = _ffn(x2, ffn_g, w_gate_b, w_up_b, conv_w, conv_b3, w_down_b, layer)
    out = _final_norm(x2, final_norm.reshape(1, D_MODEL))
    return out.reshape(1, SEQ, D_MODEL)
```

```python
import functools
import math

import jax
import jax.numpy as jnp
import numpy as np
from jax import lax
from jax.experimental import pallas as pl
from jax.experimental.pallas import tpu as pltpu

D_MODEL = 2048
SEQ = 8192
DEPTH = 4
CHUNK = 64
HEAD_DIM = 64
N_HEADS = 8
DIFF_WIDTH = 1024
SB_WIDTH = 512
CH_WIDTH = 512
QKV_WIDTH = 6144
LEFT_CHUNKS = 8
MAX_REL = 128
D_FF = 5632
CONV_WIDTH = 3
NORM_EPS = 1e-6
SUBLN_EPS = 1e-5
SCALE = HEAD_DIM ** -0.5
LOG2E = math.log2(math.e)

LANES = 128
ATT_T = 256
N_QBLK = SEQ // ATT_T
HALO = 16
VMEM_LIMIT = 48 * 1024 * 1024

QA, KA, VA = 0, 8, 16
QB, KB, VB = 24, 28, 32
QC, KC, VC = 36, 40, 44

_NT = (((1,), (1,)), ((), ()))
_TN = (((0,), (0,)), ((), ()))
_F32 = jnp.float32
_BF16 = jnp.bfloat16
_NEG = -1e30


def _cparams(n_axes):
    return pltpu.CompilerParams(
        dimension_semantics=("arbitrary",) * n_axes,
        vmem_limit_bytes=VMEM_LIMIT)


def _qkv_kernel(x_ref, g_ref, w_ref, cs_ref, o_ref, h_scr):
    @pl.when(pl.program_id(1) == 0)
    def _():
        x = x_ref[...]
        ms = jnp.mean(x * x, axis=-1, keepdims=True)
        h_scr[...] = (x * lax.rsqrt(ms + NORM_EPS) * g_ref[...]).astype(_BF16)

    acc = jnp.dot(h_scr[...], w_ref[...], preferred_element_type=_F32)
    o_ref[...] = (acc * cs_ref[...]).astype(_BF16)


def _qkv_col_scale():
    cs = np.ones((1, QKV_WIDTH), np.float32)
    for q_start, k_start in ((QA, KA), (QB, KB), (QC, KC)):
        cs[:, q_start * LANES:k_start * LANES] = SCALE * LOG2E
    return cs


def _qkv_proj(x, g_all, w_all, col_scale, layer, tm=1024, tn=1024):
    return pl.pallas_call(
        _qkv_kernel,
        grid=(SEQ // tm, QKV_WIDTH // tn),
        in_specs=[
            pl.BlockSpec((tm, D_MODEL), lambda i, j: (i, 0)),
            pl.BlockSpec((None, 1, D_MODEL), lambda i, j: (layer, 0, 0)),
            pl.BlockSpec((None, D_MODEL, tn), lambda i, j: (layer, 0, j)),
            pl.BlockSpec((1, tn), lambda i, j: (0, j)),
        ],
        out_specs=pl.BlockSpec((tm, tn), lambda i, j: (i, j)),
        out_shape=jax.ShapeDtypeStruct((SEQ, QKV_WIDTH), _BF16),
        scratch_shapes=[pltpu.VMEM((tm, D_MODEL), _BF16)],
        compiler_params=_cparams(2),
        name="qkv_proj",
    )(x, g_all, w_all, col_scale)


def _split_heads(q_ref):
    q = q_ref[...]
    lane = lax.broadcasted_iota(jnp.int32, q.shape, 1)
    zero = jnp.zeros_like(q)
    return jnp.where(lane < HEAD_DIM, q, zero), jnp.where(lane >= HEAD_DIM, q, zero)


def _key_block(ref, j):
    return ref[pl.ds(pl.multiple_of(j * ATT_T, ATT_T), ATT_T), :]


def _pair_rmsnorm_store(o_t, gain_col, o_ref):
    top, bot = o_t[:HEAD_DIM], o_t[HEAD_DIM:]
    top = top * lax.rsqrt(jnp.mean(top * top, axis=0, keepdims=True) + NORM_EPS)
    bot = bot * lax.rsqrt(jnp.mean(bot * bot, axis=0, keepdims=True) + NORM_EPS)
    y = jnp.concatenate([top, bot], axis=0) * gain_col
    o_ref[...] = y.T.astype(o_ref.dtype)


DIFF_TQ = 512
DIFF_TK = 256
DIFF_DIAG = DIFF_TQ // DIFF_TK
assert DIFF_DIAG == 2


def _diff_kernel(slope_ref, lam_ref, gain_ref, q_ref, k_ref, v_ref, o_ref,
                 bias_scr, vt_scr, s_scr, mx_scr, p_scr, al_scr, m_scr, l_scr, acc_scr,
                 *, lam_init):
    h = pl.program_id(0)
    i = pl.program_id(1)
    slope2 = slope_ref[h] * LOG2E
    n_off = i * DIFF_DIAG
    nb = n_off + DIFF_DIAG

    @pl.when(i == 0)
    def _():
        row = lax.broadcasted_iota(jnp.int32, (DIFF_TK, DIFF_TQ), 0)
        col = lax.broadcasted_iota(jnp.int32, (DIFF_TK, DIFF_TQ), 1)
        bias_scr[0] = -slope2 * (col - row).astype(_F32)
        for d in range(DIFF_DIAG):
            key = row + d * DIFF_TK
            allowed = (key // CHUNK) <= (col // CHUNK)
            dist = jnp.abs(col - key).astype(_F32)
            bias_scr[1 + d] = jnp.where(allowed, -slope2 * dist, -jnp.inf)

        def transpose_block(b, carry):
            vt_scr[b] = v_ref[pl.ds(pl.multiple_of(b * DIFF_TK, DIFF_TK), DIFF_TK), :].T
            return carry

        lax.fori_loop(0, SEQ // DIFF_TK, transpose_block, 0)

    q_t = q_ref[...].T
    dim = lax.broadcasted_iota(jnp.int32, q_t.shape, 0)
    zero = jnp.zeros_like(q_t)
    q_maps = (jnp.where(dim < HEAD_DIM, q_t, zero), jnp.where(dim >= HEAD_DIM, q_t, zero))

    m_scr[...] = jnp.full(m_scr.shape, _NEG, _F32)
    l_scr[...] = jnp.zeros(l_scr.shape, _F32)
    acc_scr[...] = jnp.zeros(acc_scr.shape, _F32)

    def tile_offset(t):
        gap = (i * DIFF_TQ - t * DIFF_TK).astype(_F32)
        return jnp.where(t < n_off, -slope2 * gap, 0.0)

    def stage_a(t, slot):
        kb = k_ref[pl.ds(pl.multiple_of(t * DIFF_TK, DIFF_TK), DIFF_TK), :]
        bias = bias_scr[jnp.maximum(t - n_off + 1, 0)]
        offset = tile_offset(t)
        for c in range(2):
            s = jnp.dot(kb, q_maps[c], preferred_element_type=_F32) + bias
            s_scr[slot, c] = s
            mx_scr[slot, c] = jnp.max(s, axis=0, keepdims=True) + offset

    def stage_b(t, slot):
        offset = tile_offset(t)
        for c in range(2):
            m_old = m_scr[c]
            m_new = jnp.maximum(m_old, mx_scr[slot, c])
            alpha = jnp.exp2(m_old - m_new)
            p = jnp.exp2(s_scr[slot, c] - (m_new - offset))
            l_scr[c] = alpha * l_scr[c] + jnp.sum(p, axis=0, keepdims=True)
            p_scr[slot, c] = p.astype(_BF16)
            al_scr[slot, c] = alpha
            m_scr[c] = m_new

    def stage_c(t, slot):
        vt = vt_scr[t]
        for c in range(2):
            pv = jnp.dot(vt, p_scr[slot, c], preferred_element_type=_F32)
            acc_scr[c] = al_scr[slot, c] * acc_scr[c] + pv

    stage_a(0, 0)
    stage_b(0, 0)
    stage_a(1, 1)

    def body(u, carry):
        t = 2 * u
        stage_c(t - 2, 0)
        stage_b(t - 1, 1)
        stage_a(t, 0)
        stage_c(t - 1, 1)
        stage_b(t, 0)
        stage_a(t + 1, 1)
        return carry

    lax.fori_loop(1, nb // 2, body, 0)
    stage_c(nb - 2, 0)
    stage_b(nb - 1, 1)
    stage_c(nb - 1, 1)

    lam_v = lam_ref[...]
    lam = (jnp.exp(jnp.sum(lam_v[0:1] * lam_v[1:2], axis=-1, keepdims=True))
           - jnp.exp(jnp.sum(lam_v[2:3] * lam_v[3:4], axis=-1, keepdims=True))
           + lam_init)
    o_t = acc_scr[0] / l_scr[0] - lam * (acc_scr[1] / l_scr[1])
    ms = jnp.mean(o_t * o_t, axis=0, keepdims=True)
    y = o_t * lax.rsqrt(ms + SUBLN_EPS) * gain_ref[...] * (1.0 - lam_init)
    o_ref[...] = y.T.astype(o_ref.dtype)


def _diff_attention(qkv, slopes, lam_vecs, gain_col, layer, lam_init):
    tile = (DIFF_TK, DIFF_TQ)
    return pl.pallas_call(
        functools.partial(_diff_kernel, lam_init=lam_init),
        grid=(N_HEADS, SEQ // DIFF_TQ),
        in_specs=[
            pl.BlockSpec(memory_space=pltpu.SMEM),
            pl.BlockSpec((None, 4, HEAD_DIM), lambda h, i: (layer, 0, 0)),
            pl.BlockSpec((None, LANES, 1), lambda h, i: (layer, 0, 0)),
            pl.BlockSpec((DIFF_TQ, LANES), lambda h, i: (i, QA + h)),
            pl.BlockSpec((SEQ, LANES), lambda h, i: (0, KA + h)),
            pl.BlockSpec((SEQ, LANES), lambda h, i: (0, VA + h)),
        ],
        out_specs=pl.BlockSpec((DIFF_TQ, LANES), lambda h, i: (i, h)),
        out_shape=jax.ShapeDtypeStruct((SEQ, DIFF_WIDTH), _BF16),
        scratch_shapes=[
            pltpu.VMEM((DIFF_DIAG + 1,) + tile, _F32),
            pltpu.VMEM((SEQ // DIFF_TK, LANES, DIFF_TK), _BF16),
            pltpu.VMEM((2, 2) + tile, _F32),
            pltpu.VMEM((2, 2, 1, DIFF_TQ), _F32),
            pltpu.VMEM((2, 2) + tile, _BF16),
            pltpu.VMEM((2, 2, 1, DIFF_TQ), _F32),
            pltpu.VMEM((2, 1, DIFF_TQ), _F32),
            pltpu.VMEM((2, 1, DIFF_TQ), _F32),
            pltpu.VMEM((2, LANES, DIFF_TQ), _F32),
        ],
        compiler_params=_cparams(2),
        name="diff_attention",
    )(slopes, lam_vecs, gain_col, qkv, qkv, qkv)


SB_TQ = 512
SB_TK = 256
SB_SUB = 128
SB_DIAG = SB_TQ // SB_TK
assert SB_DIAG == 2


def _sb_kernel(gain_ref, q_ref, k_ref, v_ref, o_ref,
               mask_scr, tri_scr, vt_scr, z_scr, hl_scr, a_scr, r_scr, acc_scr):
    i = pl.program_id(1)
    nb = (i + 1) * SB_DIAG

    @pl.when(i == 0)
    def _():
        row = lax.broadcasted_iota(jnp.int32, (SB_TK, SB_TQ), 0)
        col = lax.broadcasted_iota(jnp.int32, (SB_TK, SB_TQ), 1)
        for d in range(SB_DIAG):
            mask_scr[d] = jnp.where(row + d * SB_TK < col, 0.0, -jnp.inf)
        r2 = lax.broadcasted_iota(jnp.int32, (SB_SUB, 2 * SB_SUB), 0)
        c2 = lax.broadcasted_iota(jnp.int32, (SB_SUB, 2 * SB_SUB), 1) % SB_SUB
        tri_scr[...] = jnp.where(c2 >= r2, 1.0, 0.0).astype(_BF16)

        def transpose_block(b, carry):
            vt_scr[b] = v_ref[pl.ds(pl.multiple_of(b * SB_TK, SB_TK), SB_TK), :].T
            return carry

        lax.fori_loop(0, SEQ // SB_TK, transpose_block, 0)

    q_t = q_ref[...].T
    dim = lax.broadcasted_iota(jnp.int32, q_t.shape, 0)
    zero = jnp.zeros_like(q_t)
    q_heads = (jnp.where(dim < HEAD_DIM, q_t, zero), jnp.where(dim >= HEAD_DIM, q_t, zero))

    r_scr[...] = jnp.zeros(r_scr.shape, _F32)
    acc_scr[...] = jnp.zeros(acc_scr.shape, _F32)

    def key_block(t):
        return nb - 1 - t

    def stage_a(t, slot, diagonal=None):
        kb = k_ref[pl.ds(pl.multiple_of(key_block(t) * SB_TK, SB_TK), SB_TK), :]
        for hh in range(2):
            z = jnp.dot(kb, q_heads[hh], preferred_element_type=_F32)
            if diagonal is not None:
                z = z + mask_scr[diagonal]
            nl = jnp.maximum(z, 0.0) + jnp.log2(1.0 + jnp.exp2(-jnp.abs(z)))
            hi = nl.astype(_BF16)
            lo = (nl - hi.astype(_F32)).astype(_BF16)
            z_scr[slot, hh] = z
            for sub in range(SB_TK // SB_SUB):
                rows = slice(sub * SB_SUB, (sub + 1) * SB_SUB)
                hl_scr[slot, hh, sub, :SB_SUB] = hi[rows]
                hl_scr[slot, hh, sub, SB_SUB:] = lo[rows]

    def stage_b(slot):
        tri = tri_scr[...]
        for hh in range(2):
            right = r_scr[hh]
            for sub in reversed(range(SB_TK // SB_SUB)):
                rows = slice(sub * SB_SUB, (sub + 1) * SB_SUB)
                incl = jnp.dot(tri, hl_scr[slot, hh, sub], preferred_element_type=_F32)
                a_scr[slot, hh, rows] = jnp.exp2(z_scr[slot, hh, rows] - incl - right).astype(_BF16)
                right = right + incl[0:1]
            r_scr[hh] = right

    def stage_c(t, slot):
        vt = vt_scr[key_block(t)]
        for hh in range(2):
            acc_scr[hh] += jnp.dot(vt[hh * HEAD_DIM:(hh + 1) * HEAD_DIM], a_scr[slot, hh],
                                   preferred_element_type=_F32)

    stage_a(0, 0, diagonal=1)
    stage_b(0)
    stage_a(1, 1, diagonal=0)

    def body(u, carry):
        t = 2 * u
        stage_c(t - 2, 0)
        stage_b(1)
        stage_a(t, 0)
        stage_c(t - 1, 1)
        stage_b(0)
        stage_a(t + 1, 1)
        return carry

    lax.fori_loop(1, nb // 2, body, 0)
    stage_c(nb - 2, 0)
    stage_b(1)
    stage_c(nb - 1, 1)

    o_t = jnp.concatenate([acc_scr[0], acc_scr[1]], axis=0)
    _pair_rmsnorm_store(o_t, gain_ref[...], o_ref)


def _sb_attention(qkv, gain_col, layer):
    tile = (SB_TK, SB_TQ)
    return pl.pallas_call(
        _sb_kernel,
        grid=(N_HEADS // 2, SEQ // SB_TQ),
        in_specs=[
            pl.BlockSpec((None, None, LANES, 1), lambda p, i: (layer, p, 0, 0)),
            pl.BlockSpec((SB_TQ, LANES), lambda p, i: (i, QB + p)),
            pl.BlockSpec((SEQ, LANES), lambda p, i: (0, KB + p)),
            pl.BlockSpec((SEQ, LANES), lambda p, i: (0, VB + p)),
        ],
        out_specs=pl.BlockSpec((SB_TQ, LANES), lambda p, i: (i, p)),
        out_shape=jax.ShapeDtypeStruct((SEQ, SB_WIDTH), _BF16),
        scratch_shapes=[
            pltpu.VMEM((SB_DIAG,) + tile, _F32),
            pltpu.VMEM((SB_SUB, 2 * SB_SUB), _BF16),
            pltpu.VMEM((SEQ // SB_TK, LANES, SB_TK), _BF16),
            pltpu.VMEM((2, 2) + tile, _F32),
            pltpu.VMEM((2, 2, SB_TK // SB_SUB, 2 * SB_SUB, SB_TQ), _BF16),
            pltpu.VMEM((2, 2) + tile, _BF16),
            pltpu.VMEM((2, 1, SB_TQ), _F32),
            pltpu.VMEM((2, HEAD_DIM, SB_TQ), _F32),
        ],
        compiler_params=_cparams(2),
        name="sb_attention",
    )(gain_col, qkv, qkv, qkv)


BAND_TILES = 1 + (LEFT_CHUNKS * CHUNK) // ATT_T


def _band_kernel(bias_ref, gain_ref, q_ref, k_ref, v_ref, o_ref):
    i = pl.program_id(1)
    q_heads = _split_heads(q_ref)
    rows = lax.broadcasted_iota(jnp.int32, (LANES, ATT_T), 0)
    outs = []
    for hh in range(2):
        scores, vals = [], []
        for kk in range(BAND_TILES):
            jb = i - (BAND_TILES - 1) + kk
            jc = jnp.maximum(jb, 0)
            s = lax.dot_general(_key_block(k_ref, jc), q_heads[hh], _NT,
                                preferred_element_type=_F32) + bias_ref[hh, kk]
            scores.append(jnp.where(jb >= 0, s, -jnp.inf))
            vals.append(_key_block(v_ref, jc))
        m = functools.reduce(jnp.maximum,
                             [jnp.max(s, axis=0, keepdims=True) for s in scores])
        probs = [jnp.exp2(s - m) for s in scores]
        denom = functools.reduce(jnp.add,
                                 [jnp.sum(p, axis=0, keepdims=True) for p in probs])
        o_t = functools.reduce(jnp.add, [
            lax.dot_general(v, p.astype(_BF16), _TN, preferred_element_type=_F32)
            for v, p in zip(vals, probs)])
        outs.append(o_t / denom)
    o_t = jnp.where(rows < HEAD_DIM, outs[0], outs[1])
    _pair_rmsnorm_store(o_t, gain_ref[...], o_ref)


def _band_bias_tiles(rel_table):
    t = ATT_T
    key = np.arange(t)[:, None]
    qry = np.arange(t)[None, :]
    shift = np.arange(2 * t + 1) - t
    vec_idx, mask = [], []
    for kk in range(BAND_TILES):
        back = (BAND_TILES - 1 - kk) * t
        vec_idx.append(np.clip(back + shift, -(CHUNK - 1), MAX_REL) + (CHUNK - 1))
        dc = back // CHUNK + qry // CHUNK - key // CHUNK
        mask.append((dc >= 0) & (dc <= LEFT_CHUNKS))
    vec = rel_table[:, np.stack(vec_idx)] * LOG2E
    toe = jnp.tile(vec, (1, 1, t))[..., :t * 2 * t].reshape(N_HEADS, BAND_TILES, t, 2 * t)[..., t:]
    tiles = jnp.where(np.stack(mask), toe, -jnp.inf)
    return tiles.reshape(N_HEADS // 2, 2, BAND_TILES, t, t)


def _band_attention(qkv, bias_tiles, gain_col, layer):
    return pl.pallas_call(
        _band_kernel,
        grid=(N_HEADS // 2, N_QBLK),
        in_specs=[
            pl.BlockSpec((None, 2, BAND_TILES, ATT_T, ATT_T), lambda p, i: (p, 0, 0, 0, 0)),
            pl.BlockSpec((None, None, LANES, 1), lambda p, i: (layer, p, 0, 0)),
            pl.BlockSpec((ATT_T, LANES), lambda p, i: (i, QC + p)),
            pl.BlockSpec((SEQ, LANES), lambda p, i: (0, KC + p)),
            pl.BlockSpec((SEQ, LANES), lambda p, i: (0, VC + p)),
        ],
        out_specs=pl.BlockSpec((ATT_T, LANES), lambda p, i: (i, p)),
        out_shape=jax.ShapeDtypeStruct((SEQ, CH_WIDTH), _BF16),
        compiler_params=_cparams(2),
        name="band_attention",
    )(bias_tiles, gain_col, qkv, qkv, qkv)


def _oproj_kernel(y_ref, w_ref, x_ref, o_ref):
    o_ref[...] = x_ref[...] + jnp.dot(y_ref[...], w_ref[...], preferred_element_type=_F32)


def _out_proj(y, w_all, x, layer, tm=1024, tn=512):
    return pl.pallas_call(
        _oproj_kernel,
        grid=(SEQ // tm, D_MODEL // tn),
        in_specs=[
            pl.BlockSpec((tm, D_MODEL), lambda i, j: (i, 0)),
            pl.BlockSpec((None, D_MODEL, tn), lambda i, j: (layer, 0, j)),
            pl.BlockSpec((tm, tn), lambda i, j: (i, j)),
        ],
        out_specs=pl.BlockSpec((tm, tn), lambda i, j: (i, j)),
        out_shape=jax.ShapeDtypeStruct((SEQ, D_MODEL), _F32),
        compiler_params=_cparams(2),
        name="out_proj",
    )(y, w_all, x)


def _ffn_kernel(x_ref, halo_ref, g_ref, wg_ref, wu_ref, cw_ref, cb_ref, wd_ref, o_ref,
                h_scr, gate_scr, *, tm):
    i = pl.program_id(0)
    j = pl.program_id(1)

    def norm(x):
        ms = jnp.mean(x * x, axis=-1, keepdims=True)
        return (x * lax.rsqrt(ms + NORM_EPS) * g_ref[...]).astype(_BF16)

    @pl.when(j == 0)
    def _():
        x = x_ref[...]
        h_scr[HALO:, :] = norm(x)
        halo = jnp.where(i > 0, halo_ref[...], 0.0)
        h_scr[:HALO, :] = norm(halo)
        o_ref[...] = x

    gate_scr[...] = jnp.dot(h_scr[...], wg_ref[...], preferred_element_type=_F32)
    up = jnp.dot(h_scr[HALO:, :], wu_ref[...], preferred_element_type=_F32)
    cw = cw_ref[...]
    g = cb_ref[...]
    for t in range(CONV_WIDTH):
        lo = HALO - (CONV_WIDTH - 1) + t
        g = g + gate_scr[lo:lo + tm, :] * cw[t:t + 1]
    act = (g * jax.nn.sigmoid(g) * up).astype(_BF16)
    o_ref[...] += jnp.dot(act, wd_ref[...], preferred_element_type=_F32)


def _ffn(x, g_all, wg_all, wu_all, cw_all, cb_all, wd_all, layer, tm=512, tf=512):
    halo_blocks = tm // HALO
    return pl.pallas_call(
        functools.partial(_ffn_kernel, tm=tm),
        grid=(SEQ // tm, D_FF // tf),
        in_specs=[
            pl.BlockSpec((tm, D_MODEL), lambda i, j: (i, 0)),
            pl.BlockSpec((HALO, D_MODEL),
                         lambda i, j: (jnp.maximum(i * halo_blocks - 1, 0), 0)),
            pl.BlockSpec((None, 1, D_MODEL), lambda i, j: (layer, 0, 0)),
            pl.BlockSpec((None, D_MODEL, tf), lambda i, j: (layer, 0, j)),
            pl.BlockSpec((None, D_MODEL, tf), lambda i, j: (layer, 0, j)),
            pl.BlockSpec((None, CONV_WIDTH, tf), lambda i, j: (layer, 0, j)),
            pl.BlockSpec((None, 1, tf), lambda i, j: (layer, 0, j)),
            pl.BlockSpec((None, tf, D_MODEL), lambda i, j: (layer, j, 0)),
        ],
        out_specs=pl.BlockSpec((tm, D_MODEL), lambda i, j: (i, 0)),
        out_shape=jax.ShapeDtypeStruct((SEQ, D_MODEL), _F32),
        scratch_shapes=[
            pltpu.VMEM((tm + HALO, D_MODEL), _BF16),
            pltpu.VMEM((tm + HALO, tf), _F32),
        ],
        compiler_params=_cparams(2),
        name="gated_mlp",
    )(x, x, g_all, wg_all, wu_all, cw_all, cb_all, wd_all)


def _final_norm_kernel(x_ref, g_ref, o_ref):
    x = x_ref[...]
    ms = jnp.mean(x * x, axis=-1, keepdims=True)
    o_ref[...] = x * lax.rsqrt(ms + NORM_EPS) * g_ref[...]


def _final_norm(x, g, tm=512):
    return pl.pallas_call(
        _final_norm_kernel,
        grid=(SEQ // tm,),
        in_specs=[
            pl.BlockSpec((tm, D_MODEL), lambda i: (i, 0)),
            pl.BlockSpec((1, D_MODEL), lambda i: (0, 0)),
        ],
        out_specs=pl.BlockSpec((tm, D_MODEL), lambda i: (i, 0)),
        out_shape=jax.ShapeDtypeStruct((SEQ, D_MODEL), _F32),
        compiler_params=_cparams(1),
        name="final_norm",
    )(x, g)


def kernel(x, attn_norm, w_qkv, lambda_q1, lambda_k1, lambda_q2, lambda_k2, diff_subln,
           sb_norm, rel_bias, ch_norm, w_o, ffn_norm, w_gate, w_up, conv_w, conv_b, w_down,
           final_norm):
    batch, seq, d_model = x.shape
    assert (batch, seq, d_model) == (1, SEQ, D_MODEL)
    x2 = x.reshape(SEQ, D_MODEL)

    w_qkv_b = w_qkv.astype(_BF16)
    w_o_b = w_o.astype(_BF16)
    w_gate_b = w_gate.astype(_BF16)
    w_up_b = w_up.astype(_BF16)
    w_down_b = w_down.astype(_BF16)

    attn_g = attn_norm.reshape(DEPTH, 1, D_MODEL)
    ffn_g = ffn_norm.reshape(DEPTH, 1, D_MODEL)
    conv_b3 = conv_b.reshape(DEPTH, 1, D_FF)
    lam_vecs = jnp.stack([lambda_q1, lambda_k1, lambda_q2, lambda_k2], axis=1)
    subln_col = diff_subln.reshape(DEPTH, 2 * HEAD_DIM, 1)
    sb_col = sb_norm.reshape(DEPTH, N_HEADS // 2, 2 * HEAD_DIM, 1)
    ch_col = ch_norm.reshape(DEPTH, N_HEADS // 2, 2 * HEAD_DIM, 1)
    slopes = 2.0 ** (-8.0 * jnp.arange(1, N_HEADS + 1, dtype=_F32) / N_HEADS)
    col_scale = jnp.asarray(_qkv_col_scale())

    for layer in range(DEPTH):
        lam_init = 0.8 - 0.6 * math.exp(-0.3 * layer)
        qkv = _qkv_proj(x2, attn_g, w_qkv_b, col_scale, layer)
        y_a = _diff_attention(qkv, slopes, lam_vecs, subln_col, layer, lam_init)
        y_b = _sb_attention(qkv, sb_col, layer)
        y_c = _band_attention(qkv, _band_bias_tiles(rel_bias[layer]), ch_col, layer)
        y = jnp.concatenate([y_a, y_b, y_c], axis=-1)
        x2 = _out_proj(y, w_o_b, x2, layer)
        x2 = _ffn(x2, ffn_g, w_gate_b, w_up_b, conv_w, conv_b3, w_down_b, layer)
    out = _final_norm(x2, final_norm.reshape(1, D_MODEL))
    return out.reshape(1, SEQ, D_MODEL)
```

```python
import functools
import math

import jax
import jax.numpy as jnp
import numpy as np
from jax import lax
from jax.experimental import pallas as pl
from jax.experimental.pallas import tpu as pltpu

D_MODEL = 2048
SEQ = 8192
DEPTH = 4
CHUNK = 64
HEAD_DIM = 64
N_HEADS = 8
DIFF_WIDTH = 1024
SB_WIDTH = 512
CH_WIDTH = 512
QKV_WIDTH = 6144
LEFT_CHUNKS = 8
MAX_REL = 128
D_FF = 5632
CONV_WIDTH = 3
NORM_EPS = 1e-6
SUBLN_EPS = 1e-5
SCALE = HEAD_DIM ** -0.5
LOG2E = math.log2(math.e)

LANES = 128
ATT_T = 256
N_QBLK = SEQ // ATT_T
HALO = 16
VMEM_LIMIT = 48 * 1024 * 1024

QA, KA, VA = 0, 8, 16
QB, KB, VB = 24, 28, 32
QC, KC, VC = 36, 40, 44

_NT = (((1,), (1,)), ((), ()))
_TN = (((0,), (0,)), ((), ()))
_F32 = jnp.float32
_BF16 = jnp.bfloat16
_NEG = -1e30


def _cparams(n_axes):
    return pltpu.CompilerParams(
        dimension_semantics=("arbitrary",) * n_axes,
        vmem_limit_bytes=VMEM_LIMIT)


def _qkv_kernel(x_ref, g_ref, w_ref, cs_ref, o_ref, h_scr):
    @pl.when(pl.program_id(1) == 0)
    def _():
        x = x_ref[...]
        ms = jnp.mean(x * x, axis=-1, keepdims=True)
        h_scr[...] = (x * lax.rsqrt(ms + NORM_EPS) * g_ref[...]).astype(_BF16)

    acc = jnp.dot(h_scr[...], w_ref[...], preferred_element_type=_F32)
    o_ref[...] = (acc * cs_ref[...]).astype(_BF16)


def _qkv_col_scale():
    cs = np.ones((1, QKV_WIDTH), np.float32)
    for q_start, k_start in ((QA, KA), (QB, KB), (QC, KC)):
        cs[:, q_start * LANES:k_start * LANES] = SCALE * LOG2E
    return cs


def _qkv_proj(x, g_all, w_all, col_scale, layer, tm=1024, tn=1024):
    return pl.pallas_call(
        _qkv_kernel,
        grid=(SEQ // tm, QKV_WIDTH // tn),
        in_specs=[
            pl.BlockSpec((tm, D_MODEL), lambda i, j: (i, 0)),
            pl.BlockSpec((None, 1, D_MODEL), lambda i, j: (layer, 0, 0)),
            pl.BlockSpec((None, D_MODEL, tn), lambda i, j: (layer, 0, j)),
            pl.BlockSpec((1, tn), lambda i, j: (0, j)),
        ],
        out_specs=pl.BlockSpec((tm, tn), lambda i, j: (i, j)),
        out_shape=jax.ShapeDtypeStruct((SEQ, QKV_WIDTH), _BF16),
        scratch_shapes=[pltpu.VMEM((tm, D_MODEL), _BF16)],
        compiler_params=_cparams(2),
        name="qkv_proj",
    )(x, g_all, w_all, col_scale)


def _split_heads(q_ref):
    q = q_ref[...]
    lane = lax.broadcasted_iota(jnp.int32, q.shape, 1)
    zero = jnp.zeros_like(q)
    return jnp.where(lane < HEAD_DIM, q, zero), jnp.where(lane >= HEAD_DIM, q, zero)


def _key_block(ref, j):
    return ref[pl.ds(pl.multiple_of(j * ATT_T, ATT_T), ATT_T), :]


def _pair_rmsnorm_store(o_t, gain_col, o_ref):
    top, bot = o_t[:HEAD_DIM], o_t[HEAD_DIM:]
    top = top * lax.rsqrt(jnp.mean(top * top, axis=0, keepdims=True) + NORM_EPS)
    bot = bot * lax.rsqrt(jnp.mean(bot * bot, axis=0, keepdims=True) + NORM_EPS)
    y = jnp.concatenate([top, bot], axis=0) * gain_col
    o_ref[...] = y.T.astype(o_ref.dtype)


DIFF_TQ = 512
DIFF_TK = 256
DIFF_DIAG = DIFF_TQ // DIFF_TK
assert DIFF_DIAG == 2
DIFF_SPLIT = 3
DIFF_EXTRA = 16


def _diff_kernel(slope_ref, lam_ref, gain_ref, q_ref, k_ref, v_ref, o_ref,
                 diag_scr, vt_scr, s_scr, mx_scr, p_scr, al_scr, m_scr, acc_scr, *, lam_init):
    h = pl.program_id(0)
    i = pl.program_id(1)
    slope2 = slope_ref[h, 0]
    nb = (i + 1) * DIFF_DIAG

    @pl.when(i == 0)
    def _():
        row = lax.broadcasted_iota(jnp.int32, (DIFF_TK, DIFF_TQ), 0)
        col = lax.broadcasted_iota(jnp.int32, (DIFF_TK, DIFF_TQ), 1)
        for d in range(DIFF_DIAG):
            key = row + d * DIFF_TK
            allowed = (key // CHUNK) <= (col // CHUNK)
            lead = (col - row).astype(_F32) - jnp.abs(col - key).astype(_F32)
            diag_scr[d] = jnp.where(allowed, slope2 * lead, -jnp.inf)

        rows = lax.broadcasted_iota(jnp.int32, (DIFF_EXTRA, DIFF_TK), 0)
        ones_rows = jnp.where(rows == 0, 1.0, 0.0).astype(_BF16)

        def transpose_block(b, carry):
            vt_scr[b, :LANES] = v_ref[pl.ds(pl.multiple_of(b * DIFF_TK, DIFF_TK), DIFF_TK), :].T
            vt_scr[b, LANES:] = ones_rows
            return carry

        lax.fori_loop(0, SEQ // DIFF_TK, transpose_block, 0)

    q_t = q_ref[...].T
    dim = lax.broadcasted_iota(jnp.int32, q_t.shape, 0)
    zero = jnp.zeros_like(q_t)
    slope_rows = functools.reduce(
        lambda acc, n: jnp.where(dim == n, slope_ref[h, 1 + n], acc), range(DIFF_SPLIT),
        jnp.zeros(q_t.shape, _F32)).astype(_BF16)
    q_maps = tuple(jnp.concatenate([jnp.where(keep, q_t, zero), slope_rows], axis=0)
                   for keep in (dim < HEAD_DIM, dim >= HEAD_DIM))
    lane = lax.broadcasted_iota(jnp.int32, (DIFF_TK, LANES), 1)
    key_offset = lax.broadcasted_iota(jnp.int32, (DIFF_TK, LANES), 0).astype(_F32)
    key_feat = jnp.where(lane < DIFF_SPLIT, key_offset, 0.0).astype(_BF16)

    m_scr[...] = jnp.full(m_scr.shape, _NEG, _F32)
    acc_scr[...] = jnp.zeros(acc_scr.shape, _F32)

    def tile_offset(t, diagonal):
        if diagonal is not None:
            return jnp.zeros((), _F32)
        return -slope2 * (i * DIFF_TQ - t * DIFF_TK).astype(_F32)

    def stage_a(t, slot, diagonal=None):
        kb = k_ref[pl.ds(pl.multiple_of(t * DIFF_TK, DIFF_TK), DIFF_TK), :]
        kx = jnp.concatenate([kb, key_feat], axis=1)
        offset = tile_offset(t, diagonal)
        for c in range(2):
            s = jnp.dot(kx, q_maps[c], preferred_element_type=_F32)
            if diagonal is not None:
                s = s + diag_scr[diagonal]
            s_scr[slot, c] = s
            mx_scr[slot, c] = jnp.max(s, axis=0, keepdims=True) + offset

    def stage_b(t, slot, diagonal=None):
        offset = tile_offset(t, diagonal)
        for c in range(2):
            m_old = m_scr[c]
            m_new = jnp.maximum(m_old, mx_scr[slot, c])
            p_scr[slot, c] = jnp.exp2(s_scr[slot, c] - (m_new - offset)).astype(_BF16)
            al_scr[slot, c] = jnp.exp2(m_old - m_new)
            m_scr[c] = m_new

    def stage_c(t, slot):
        vt = vt_scr[t]
        for c in range(2):
            pv = jnp.dot(vt, p_scr[slot, c], preferred_element_type=_F32)
            acc_scr[c] = al_scr[slot, c] * acc_scr[c] + pv

    def drain():
        stage_c(nb - 2, 0)
        stage_b(nb - 1, 1, diagonal=1)
        stage_c(nb - 1, 1)

    @pl.when(i == 0)
    def _():
        stage_a(0, 0, diagonal=0)
        stage_b(0, 0, diagonal=0)
        stage_a(1, 1, diagonal=1)
        drain()

    @pl.when(i > 0)
    def _():
        stage_a(0, 0)
        stage_b(0, 0)
        stage_a(1, 1)

        def body(u, carry):
            t = 2 * u
            stage_c(t - 2, 0)
            stage_b(t - 1, 1)
            stage_a(t, 0)
            stage_c(t - 1, 1)
            stage_b(t, 0)
            stage_a(t + 1, 1)
            return carry

        lax.fori_loop(1, nb // 2 - 1, body, 0)
        t = nb - 2
        stage_c(t - 2, 0)
        stage_b(t - 1, 1)
        stage_a(t, 0, diagonal=0)
        stage_c(t - 1, 1)
        stage_b(t, 0, diagonal=0)
        stage_a(t + 1, 1, diagonal=1)
        drain()

    lam_v = lam_ref[...]
    lam = (jnp.exp(jnp.sum(lam_v[0:1] * lam_v[1:2], axis=-1, keepdims=True))
           - jnp.exp(jnp.sum(lam_v[2:3] * lam_v[3:4], axis=-1, keepdims=True))
           + lam_init)
    acc0, acc1 = acc_scr[0], acc_scr[1]
    o_t = (acc0[:LANES] / acc0[LANES:LANES + 1]
           - lam * (acc1[:LANES] / acc1[LANES:LANES + 1]))
    ms = jnp.mean(o_t * o_t, axis=0, keepdims=True)
    y = o_t * lax.rsqrt(ms + SUBLN_EPS) * gain_ref[...] * (1.0 - lam_init)
    o_ref[...] = y.T.astype(o_ref.dtype)


def _alibi_slope_table():
    slope = 2.0 ** (-8.0 * np.arange(1, N_HEADS + 1, dtype=np.float32) / N_HEADS)
    slope2 = (slope * np.float32(LOG2E)).astype(np.float32)
    cols, rest = [slope2], slope2
    for _ in range(DIFF_SPLIT):
        term = rest.astype(_BF16).astype(np.float32)
        cols.append(term)
        rest = rest - term
    return jnp.asarray(np.stack(cols, axis=1))


def _diff_attention(qkv, slopes, lam_vecs, gain_col, layer, lam_init):
    tile = (DIFF_TK, DIFF_TQ)
    return pl.pallas_call(
        functools.partial(_diff_kernel, lam_init=lam_init),
        grid=(N_HEADS, SEQ // DIFF_TQ),
        in_specs=[
            pl.BlockSpec(memory_space=pltpu.SMEM),
            pl.BlockSpec((None, 4, HEAD_DIM), lambda h, i: (layer, 0, 0)),
            pl.BlockSpec((None, LANES, 1), lambda h, i: (layer, 0, 0)),
            pl.BlockSpec((DIFF_TQ, LANES), lambda h, i: (i, QA + h)),
            pl.BlockSpec((SEQ, LANES), lambda h, i: (0, KA + h)),
            pl.BlockSpec((SEQ, LANES), lambda h, i: (0, VA + h)),
        ],
        out_specs=pl.BlockSpec((DIFF_TQ, LANES), lambda h, i: (i, h)),
        out_shape=jax.ShapeDtypeStruct((SEQ, DIFF_WIDTH), _BF16),
        scratch_shapes=[
            pltpu.VMEM((DIFF_DIAG,) + tile, _F32),
            pltpu.VMEM((SEQ // DIFF_TK, LANES + DIFF_EXTRA, DIFF_TK), _BF16),
            pltpu.VMEM((2, 2) + tile, _F32),
            pltpu.VMEM((2, 2, 1, DIFF_TQ), _F32),
            pltpu.VMEM((2, 2) + tile, _BF16),
            pltpu.VMEM((2, 2, 1, DIFF_TQ), _F32),
            pltpu.VMEM((2, 1, DIFF_TQ), _F32),
            pltpu.VMEM((2, LANES + DIFF_EXTRA, DIFF_TQ), _F32),
        ],
        compiler_params=_cparams(2),
        name="diff_attention",
    )(slopes, lam_vecs, gain_col, qkv, qkv, qkv)


SB_TQ = 512
SB_TK = 256
SB_SUB = 128
SB_DIAG = SB_TQ // SB_TK
assert SB_DIAG == 2


def _sb_kernel(gain_ref, q_ref, k_ref, v_ref, o_ref,
               mask_scr, tri_scr, vt_scr, z_scr, hl_scr, a_scr, r_scr, acc_scr):
    i = pl.program_id(1)
    nb = (i + 1) * SB_DIAG

    @pl.when(i == 0)
    def _():
        row = lax.broadcasted_iota(jnp.int32, (SB_TK, SB_TQ), 0)
        col = lax.broadcasted_iota(jnp.int32, (SB_TK, SB_TQ), 1)
        for d in range(SB_DIAG):
            mask_scr[d] = jnp.where(row + d * SB_TK < col, 0.0, -jnp.inf)
        r2 = lax.broadcasted_iota(jnp.int32, (SB_SUB, 2 * SB_SUB), 0)
        c2 = lax.broadcasted_iota(jnp.int32, (SB_SUB, 2 * SB_SUB), 1) % SB_SUB
        tri_scr[...] = jnp.where(c2 >= r2, 1.0, 0.0).astype(_BF16)

        def transpose_block(b, carry):
            vt_scr[b] = v_ref[pl.ds(pl.multiple_of(b * SB_TK, SB_TK), SB_TK), :].T
            return carry

        lax.fori_loop(0, SEQ // SB_TK, transpose_block, 0)

    q_t = q_ref[...].T
    dim = lax.broadcasted_iota(jnp.int32, q_t.shape, 0)
    zero = jnp.zeros_like(q_t)
    q_heads = (jnp.where(dim < HEAD_DIM, q_t, zero), jnp.where(dim >= HEAD_DIM, q_t, zero))

    r_scr[...] = jnp.zeros(r_scr.shape, _F32)
    acc_scr[...] = jnp.zeros(acc_scr.shape, _F32)

    def key_block(t):
        return nb - 1 - t

    def stage_a(t, slot, diagonal=None):
        kb = k_ref[pl.ds(pl.multiple_of(key_block(t) * SB_TK, SB_TK), SB_TK), :]
        for hh in range(2):
            z = jnp.dot(kb, q_heads[hh], preferred_element_type=_F32)
            if diagonal is not None:
                z = z + mask_scr[diagonal]
            nl = jnp.maximum(z, 0.0) + jnp.log2(1.0 + jnp.exp2(-jnp.abs(z)))
            hi = nl.astype(_BF16)
            lo = (nl - hi.astype(_F32)).astype(_BF16)
            z_scr[slot, hh] = z
            for sub in range(SB_TK // SB_SUB):
                rows = slice(sub * SB_SUB, (sub + 1) * SB_SUB)
                hl_scr[slot, hh, sub, :SB_SUB] = hi[rows]
                hl_scr[slot, hh, sub, SB_SUB:] = lo[rows]

    def stage_b(slot):
        tri = tri_scr[...]
        for hh in range(2):
            right = r_scr[hh]
            for sub in reversed(range(SB_TK // SB_SUB)):
                rows = slice(sub * SB_SUB, (sub + 1) * SB_SUB)
                incl = jnp.dot(tri, hl_scr[slot, hh, sub], preferred_element_type=_F32)
                a_scr[slot, hh, rows] = jnp.exp2(z_scr[slot, hh, rows] - incl - right).astype(_BF16)
                right = right + incl[0:1]
            r_scr[hh] = right

    def stage_c(t, slot):
        vt = vt_scr[key_block(t)]
        for hh in range(2):
            acc_scr[hh] += jnp.dot(vt[hh * HEAD_DIM:(hh + 1) * HEAD_DIM], a_scr[slot, hh],
                                   preferred_element_type=_F32)

    stage_a(0, 0, diagonal=1)
    stage_b(0)
    stage_a(1, 1, diagonal=0)

    def body(u, carry):
        t = 2 * u
        stage_c(t - 2, 0)
        stage_b(1)
        stage_a(t, 0)
        stage_c(t - 1, 1)
        stage_b(0)
        stage_a(t + 1, 1)
        return carry

    lax.fori_loop(1, nb // 2, body, 0)
    stage_c(nb - 2, 0)
    stage_b(1)
    stage_c(nb - 1, 1)

    o_t = jnp.concatenate([acc_scr[0], acc_scr[1]], axis=0)
    _pair_rmsnorm_store(o_t, gain_ref[...], o_ref)


def _sb_attention(qkv, gain_col, layer):
    tile = (SB_TK, SB_TQ)
    return pl.pallas_call(
        _sb_kernel,
        grid=(N_HEADS // 2, SEQ // SB_TQ),
        in_specs=[
            pl.BlockSpec((None, None, LANES, 1), lambda p, i: (layer, p, 0, 0)),
            pl.BlockSpec((SB_TQ, LANES), lambda p, i: (i, QB + p)),
            pl.BlockSpec((SEQ, LANES), lambda p, i: (0, KB + p)),
            pl.BlockSpec((SEQ, LANES), lambda p, i: (0, VB + p)),
        ],
        out_specs=pl.BlockSpec((SB_TQ, LANES), lambda p, i: (i, p)),
        out_shape=jax.ShapeDtypeStruct((SEQ, SB_WIDTH), _BF16),
        scratch_shapes=[
            pltpu.VMEM((SB_DIAG,) + tile, _F32),
            pltpu.VMEM((SB_SUB, 2 * SB_SUB), _BF16),
            pltpu.VMEM((SEQ // SB_TK, LANES, SB_TK), _BF16),
            pltpu.VMEM((2, 2) + tile, _F32),
            pltpu.VMEM((2, 2, SB_TK // SB_SUB, 2 * SB_SUB, SB_TQ), _BF16),
            pltpu.VMEM((2, 2) + tile, _BF16),
            pltpu.VMEM((2, 1, SB_TQ), _F32),
            pltpu.VMEM((2, HEAD_DIM, SB_TQ), _F32),
        ],
        compiler_params=_cparams(2),
        name="sb_attention",
    )(gain_col, qkv, qkv, qkv)


BAND_TILES = 1 + (LEFT_CHUNKS * CHUNK) // ATT_T


def _band_kernel(bias_ref, gain_ref, q_ref, k_ref, v_ref, o_ref):
    i = pl.program_id(1)
    q_heads = _split_heads(q_ref)
    rows = lax.broadcasted_iota(jnp.int32, (LANES, ATT_T), 0)
    outs = []
    for hh in range(2):
        scores, vals = [], []
        for kk in range(BAND_TILES):
            jb = i - (BAND_TILES - 1) + kk
            jc = jnp.maximum(jb, 0)
            s = lax.dot_general(_key_block(k_ref, jc), q_heads[hh], _NT,
                                preferred_element_type=_F32) + bias_ref[hh, kk]
            scores.append(jnp.where(jb >= 0, s, -jnp.inf))
            vals.append(_key_block(v_ref, jc))
        m = functools.reduce(jnp.maximum,
                             [jnp.max(s, axis=0, keepdims=True) for s in scores])
        probs = [jnp.exp2(s - m) for s in scores]
        denom = functools.reduce(jnp.add,
                                 [jnp.sum(p, axis=0, keepdims=True) for p in probs])
        o_t = functools.reduce(jnp.add, [
            lax.dot_general(v, p.astype(_BF16), _TN, preferred_element_type=_F32)
            for v, p in zip(vals, probs)])
        outs.append(o_t / denom)
    o_t = jnp.where(rows < HEAD_DIM, outs[0], outs[1])
    _pair_rmsnorm_store(o_t, gain_ref[...], o_ref)


def _band_bias_tiles(rel_table):
    t = ATT_T
    key = np.arange(t)[:, None]
    qry = np.arange(t)[None, :]
    shift = np.arange(2 * t + 1) - t
    vec_idx, mask = [], []
    for kk in range(BAND_TILES):
        back = (BAND_TILES - 1 - kk) * t
        vec_idx.append(np.clip(back + shift, -(CHUNK - 1), MAX_REL) + (CHUNK - 1))
        dc = back // CHUNK + qry // CHUNK - key // CHUNK
        mask.append((dc >= 0) & (dc <= LEFT_CHUNKS))
    vec = rel_table[:, np.stack(vec_idx)] * LOG2E
    toe = jnp.tile(vec, (1, 1, t))[..., :t * 2 * t].reshape(N_HEADS, BAND_TILES, t, 2 * t)[..., t:]
    tiles = jnp.where(np.stack(mask), toe, -jnp.inf)
    return tiles.reshape(N_HEADS // 2, 2, BAND_TILES, t, t)


def _band_attention(qkv, bias_tiles, gain_col, layer):
    return pl.pallas_call(
        _band_kernel,
        grid=(N_HEADS // 2, N_QBLK),
        in_specs=[
            pl.BlockSpec((None, 2, BAND_TILES, ATT_T, ATT_T), lambda p, i: (p, 0, 0, 0, 0)),
            pl.BlockSpec((None, None, LANES, 1), lambda p, i: (layer, p, 0, 0)),
            pl.BlockSpec((ATT_T, LANES), lambda p, i: (i, QC + p)),
            pl.BlockSpec((SEQ, LANES), lambda p, i: (0, KC + p)),
            pl.BlockSpec((SEQ, LANES), lambda p, i: (0, VC + p)),
        ],
        out_specs=pl.BlockSpec((ATT_T, LANES), lambda p, i: (i, p)),
        out_shape=jax.ShapeDtypeStruct((SEQ, CH_WIDTH), _BF16),
        compiler_params=_cparams(2),
        name="band_attention",
    )(bias_tiles, gain_col, qkv, qkv, qkv)


def _oproj_kernel(y_ref, w_ref, x_ref, o_ref):
    o_ref[...] = x_ref[...] + jnp.dot(y_ref[...], w_ref[...], preferred_element_type=_F32)


def _out_proj(y, w_all, x, layer, tm=1024, tn=512):
    return pl.pallas_call(
        _oproj_kernel,
        grid=(SEQ // tm, D_MODEL // tn),
        in_specs=[
            pl.BlockSpec((tm, D_MODEL), lambda i, j: (i, 0)),
            pl.BlockSpec((None, D_MODEL, tn), lambda i, j: (layer, 0, j)),
            pl.BlockSpec((tm, tn), lambda i, j: (i, j)),
        ],
        out_specs=pl.BlockSpec((tm, tn), lambda i, j: (i, j)),
        out_shape=jax.ShapeDtypeStruct((SEQ, D_MODEL), _F32),
        compiler_params=_cparams(2),
        name="out_proj",
    )(y, w_all, x)


def _ffn_kernel(x_ref, halo_ref, g_ref, wg_ref, wu_ref, cw_ref, cb_ref, wd_ref, o_ref,
                h_scr, gate_scr, *, tm):
    i = pl.program_id(0)
    j = pl.program_id(1)

    def norm(x):
        ms = jnp.mean(x * x, axis=-1, keepdims=True)
        return (x * lax.rsqrt(ms + NORM_EPS) * g_ref[...]).astype(_BF16)

    @pl.when(j == 0)
    def _():
        x = x_ref[...]
        h_scr[HALO:, :] = norm(x)
        halo = jnp.where(i > 0, halo_ref[...], 0.0)
        h_scr[:HALO, :] = norm(halo)
        o_ref[...] = x

    gate_scr[...] = jnp.dot(h_scr[...], wg_ref[...], preferred_element_type=_F32)
    up = jnp.dot(h_scr[HALO:, :], wu_ref[...], preferred_element_type=_F32)
    cw = cw_ref[...]
    g = cb_ref[...]
    for t in range(CONV_WIDTH):
        lo = HALO - (CONV_WIDTH - 1) + t
        g = g + gate_scr[lo:lo + tm, :] * cw[t:t + 1]
    act = (g * jax.nn.sigmoid(g) * up).astype(_BF16)
    o_ref[...] += jnp.dot(act, wd_ref[...], preferred_element_type=_F32)


def _ffn(x, g_all, wg_all, wu_all, cw_all, cb_all, wd_all, layer, tm=512, tf=512):
    halo_blocks = tm // HALO
    return pl.pallas_call(
        functools.partial(_ffn_kernel, tm=tm),
        grid=(SEQ // tm, D_FF // tf),
        in_specs=[
            pl.BlockSpec((tm, D_MODEL), lambda i, j: (i, 0)),
            pl.BlockSpec((HALO, D_MODEL),
                         lambda i, j: (jnp.maximum(i * halo_blocks - 1, 0), 0)),
            pl.BlockSpec((None, 1, D_MODEL), lambda i, j: (layer, 0, 0)),
            pl.BlockSpec((None, D_MODEL, tf), lambda i, j: (layer, 0, j)),
            pl.BlockSpec((None, D_MODEL, tf), lambda i, j: (layer, 0, j)),
            pl.BlockSpec((None, CONV_WIDTH, tf), lambda i, j: (layer, 0, j)),
            pl.BlockSpec((None, 1, tf), lambda i, j: (layer, 0, j)),
            pl.BlockSpec((None, tf, D_MODEL), lambda i, j: (layer, j, 0)),
        ],
        out_specs=pl.BlockSpec((tm, D_MODEL), lambda i, j: (i, 0)),
        out_shape=jax.ShapeDtypeStruct((SEQ, D_MODEL), _F32),
        scratch_shapes=[
            pltpu.VMEM((tm + HALO, D_MODEL), _BF16),
            pltpu.VMEM((tm + HALO, tf), _F32),
        ],
        compiler_params=_cparams(2),
        name="gated_mlp",
    )(x, x, g_all, wg_all, wu_all, cw_all, cb_all, wd_all)


def _final_norm_kernel(x_ref, g_ref, o_ref):
    x = x_ref[...]
    ms = jnp.mean(x * x, axis=-1, keepdims=True)
    o_ref[...] = x * lax.rsqrt(ms + NORM_EPS) * g_ref[...]


def _final_norm(x, g, tm=512):
    return pl.pallas_call(
        _final_norm_kernel,
        grid=(SEQ // tm,),
        in_specs=[
            pl.BlockSpec((tm, D_MODEL), lambda i: (i, 0)),
            pl.BlockSpec((1, D_MODEL), lambda i: (0, 0)),
        ],
        out_specs=pl.BlockSpec((tm, D_MODEL), lambda i: (i, 0)),
        out_shape=jax.ShapeDtypeStruct((SEQ, D_MODEL), _F32),
        compiler_params=_cparams(1),
        name="final_norm",
    )(x, g)


def kernel(x, attn_norm, w_qkv, lambda_q1, lambda_k1, lambda_q2, lambda_k2, diff_subln,
           sb_norm, rel_bias, ch_norm, w_o, ffn_norm, w_gate, w_up, conv_w, conv_b, w_down,
           final_norm):
    batch, seq, d_model = x.shape
    assert (batch, seq, d_model) == (1, SEQ, D_MODEL)
    x2 = x.reshape(SEQ, D_MODEL)

    w_qkv_b = w_qkv.astype(_BF16)
    w_o_b = w_o.astype(_BF16)
    w_gate_b = w_gate.astype(_BF16)
    w_up_b = w_up.astype(_BF16)
    w_down_b = w_down.astype(_BF16)

    attn_g = attn_norm.reshape(DEPTH, 1, D_MODEL)
    ffn_g = ffn_norm.reshape(DEPTH, 1, D_MODEL)
    conv_b3 = conv_b.reshape(DEPTH, 1, D_FF)
    lam_vecs = jnp.stack([lambda_q1, lambda_k1, lambda_q2, lambda_k2], axis=1)
    subln_col = diff_subln.reshape(DEPTH, 2 * HEAD_DIM, 1)
    sb_col = sb_norm.reshape(DEPTH, N_HEADS // 2, 2 * HEAD_DIM, 1)
    ch_col = ch_norm.reshape(DEPTH, N_HEADS // 2, 2 * HEAD_DIM, 1)
    slopes = _alibi_slope_table()
    col_scale = jnp.asarray(_qkv_col_scale())

    for layer in range(DEPTH):
        lam_init = 0.8 - 0.6 * math.exp(-0.3 * layer)
        qkv = _qkv_proj(x2, attn_g, w_qkv_b, col_scale, layer)
        y_a = _diff_attention(qkv, slopes, lam_vecs, subln_col, layer, lam_init)
        y_b = _sb_attention(qkv, sb_col, layer)
        y_c = _band_attention(qkv, _band_bias_tiles(rel_bias[layer]), ch_col, layer)
        y = jnp.concatenate([y_a, y_b, y_c], axis=-1)
        x2 = _out_proj(y, w_o_b, x2, layer)
        x2 = _ffn(x2, ffn_g, w_gate_b, w_up_b, conv_w, conv_b3, w_down_b, layer)
    out = _final_norm(x2, final_norm.reshape(1, D_MODEL))
    return out.reshape(1, SEQ, D_MODEL)
```

```python
import functools
import math

import jax
import jax.numpy as jnp
import numpy as np
from jax import lax
from jax.experimental import pallas as pl
from jax.experimental.pallas import tpu as pltpu

D_MODEL = 2048
SEQ = 8192
DEPTH = 4
CHUNK = 64
HEAD_DIM = 64
N_HEADS = 8
DIFF_WIDTH = 1024
SB_WIDTH = 512
CH_WIDTH = 512
QKV_WIDTH = 6144
LEFT_CHUNKS = 8
MAX_REL = 128
D_FF = 5632
CONV_WIDTH = 3
NORM_EPS = 1e-6
SUBLN_EPS = 1e-5
SCALE = HEAD_DIM ** -0.5
LOG2E = math.log2(math.e)

LANES = 128
ATT_T = 256
N_QBLK = SEQ // ATT_T
HALO = 16
VMEM_LIMIT = 48 * 1024 * 1024

QA, KA, VA = 0, 8, 16
QB, KB, VB = 24, 28, 32
QC, KC, VC = 36, 40, 44

_F32 = jnp.float32
_BF16 = jnp.bfloat16
_NEG = -1e30


def _cparams(n_axes):
    return pltpu.CompilerParams(
        dimension_semantics=("arbitrary",) * n_axes,
        vmem_limit_bytes=VMEM_LIMIT)


def _qkv_kernel(x_ref, g_ref, w_ref, cs_ref, o_ref, h_scr):
    @pl.when(pl.program_id(1) == 0)
    def _():
        x = x_ref[...]
        ms = jnp.mean(x * x, axis=-1, keepdims=True)
        h_scr[...] = (x * lax.rsqrt(ms + NORM_EPS) * g_ref[...]).astype(_BF16)

    acc = jnp.dot(h_scr[...], w_ref[...], preferred_element_type=_F32)
    o_ref[...] = (acc * cs_ref[...]).astype(_BF16)


def _qkv_col_scale():
    cs = np.ones((1, QKV_WIDTH), np.float32)
    for q_start, k_start in ((QA, KA), (QB, KB), (QC, KC)):
        cs[:, q_start * LANES:k_start * LANES] = SCALE * LOG2E
    return cs


def _qkv_proj(x, g_all, w_all, col_scale, layer, tm=1024, tn=1024):
    return pl.pallas_call(
        _qkv_kernel,
        grid=(SEQ // tm, QKV_WIDTH // tn),
        in_specs=[
            pl.BlockSpec((tm, D_MODEL), lambda i, j: (i, 0)),
            pl.BlockSpec((None, 1, D_MODEL), lambda i, j: (layer, 0, 0)),
            pl.BlockSpec((None, D_MODEL, tn), lambda i, j: (layer, 0, j)),
            pl.BlockSpec((1, tn), lambda i, j: (0, j)),
        ],
        out_specs=pl.BlockSpec((tm, tn), lambda i, j: (i, j)),
        out_shape=jax.ShapeDtypeStruct((SEQ, QKV_WIDTH), _BF16),
        scratch_shapes=[pltpu.VMEM((tm, D_MODEL), _BF16)],
        compiler_params=_cparams(2),
        name="qkv_proj",
    )(x, g_all, w_all, col_scale)


def _split_heads_t(q_ref):
    q_t = q_ref[...].T
    dim = lax.broadcasted_iota(jnp.int32, q_t.shape, 0)
    zero = jnp.zeros_like(q_t)
    return jnp.where(dim < HEAD_DIM, q_t, zero), jnp.where(dim >= HEAD_DIM, q_t, zero)


def _key_block(ref, j):
    return ref[pl.ds(pl.multiple_of(j * ATT_T, ATT_T), ATT_T), :]


def _pair_rmsnorm_store(o_t, gain_col, o_ref):
    top, bot = o_t[:HEAD_DIM], o_t[HEAD_DIM:]
    top = top * lax.rsqrt(jnp.mean(top * top, axis=0, keepdims=True) + NORM_EPS)
    bot = bot * lax.rsqrt(jnp.mean(bot * bot, axis=0, keepdims=True) + NORM_EPS)
    y = jnp.concatenate([top, bot], axis=0) * gain_col
    o_ref[...] = y.T.astype(o_ref.dtype)


DIFF_TQ = 512
DIFF_TK = 256
DIFF_DIAG = DIFF_TQ // DIFF_TK
assert DIFF_DIAG == 2
DIFF_SPLIT = 3
DIFF_EXTRA = 16


def _diff_kernel(slope_ref, lam_ref, gain_ref, q_ref, k_ref, v_ref, o_ref,
                 diag_scr, vt_scr, s_scr, mx_scr, p_scr, al_scr, m_scr, acc_scr, *, lam_init):
    h = pl.program_id(0)
    i = pl.program_id(1)
    slope2 = slope_ref[h, 0]
    nb = (i + 1) * DIFF_DIAG

    @pl.when(i == 0)
    def _():
        row = lax.broadcasted_iota(jnp.int32, (DIFF_TK, DIFF_TQ), 0)
        col = lax.broadcasted_iota(jnp.int32, (DIFF_TK, DIFF_TQ), 1)
        for d in range(DIFF_DIAG):
            key = row + d * DIFF_TK
            allowed = (key // CHUNK) <= (col // CHUNK)
            lead = (col - row).astype(_F32) - jnp.abs(col - key).astype(_F32)
            diag_scr[d] = jnp.where(allowed, slope2 * lead, -jnp.inf)

        rows = lax.broadcasted_iota(jnp.int32, (DIFF_EXTRA, DIFF_TK), 0)
        ones_rows = jnp.where(rows == 0, 1.0, 0.0).astype(_BF16)

        def transpose_block(b, carry):
            vt_scr[b, :LANES] = v_ref[pl.ds(pl.multiple_of(b * DIFF_TK, DIFF_TK), DIFF_TK), :].T
            vt_scr[b, LANES:] = ones_rows
            return carry

        lax.fori_loop(0, SEQ // DIFF_TK, transpose_block, 0)

    dim = lax.broadcasted_iota(jnp.int32, (LANES, DIFF_TQ), 0)
    slope_rows = functools.reduce(
        lambda acc, n: jnp.where(dim == n, slope_ref[h, 1 + n], acc), range(DIFF_SPLIT),
        jnp.zeros((LANES, DIFF_TQ), _F32)).astype(_BF16)
    q_maps = tuple(jnp.concatenate([q_half, slope_rows], axis=0)
                   for q_half in _split_heads_t(q_ref))
    lane = lax.broadcasted_iota(jnp.int32, (DIFF_TK, LANES), 1)
    key_offset = lax.broadcasted_iota(jnp.int32, (DIFF_TK, LANES), 0).astype(_F32)
    key_feat = jnp.where(lane < DIFF_SPLIT, key_offset, 0.0).astype(_BF16)

    m_scr[...] = jnp.full(m_scr.shape, _NEG, _F32)
    acc_scr[...] = jnp.zeros(acc_scr.shape, _F32)

    def tile_offset(t, diagonal):
        if diagonal is not None:
            return jnp.zeros((), _F32)
        return -slope2 * (i * DIFF_TQ - t * DIFF_TK).astype(_F32)

    def stage_a(t, slot, diagonal=None):
        kb = k_ref[pl.ds(pl.multiple_of(t * DIFF_TK, DIFF_TK), DIFF_TK), :]
        kx = jnp.concatenate([kb, key_feat], axis=1)
        offset = tile_offset(t, diagonal)
        for c in range(2):
            s = jnp.dot(kx, q_maps[c], preferred_element_type=_F32)
            if diagonal is not None:
                s = s + diag_scr[diagonal]
            s_scr[slot, c] = s
            mx_scr[slot, c] = jnp.max(s, axis=0, keepdims=True) + offset

    def stage_b(t, slot, diagonal=None):
        offset = tile_offset(t, diagonal)
        for c in range(2):
            m_old = m_scr[c]
            m_new = jnp.maximum(m_old, mx_scr[slot, c])
            p_scr[slot, c] = jnp.exp2(s_scr[slot, c] - (m_new - offset)).astype(_BF16)
            al_scr[slot, c] = jnp.exp2(m_old - m_new)
            m_scr[c] = m_new

    def stage_c(t, slot):
        vt = vt_scr[t]
        for c in range(2):
            pv = jnp.dot(vt, p_scr[slot, c], preferred_element_type=_F32)
            acc_scr[c] = al_scr[slot, c] * acc_scr[c] + pv

    def drain():
        stage_c(nb - 2, 0)
        stage_b(nb - 1, 1, diagonal=1)
        stage_c(nb - 1, 1)

    @pl.when(i == 0)
    def _():
        stage_a(0, 0, diagonal=0)
        stage_b(0, 0, diagonal=0)
        stage_a(1, 1, diagonal=1)
        drain()

    @pl.when(i > 0)
    def _():
        stage_a(0, 0)
        stage_b(0, 0)
        stage_a(1, 1)

        def two_steps(t):
            stage_c(t - 2, 0)
            stage_b(t - 1, 1)
            stage_a(t, 0)
            stage_c(t - 1, 1)
            stage_b(t, 0)
            stage_a(t + 1, 1)

        odd = (i - 1) % 2

        @pl.when(odd == 1)
        def _():
            two_steps(2)

        def body(u, carry):
            t = 2 + 2 * odd + 4 * u
            two_steps(t)
            two_steps(t + 2)
            return carry

        lax.fori_loop(0, (i - 1) // 2, body, 0)
        t = nb - 2
        stage_c(t - 2, 0)
        stage_b(t - 1, 1)
        stage_a(t, 0, diagonal=0)
        stage_c(t - 1, 1)
        stage_b(t, 0, diagonal=0)
        stage_a(t + 1, 1, diagonal=1)
        drain()

    lam_v = lam_ref[...]
    lam = (jnp.exp(jnp.sum(lam_v[0:1] * lam_v[1:2], axis=-1, keepdims=True))
           - jnp.exp(jnp.sum(lam_v[2:3] * lam_v[3:4], axis=-1, keepdims=True))
           + lam_init)
    acc0, acc1 = acc_scr[0], acc_scr[1]
    o_t = (acc0[:LANES] / acc0[LANES:LANES + 1]
           - lam * (acc1[:LANES] / acc1[LANES:LANES + 1]))
    ms = jnp.mean(o_t * o_t, axis=0, keepdims=True)
    y = o_t * lax.rsqrt(ms + SUBLN_EPS) * gain_ref[...] * (1.0 - lam_init)
    o_ref[...] = y.T.astype(o_ref.dtype)


def _alibi_slope_table():
    slope = 2.0 ** (-8.0 * np.arange(1, N_HEADS + 1, dtype=np.float32) / N_HEADS)
    slope2 = (slope * np.float32(LOG2E)).astype(np.float32)
    cols, rest = [slope2], slope2
    for _ in range(DIFF_SPLIT):
        term = rest.astype(_BF16).astype(np.float32)
        cols.append(term)
        rest = rest - term
    return jnp.asarray(np.stack(cols, axis=1))


def _diff_attention(qkv, slopes, lam_vecs, gain_col, layer, lam_init):
    tile = (DIFF_TK, DIFF_TQ)
    return pl.pallas_call(
        functools.partial(_diff_kernel, lam_init=lam_init),
        grid=(N_HEADS, SEQ // DIFF_TQ),
        in_specs=[
            pl.BlockSpec(memory_space=pltpu.SMEM),
            pl.BlockSpec((None, 4, HEAD_DIM), lambda h, i: (layer, 0, 0)),
            pl.BlockSpec((None, LANES, 1), lambda h, i: (layer, 0, 0)),
            pl.BlockSpec((DIFF_TQ, LANES), lambda h, i: (i, QA + h)),
            pl.BlockSpec((SEQ, LANES), lambda h, i: (0, KA + h)),
            pl.BlockSpec((SEQ, LANES), lambda h, i: (0, VA + h)),
        ],
        out_specs=pl.BlockSpec((DIFF_TQ, LANES), lambda h, i: (i, h)),
        out_shape=jax.ShapeDtypeStruct((SEQ, DIFF_WIDTH), _BF16),
        scratch_shapes=[
            pltpu.VMEM((DIFF_DIAG,) + tile, _F32),
            pltpu.VMEM((SEQ // DIFF_TK, LANES + DIFF_EXTRA, DIFF_TK), _BF16),
            pltpu.VMEM((2, 2) + tile, _F32),
            pltpu.VMEM((2, 2, 1, DIFF_TQ), _F32),
            pltpu.VMEM((2, 2) + tile, _BF16),
            pltpu.VMEM((2, 2, 1, DIFF_TQ), _F32),
            pltpu.VMEM((2, 1, DIFF_TQ), _F32),
            pltpu.VMEM((2, LANES + DIFF_EXTRA, DIFF_TQ), _F32),
        ],
        compiler_params=_cparams(2),
        name="diff_attention",
    )(slopes, lam_vecs, gain_col, qkv, qkv, qkv)


SB_TQ = 512
SB_TK = 256
SB_SUB = 128
SB_DIAG = SB_TQ // SB_TK
assert SB_DIAG == 2


def _sb_kernel(gain_ref, q_ref, k_ref, v_ref, o_ref,
               mask_scr, tri_scr, vt_scr, z_scr, hl_scr, a_scr, r_scr, acc_scr):
    i = pl.program_id(1)
    nb = (i + 1) * SB_DIAG

    @pl.when(i == 0)
    def _():
        row = lax.broadcasted_iota(jnp.int32, (SB_TK, SB_TQ), 0)
        col = lax.broadcasted_iota(jnp.int32, (SB_TK, SB_TQ), 1)
        for d in range(SB_DIAG):
            mask_scr[d] = jnp.where(row + d * SB_TK < col, 0.0, -jnp.inf)
        r2 = lax.broadcasted_iota(jnp.int32, (SB_SUB, 2 * SB_SUB), 0)
        c2 = lax.broadcasted_iota(jnp.int32, (SB_SUB, 2 * SB_SUB), 1) % SB_SUB
        tri_scr[...] = jnp.where(c2 >= r2, 1.0, 0.0).astype(_BF16)

        def transpose_block(b, carry):
            vt_scr[b] = v_ref[pl.ds(pl.multiple_of(b * SB_TK, SB_TK), SB_TK), :].T
            return carry

        lax.fori_loop(0, SEQ // SB_TK, transpose_block, 0)

    q_heads = _split_heads_t(q_ref)

    r_scr[...] = jnp.zeros(r_scr.shape, _F32)
    acc_scr[...] = jnp.zeros(acc_scr.shape, _F32)

    def key_block(t):
        return nb - 1 - t

    def stage_a(t, slot, diagonal=None):
        kb = k_ref[pl.ds(pl.multiple_of(key_block(t) * SB_TK, SB_TK), SB_TK), :]
        for hh in range(2):
            z = jnp.dot(kb, q_heads[hh], preferred_element_type=_F32)
            if diagonal is not None:
                z = z + mask_scr[diagonal]
            nl = jnp.maximum(z, 0.0) + jnp.log2(1.0 + jnp.exp2(-jnp.abs(z)))
            hi = nl.astype(_BF16)
            lo = (nl - hi.astype(_F32)).astype(_BF16)
            z_scr[slot, hh] = z
            for sub in range(SB_TK // SB_SUB):
                rows = slice(sub * SB_SUB, (sub + 1) * SB_SUB)
                hl_scr[slot, hh, sub, :SB_SUB] = hi[rows]
                hl_scr[slot, hh, sub, SB_SUB:] = lo[rows]

    def stage_b(slot):
        tri = tri_scr[...]
        for hh in range(2):
            right = r_scr[hh]
            for sub in reversed(range(SB_TK // SB_SUB)):
                rows = slice(sub * SB_SUB, (sub + 1) * SB_SUB)
                incl = jnp.dot(tri, hl_scr[slot, hh, sub], preferred_element_type=_F32)
                a_scr[slot, hh, rows] = jnp.exp2(z_scr[slot, hh, rows] - incl - right).astype(_BF16)
                right = right + incl[0:1]
            r_scr[hh] = right

    def stage_c(t, slot):
        vt = vt_scr[key_block(t)]
        for hh in range(2):
            acc_scr[hh] += jnp.dot(vt[hh * HEAD_DIM:(hh + 1) * HEAD_DIM], a_scr[slot, hh],
                                   preferred_element_type=_F32)

    stage_a(0, 0, diagonal=1)
    stage_b(0)
    stage_a(1, 1, diagonal=0)

    def two_steps(t):
        stage_c(t - 2, 0)
        stage_b(1)
        stage_a(t, 0)
        stage_c(t - 1, 1)
        stage_b(0)
        stage_a(t + 1, 1)

    odd = i % 2

    @pl.when(odd == 1)
    def _():
        two_steps(2)

    def body(u, carry):
        t = 2 + 2 * odd + 4 * u
        two_steps(t)
        two_steps(t + 2)
        return carry

    lax.fori_loop(0, i // 2, body, 0)
    stage_c(nb - 2, 0)
    stage_b(1)
    stage_c(nb - 1, 1)

    o_t = jnp.concatenate([acc_scr[0], acc_scr[1]], axis=0)
    _pair_rmsnorm_store(o_t, gain_ref[...], o_ref)


def _sb_attention(qkv, gain_col, layer):
    tile = (SB_TK, SB_TQ)
    return pl.pallas_call(
        _sb_kernel,
        grid=(N_HEADS // 2, SEQ // SB_TQ),
        in_specs=[
            pl.BlockSpec((None, None, LANES, 1), lambda p, i: (layer, p, 0, 0)),
            pl.BlockSpec((SB_TQ, LANES), lambda p, i: (i, QB + p)),
            pl.BlockSpec((SEQ, LANES), lambda p, i: (0, KB + p)),
            pl.BlockSpec((SEQ, LANES), lambda p, i: (0, VB + p)),
        ],
        out_specs=pl.BlockSpec((SB_TQ, LANES), lambda p, i: (i, p)),
        out_shape=jax.ShapeDtypeStruct((SEQ, SB_WIDTH), _BF16),
        scratch_shapes=[
            pltpu.VMEM((SB_DIAG,) + tile, _F32),
            pltpu.VMEM((SB_SUB, 2 * SB_SUB), _BF16),
            pltpu.VMEM((SEQ // SB_TK, LANES, SB_TK), _BF16),
            pltpu.VMEM((2, 2) + tile, _F32),
            pltpu.VMEM((2, 2, SB_TK // SB_SUB, 2 * SB_SUB, SB_TQ), _BF16),
            pltpu.VMEM((2, 2) + tile, _BF16),
            pltpu.VMEM((2, 1, SB_TQ), _F32),
            pltpu.VMEM((2, HEAD_DIM, SB_TQ), _F32),
        ],
        compiler_params=_cparams(2),
        name="sb_attention",
    )(gain_col, qkv, qkv, qkv)


BAND_TILES = 1 + (LEFT_CHUNKS * CHUNK) // ATT_T


BAND_TQ = 512
BAND_SUBS = BAND_TQ // ATT_T


def _band_kernel(bias_ref, gain_ref, q_ref, k_ref, v_ref, o_ref, vt_scr):
    i = pl.program_id(1)

    @pl.when(i == 0)
    def _():
        def transpose_block(b, carry):
            vt_scr[b] = _key_block(v_ref, b).T
            return carry

        lax.fori_loop(0, N_QBLK, transpose_block, 0)

    q_heads = _split_heads_t(q_ref)

    first = i * BAND_SUBS - (BAND_TILES - 1)
    tiles = [jnp.maximum(first + n, 0) for n in range(BAND_TILES + BAND_SUBS - 1)]
    k_tiles = [_key_block(k_ref, j) for j in tiles]
    vt_tiles = [vt_scr[j] for j in tiles]

    cols = []
    for sub in range(BAND_SUBS):
        heads = []
        for hh in range(2):
            q_sub = q_heads[hh][:, sub * ATT_T:(sub + 1) * ATT_T]
            scores = []
            for kk in range(BAND_TILES):
                s = jnp.dot(k_tiles[sub + kk], q_sub, preferred_element_type=_F32) + bias_ref[hh, kk]
                scores.append(jnp.where(first + sub + kk >= 0, s, -jnp.inf))
            m = functools.reduce(jnp.maximum,
                                 [jnp.max(s, axis=0, keepdims=True) for s in scores])
            probs = [jnp.exp2(s - m) for s in scores]
            denom = functools.reduce(jnp.add,
                                     [jnp.sum(p, axis=0, keepdims=True) for p in probs])
            o_t = functools.reduce(jnp.add, [
                jnp.dot(vt_tiles[sub + kk][hh * HEAD_DIM:(hh + 1) * HEAD_DIM],
                        probs[kk].astype(_BF16), preferred_element_type=_F32)
                for kk in range(BAND_TILES)])
            heads.append(o_t / denom)
        cols.append(jnp.concatenate(heads, axis=0))
    _pair_rmsnorm_store(jnp.concatenate(cols, axis=1), gain_ref[...], o_ref)


def _band_bias_tiles(rel_table):
    t = ATT_T
    key = np.arange(t)[:, None]
    qry = np.arange(t)[None, :]
    shift = np.arange(2 * t + 1) - t
    vec_idx, mask = [], []
    for kk in range(BAND_TILES):
        back = (BAND_TILES - 1 - kk) * t
        vec_idx.append(np.clip(back + shift, -(CHUNK - 1), MAX_REL) + (CHUNK - 1))
        dc = back // CHUNK + qry // CHUNK - key // CHUNK
        mask.append((dc >= 0) & (dc <= LEFT_CHUNKS))
    vec = rel_table[:, np.stack(vec_idx)] * LOG2E
    toe = jnp.tile(vec, (1, 1, t))[..., :t * 2 * t].reshape(N_HEADS, BAND_TILES, t, 2 * t)[..., t:]
    tiles = jnp.where(np.stack(mask), toe, -jnp.inf)
    return tiles.reshape(N_HEADS // 2, 2, BAND_TILES, t, t)


def _band_attention(qkv, bias_tiles, gain_col, layer):
    return pl.pallas_call(
        _band_kernel,
        grid=(N_HEADS // 2, SEQ // BAND_TQ),
        in_specs=[
            pl.BlockSpec((None, 2, BAND_TILES, ATT_T, ATT_T), lambda p, i: (p, 0, 0, 0, 0)),
            pl.BlockSpec((None, None, LANES, 1), lambda p, i: (layer, p, 0, 0)),
            pl.BlockSpec((BAND_TQ, LANES), lambda p, i: (i, QC + p)),
            pl.BlockSpec((SEQ, LANES), lambda p, i: (0, KC + p)),
            pl.BlockSpec((SEQ, LANES), lambda p, i: (0, VC + p)),
        ],
        out_specs=pl.BlockSpec((BAND_TQ, LANES), lambda p, i: (i, p)),
        out_shape=jax.ShapeDtypeStruct((SEQ, CH_WIDTH), _BF16),
        scratch_shapes=[pltpu.VMEM((N_QBLK, LANES, ATT_T), _BF16)],
        compiler_params=_cparams(2),
        name="band_attention",
    )(bias_tiles, gain_col, qkv, qkv, qkv)


def _oproj_kernel(y_ref, w_ref, x_ref, o_ref):
    o_ref[...] = x_ref[...] + jnp.dot(y_ref[...], w_ref[...], preferred_element_type=_F32)


def _out_proj(y, w_all, x, layer, tm=1024, tn=512):
    return pl.pallas_call(
        _oproj_kernel,
        grid=(SEQ // tm, D_MODEL // tn),
        in_specs=[
            pl.BlockSpec((tm, D_MODEL), lambda i, j: (i, 0)),
            pl.BlockSpec((None, D_MODEL, tn), lambda i, j: (layer, 0, j)),
            pl.BlockSpec((tm, tn), lambda i, j: (i, j)),
        ],
        out_specs=pl.BlockSpec((tm, tn), lambda i, j: (i, j)),
        out_shape=jax.ShapeDtypeStruct((SEQ, D_MODEL), _F32),
        compiler_params=_cparams(2),
        name="out_proj",
    )(y, w_all, x)


def _ffn_kernel(x_ref, halo_ref, g_ref, wg_ref, wu_ref, cw_ref, cb_ref, wd_ref, o_ref,
                h_scr, gate_scr, *, tm):
    i = pl.program_id(0)
    j = pl.program_id(1)

    def norm(x):
        ms = jnp.mean(x * x, axis=-1, keepdims=True)
        return (x * lax.rsqrt(ms + NORM_EPS) * g_ref[...]).astype(_BF16)

    @pl.when(j == 0)
    def _():
        x = x_ref[...]
        h_scr[HALO:, :] = norm(x)
        halo = jnp.where(i > 0, halo_ref[...], 0.0)
        h_scr[:HALO, :] = norm(halo)
        o_ref[...] = x

    gate_scr[...] = jnp.dot(h_scr[...], wg_ref[...], preferred_element_type=_F32)
    up = jnp.dot(h_scr[HALO:, :], wu_ref[...], preferred_element_type=_F32)
    cw = cw_ref[...]
    g = cb_ref[...]
    for t in range(CONV_WIDTH):
        lo = HALO - (CONV_WIDTH - 1) + t
        g = g + gate_scr[lo:lo + tm, :] * cw[t:t + 1]
    act = (g * jax.nn.sigmoid(g) * up).astype(_BF16)
    o_ref[...] += jnp.dot(act, wd_ref[...], preferred_element_type=_F32)


def _ffn(x, g_all, wg_all, wu_all, cw_all, cb_all, wd_all, layer, tm=512, tf=512):
    halo_blocks = tm // HALO
    return pl.pallas_call(
        functools.partial(_ffn_kernel, tm=tm),
        grid=(SEQ // tm, D_FF // tf),
        in_specs=[
            pl.BlockSpec((tm, D_MODEL), lambda i, j: (i, 0)),
            pl.BlockSpec((HALO, D_MODEL),
                         lambda i, j: (jnp.maximum(i * halo_blocks - 1, 0), 0)),
            pl.BlockSpec((None, 1, D_MODEL), lambda i, j: (layer, 0, 0)),
            pl.BlockSpec((None, D_MODEL, tf), lambda i, j: (layer, 0, j)),
            pl.BlockSpec((None, D_MODEL, tf), lambda i, j: (layer, 0, j)),
            pl.BlockSpec((None, CONV_WIDTH, tf), lambda i, j: (layer, 0, j)),
            pl.BlockSpec((None, 1, tf), lambda i, j: (layer, 0, j)),
            pl.BlockSpec((None, tf, D_MODEL), lambda i, j: (layer, j, 0)),
        ],
        out_specs=pl.BlockSpec((tm, D_MODEL), lambda i, j: (i, 0)),
        out_shape=jax.ShapeDtypeStruct((SEQ, D_MODEL), _F32),
        scratch_shapes=[
            pltpu.VMEM((tm + HALO, D_MODEL), _BF16),
            pltpu.VMEM((tm + HALO, tf), _F32),
        ],
        compiler_params=_cparams(2),
        name="gated_mlp",
    )(x, x, g_all, wg_all, wu_all, cw_all, cb_all, wd_all)


def _final_norm_kernel(x_ref, g_ref, o_ref):
    x = x_ref[...]
    ms = jnp.mean(x * x, axis=-1, keepdims=True)
    o_ref[...] = x * lax.rsqrt(ms + NORM_EPS) * g_ref[...]


def _final_norm(x, g, tm=512):
    return pl.pallas_call(
        _final_norm_kernel,
        grid=(SEQ // tm,),
        in_specs=[
            pl.BlockSpec((tm, D_MODEL), lambda i: (i, 0)),
            pl.BlockSpec((1, D_MODEL), lambda i: (0, 0)),
        ],
        out_specs=pl.BlockSpec((tm, D_MODEL), lambda i: (i, 0)),
        out_shape=jax.ShapeDtypeStruct((SEQ, D_MODEL), _F32),
        compiler_params=_cparams(1),
        name="final_norm",
    )(x, g)


def kernel(x, attn_norm, w_qkv, lambda_q1, lambda_k1, lambda_q2, lambda_k2, diff_subln,
           sb_norm, rel_bias, ch_norm, w_o, ffn_norm, w_gate, w_up, conv_w, conv_b, w_down,
           final_norm):
    batch, seq, d_model = x.shape
    assert (batch, seq, d_model) == (1, SEQ, D_MODEL)
    x2 = x.reshape(SEQ, D_MODEL)

    w_qkv_b = w_qkv.astype(_BF16)
    w_o_b = w_o.astype(_BF16)
    w_gate_b = w_gate.astype(_BF16)
    w_up_b = w_up.astype(_BF16)
    w_down_b = w_down.astype(_BF16)

    attn_g = attn_norm.reshape(DEPTH, 1, D_MODEL)
    ffn_g = ffn_norm.reshape(DEPTH, 1, D_MODEL)
    conv_b3 = conv_b.reshape(DEPTH, 1, D_FF)
    lam_vecs = jnp.stack([lambda_q1, lambda_k1, lambda_q2, lambda_k2], axis=1)
    subln_col = diff_subln.reshape(DEPTH, 2 * HEAD_DIM, 1)
    sb_col = sb_norm.reshape(DEPTH, N_HEADS // 2, 2 * HEAD_DIM, 1)
    ch_col = ch_norm.reshape(DEPTH, N_HEADS // 2, 2 * HEAD_DIM, 1)
    slopes = _alibi_slope_table()
    col_scale = jnp.asarray(_qkv_col_scale())

    for layer in range(DEPTH):
        lam_init = 0.8 - 0.6 * math.exp(-0.3 * layer)
        qkv = _qkv_proj(x2, attn_g, w_qkv_b, col_scale, layer)
        y_a = _diff_attention(qkv, slopes, lam_vecs, subln_col, layer, lam_init)
        y_b = _sb_attention(qkv, sb_col, layer)
        y_c = _band_attention(qkv, _band_bias_tiles(rel_bias[layer]), ch_col, layer)
        y = jnp.concatenate([y_a, y_b, y_c], axis=-1)
        x2 = _out_proj(y, w_o_b, x2, layer)
        x2 = _ffn(x2, ffn_g, w_gate_b, w_up_b, conv_w, conv_b3, w_down_b, layer)
    out = _final_norm(x2, final_norm.reshape(1, D_MODEL))
    return out.reshape(1, SEQ, D_MODEL)
```

```python
import functools
import math

import jax
import jax.numpy as jnp
import numpy as np
from jax import lax
from jax.experimental import pallas as pl
from jax.experimental.pallas import tpu as pltpu

D_MODEL = 2048
SEQ = 8192
DEPTH = 4
CHUNK = 64
HEAD_DIM = 64
N_HEADS = 8
DIFF_WIDTH = 1024
SB_WIDTH = 512
CH_WIDTH = 512
QKV_WIDTH = 6144
LEFT_CHUNKS = 8
MAX_REL = 128
D_FF = 5632
CONV_WIDTH = 3
NORM_EPS = 1e-6
SUBLN_EPS = 1e-5
SCALE = HEAD_DIM ** -0.5
LOG2E = math.log2(math.e)

LANES = 128
ATT_T = 256
N_QBLK = SEQ // ATT_T
HALO = 16
VMEM_LIMIT = 48 * 1024 * 1024

QA, KA, VA = 0, 8, 16
QB, KB, VB = 24, 28, 32
QC, KC, VC = 36, 40, 44

_F32 = jnp.float32
_BF16 = jnp.bfloat16
_NEG = -1e30
_SIGN_BIT = np.int32(-2 ** 31)


def _cparams(n_axes):
    return pltpu.CompilerParams(
        dimension_semantics=("arbitrary",) * n_axes,
        vmem_limit_bytes=VMEM_LIMIT)


def _qkv_kernel(x_ref, g_ref, w_ref, cs_ref, o_ref, h_scr):
    @pl.when(pl.program_id(1) == 0)
    def _():
        x = x_ref[...]
        ms = jnp.mean(x * x, axis=-1, keepdims=True)
        h_scr[...] = (x * lax.rsqrt(ms + NORM_EPS) * g_ref[...]).astype(_BF16)

    acc = jnp.dot(h_scr[...], w_ref[...].astype(_BF16), preferred_element_type=_F32)
    o_ref[...] = (acc * cs_ref[...]).astype(_BF16)


def _qkv_col_scale():
    cs = np.ones((1, QKV_WIDTH), np.float32)
    for q_start, k_start in ((QA, KA), (QB, KB), (QC, KC)):
        cs[:, q_start * LANES:k_start * LANES] = SCALE * LOG2E
    return cs


def _qkv_proj(x, g_all, w_all, col_scale, layer, tm=1024, tn=512):
    return pl.pallas_call(
        _qkv_kernel,
        grid=(SEQ // tm, QKV_WIDTH // tn),
        in_specs=[
            pl.BlockSpec((tm, D_MODEL), lambda i, j: (i, 0)),
            pl.BlockSpec((None, 1, D_MODEL), lambda i, j: (layer, 0, 0)),
            pl.BlockSpec((None, D_MODEL, tn), lambda i, j: (layer, 0, j)),
            pl.BlockSpec((1, tn), lambda i, j: (0, j)),
        ],
        out_specs=pl.BlockSpec((tm, tn), lambda i, j: (i, j)),
        out_shape=jax.ShapeDtypeStruct((SEQ, QKV_WIDTH), _BF16),
        scratch_shapes=[pltpu.VMEM((tm, D_MODEL), _BF16)],
        compiler_params=_cparams(2),
        name="qkv_proj",
    )(x, g_all, w_all, col_scale)


def _split_heads_t(q_ref):
    q_t = q_ref[...].T
    dim = lax.broadcasted_iota(jnp.int32, q_t.shape, 0)
    zero = jnp.zeros_like(q_t)
    return jnp.where(dim < HEAD_DIM, q_t, zero), jnp.where(dim >= HEAD_DIM, q_t, zero)


def _key_block(ref, j):
    return ref[pl.ds(pl.multiple_of(j * ATT_T, ATT_T), ATT_T), :]


def _pair_rmsnorm_store(o_t, gain_col, o_ref):
    top, bot = o_t[:HEAD_DIM], o_t[HEAD_DIM:]
    top = top * lax.rsqrt(jnp.mean(top * top, axis=0, keepdims=True) + NORM_EPS)
    bot = bot * lax.rsqrt(jnp.mean(bot * bot, axis=0, keepdims=True) + NORM_EPS)
    y = jnp.concatenate([top, bot], axis=0) * gain_col
    o_ref[...] = y.T.astype(o_ref.dtype)


DIFF_TQ = 512
DIFF_TK = 256
DIFF_DIAG = DIFF_TQ // DIFF_TK
assert DIFF_DIAG == 2
DIFF_SPLIT = 3
DIFF_EXTRA = 16


def _diff_kernel(slope_ref, lam_ref, gain_ref, q_ref, k_ref, v_ref, o_ref,
                 diag_scr, vt_scr, s_scr, mx_scr, p_scr, al_scr, m_scr, acc_scr, *, lam_init):
    h = pl.program_id(0)
    i = pl.program_id(1)
    slope2 = slope_ref[h, 0]
    nb = (i + 1) * DIFF_DIAG

    @pl.when(i == 0)
    def _():
        row = lax.broadcasted_iota(jnp.int32, (DIFF_TK, DIFF_TQ), 0)
        col = lax.broadcasted_iota(jnp.int32, (DIFF_TK, DIFF_TQ), 1)
        for d in range(DIFF_DIAG):
            key = row + d * DIFF_TK
            allowed = (key // CHUNK) <= (col // CHUNK)
            lead = (col - row).astype(_F32) - jnp.abs(col - key).astype(_F32)
            diag_scr[d] = jnp.where(allowed, slope2 * lead, -jnp.inf)

        rows = lax.broadcasted_iota(jnp.int32, (DIFF_EXTRA, DIFF_TK), 0)
        ones_rows = jnp.where(rows == 0, 1.0, 0.0).astype(_BF16)

        def transpose_block(b, carry):
            vt_scr[b, :LANES] = v_ref[pl.ds(pl.multiple_of(b * DIFF_TK, DIFF_TK), DIFF_TK), :].T
            vt_scr[b, LANES:] = ones_rows
            return carry

        lax.fori_loop(0, SEQ // DIFF_TK, transpose_block, 0)

    dim = lax.broadcasted_iota(jnp.int32, (LANES, DIFF_TQ), 0)
    slope_rows = functools.reduce(
        lambda acc, n: jnp.where(dim == n, slope_ref[h, 1 + n], acc), range(DIFF_SPLIT),
        jnp.zeros((LANES, DIFF_TQ), _F32)).astype(_BF16)
    q_maps = tuple(jnp.concatenate([q_half, slope_rows], axis=0)
                   for q_half in _split_heads_t(q_ref))
    lane = lax.broadcasted_iota(jnp.int32, (DIFF_TK, LANES), 1)
    key_offset = lax.broadcasted_iota(jnp.int32, (DIFF_TK, LANES), 0).astype(_F32)
    key_feat = jnp.where(lane < DIFF_SPLIT, key_offset, 0.0).astype(_BF16)

    m_scr[...] = jnp.full(m_scr.shape, _NEG, _F32)
    acc_scr[...] = jnp.zeros(acc_scr.shape, _F32)

    def tile_offset(t, diagonal):
        if diagonal is not None:
            return jnp.zeros((), _F32)
        return -slope2 * (i * DIFF_TQ - t * DIFF_TK).astype(_F32)

    def stage_a(t, slot, diagonal=None):
        kb = k_ref[pl.ds(pl.multiple_of(t * DIFF_TK, DIFF_TK), DIFF_TK), :]
        kx = jnp.concatenate([kb, key_feat], axis=1)
        offset = tile_offset(t, diagonal)
        for c in range(2):
            s = jnp.dot(kx, q_maps[c], preferred_element_type=_F32)
            if diagonal is not None:
                s = s + diag_scr[diagonal]
            s_scr[slot, c] = s
            mx_scr[slot, c] = jnp.max(s, axis=0, keepdims=True) + offset

    def stage_b(t, slot, diagonal=None):
        offset = tile_offset(t, diagonal)
        for c in range(2):
            m_old = m_scr[c]
            m_new = jnp.maximum(m_old, mx_scr[slot, c])
            p_scr[slot, c] = jnp.exp2(s_scr[slot, c] - (m_new - offset)).astype(_BF16)
            al_scr[slot, c] = jnp.exp2(m_old - m_new)
            m_scr[c] = m_new

    def stage_c(t, slot):
        vt = vt_scr[t]
        for c in range(2):
            pv = jnp.dot(vt, p_scr[slot, c], preferred_element_type=_F32)
            acc_scr[c] = al_scr[slot, c] * acc_scr[c] + pv

    def drain():
        stage_c(nb - 2, 0)
        stage_b(nb - 1, 1, diagonal=1)
        stage_c(nb - 1, 1)

    @pl.when(i == 0)
    def _():
        stage_a(0, 0, diagonal=0)
        stage_b(0, 0, diagonal=0)
        stage_a(1, 1, diagonal=1)
        drain()

    @pl.when(i > 0)
    def _():
        stage_a(0, 0)
        stage_b(0, 0)
        stage_a(1, 1)

        def two_steps(t):
            stage_c(t - 2, 0)
            stage_b(t - 1, 1)
            stage_a(t, 0)
            stage_c(t - 1, 1)
            stage_b(t, 0)
            stage_a(t + 1, 1)

        odd = (i - 1) % 2

        @pl.when(odd == 1)
        def _():
            two_steps(2)

        def body(u, carry):
            t = 2 + 2 * odd + 4 * u
            two_steps(t)
            two_steps(t + 2)
            return carry

        lax.fori_loop(0, (i - 1) // 2, body, 0)
        t = nb - 2
        stage_c(t - 2, 0)
        stage_b(t - 1, 1)
        stage_a(t, 0, diagonal=0)
        stage_c(t - 1, 1)
        stage_b(t, 0, diagonal=0)
        stage_a(t + 1, 1, diagonal=1)
        drain()

    lam_v = lam_ref[...]
    lam = (jnp.exp(jnp.sum(lam_v[0:1] * lam_v[1:2], axis=-1, keepdims=True))
           - jnp.exp(jnp.sum(lam_v[2:3] * lam_v[3:4], axis=-1, keepdims=True))
           + lam_init)
    acc0, acc1 = acc_scr[0], acc_scr[1]
    o_t = (acc0[:LANES] / acc0[LANES:LANES + 1]
           - lam * (acc1[:LANES] / acc1[LANES:LANES + 1]))
    ms = jnp.mean(o_t * o_t, axis=0, keepdims=True)
    y = o_t * lax.rsqrt(ms + SUBLN_EPS) * gain_ref[...] * (1.0 - lam_init)
    o_ref[...] = y.T.astype(o_ref.dtype)


def _alibi_slope_table():
    slope = 2.0 ** (-8.0 * np.arange(1, N_HEADS + 1, dtype=np.float32) / N_HEADS)
    slope2 = (slope * np.float32(LOG2E)).astype(np.float32)
    cols, rest = [slope2], slope2
    for _ in range(DIFF_SPLIT):
        term = rest.astype(_BF16).astype(np.float32)
        cols.append(term)
        rest = rest - term
    return jnp.asarray(np.stack(cols, axis=1))


def _diff_attention(qkv, slopes, lam_vecs, gain_col, layer, lam_init):
    tile = (DIFF_TK, DIFF_TQ)
    return pl.pallas_call(
        functools.partial(_diff_kernel, lam_init=lam_init),
        grid=(N_HEADS, SEQ // DIFF_TQ),
        in_specs=[
            pl.BlockSpec(memory_space=pltpu.SMEM),
            pl.BlockSpec((None, 4, HEAD_DIM), lambda h, i: (layer, 0, 0)),
            pl.BlockSpec((None, LANES, 1), lambda h, i: (layer, 0, 0)),
            pl.BlockSpec((DIFF_TQ, LANES), lambda h, i: (i, QA + h)),
            pl.BlockSpec((SEQ, LANES), lambda h, i: (0, KA + h)),
            pl.BlockSpec((SEQ, LANES), lambda h, i: (0, VA + h)),
        ],
        out_specs=pl.BlockSpec((DIFF_TQ, LANES), lambda h, i: (i, h)),
        out_shape=jax.ShapeDtypeStruct((SEQ, DIFF_WIDTH), _BF16),
        scratch_shapes=[
            pltpu.VMEM((DIFF_DIAG,) + tile, _F32),
            pltpu.VMEM((SEQ // DIFF_TK, LANES + DIFF_EXTRA, DIFF_TK), _BF16),
            pltpu.VMEM((2, 2) + tile, _F32),
            pltpu.VMEM((2, 2, 1, DIFF_TQ), _F32),
            pltpu.VMEM((2, 2) + tile, _BF16),
            pltpu.VMEM((2, 2, 1, DIFF_TQ), _F32),
            pltpu.VMEM((2, 1, DIFF_TQ), _F32),
            pltpu.VMEM((2, LANES + DIFF_EXTRA, DIFF_TQ), _F32),
        ],
        compiler_params=_cparams(2),
        name="diff_attention",
    )(slopes, lam_vecs, gain_col, qkv, qkv, qkv)


SB_TQ = 512
SB_TK = 256
SB_SUB = 128
SB_DIAG = SB_TQ // SB_TK
assert SB_DIAG == 2


def _sb_kernel(gain_ref, q_ref, k_ref, v_ref, o_ref,
               mask_scr, tri_scr, vt_scr, z_scr, hl_scr, a_scr, r_scr, acc_scr):
    i = pl.program_id(1)
    nb = (i + 1) * SB_DIAG

    @pl.when(i == 0)
    def _():
        row = lax.broadcasted_iota(jnp.int32, (SB_TK, SB_TQ), 0)
        col = lax.broadcasted_iota(jnp.int32, (SB_TK, SB_TQ), 1)
        for d in range(SB_DIAG):
            mask_scr[d] = jnp.where(row + d * SB_TK < col, 0.0, -jnp.inf)
        r2 = lax.broadcasted_iota(jnp.int32, (SB_SUB, 2 * SB_SUB), 0)
        c2 = lax.broadcasted_iota(jnp.int32, (SB_SUB, 2 * SB_SUB), 1) % SB_SUB
        tri_scr[...] = jnp.where(c2 >= r2, 1.0, 0.0).astype(_BF16)

        def transpose_block(b, carry):
            vt_scr[b] = v_ref[pl.ds(pl.multiple_of(b * SB_TK, SB_TK), SB_TK), :].T
            return carry

        lax.fori_loop(0, SEQ // SB_TK, transpose_block, 0)

    q_heads = _split_heads_t(q_ref)

    r_scr[...] = jnp.zeros(r_scr.shape, _F32)
    acc_scr[...] = jnp.zeros(acc_scr.shape, _F32)

    def key_block(t):
        return nb - 1 - t

    def stage_a(t, slot, diagonal=None):
        kb = k_ref[pl.ds(pl.multiple_of(key_block(t) * SB_TK, SB_TK), SB_TK), :]
        for hh in range(2):
            z = jnp.dot(kb, q_heads[hh], preferred_element_type=_F32)
            if diagonal is not None:
                z = z + mask_scr[diagonal]
            neg_abs = pltpu.bitcast(pltpu.bitcast(z, jnp.int32) | _SIGN_BIT, _F32)
            nl = jnp.maximum(z, 0.0) + jnp.log2(1.0 + jnp.exp2(neg_abs))
            hi = nl.astype(_BF16)
            lo = (nl - hi.astype(_F32)).astype(_BF16)
            z_scr[slot, hh] = z
            for sub in range(SB_TK // SB_SUB):
                rows = slice(sub * SB_SUB, (sub + 1) * SB_SUB)
                hl_scr[slot, hh, sub, :SB_SUB] = hi[rows]
                hl_scr[slot, hh, sub, SB_SUB:] = lo[rows]

    def stage_b(slot):
        tri = tri_scr[...]
        for hh in range(2):
            right = r_scr[hh]
            for sub in reversed(range(SB_TK // SB_SUB)):
                rows = slice(sub * SB_SUB, (sub + 1) * SB_SUB)
                incl = jnp.dot(tri, hl_scr[slot, hh, sub], preferred_element_type=_F32)
                a_scr[slot, hh, rows] = jnp.exp2(z_scr[slot, hh, rows] - incl - right).astype(_BF16)
                right = right + incl[0:1]
            r_scr[hh] = right

    def stage_c(t, slot):
        vt = vt_scr[key_block(t)]
        for hh in range(2):
            acc_scr[hh] += jnp.dot(vt[hh * HEAD_DIM:(hh + 1) * HEAD_DIM], a_scr[slot, hh],
                                   preferred_element_type=_F32)

    stage_a(0, 0, diagonal=1)
    stage_b(0)
    stage_a(1, 1, diagonal=0)

    def two_steps(t):
        stage_c(t - 2, 0)
        stage_b(1)
        stage_a(t, 0)
        stage_c(t - 1, 1)
        stage_b(0)
        stage_a(t + 1, 1)

    odd = i % 2

    @pl.when(odd == 1)
    def _():
        two_steps(2)

    def body(u, carry):
        t = 2 + 2 * odd + 4 * u
        two_steps(t)
        two_steps(t + 2)
        return carry

    lax.fori_loop(0, i // 2, body, 0)
    stage_c(nb - 2, 0)
    stage_b(1)
    stage_c(nb - 1, 1)

    o_t = jnp.concatenate([acc_scr[0], acc_scr[1]], axis=0)
    _pair_rmsnorm_store(o_t, gain_ref[...], o_ref)


def _sb_attention(qkv, gain_col, layer):
    tile = (SB_TK, SB_TQ)
    return pl.pallas_call(
        _sb_kernel,
        grid=(N_HEADS // 2, SEQ // SB_TQ),
        in_specs=[
            pl.BlockSpec((None, None, LANES, 1), lambda p, i: (layer, p, 0, 0)),
            pl.BlockSpec((SB_TQ, LANES), lambda p, i: (i, QB + p)),
            pl.BlockSpec((SEQ, LANES), lambda p, i: (0, KB + p)),
            pl.BlockSpec((SEQ, LANES), lambda p, i: (0, VB + p)),
        ],
        out_specs=pl.BlockSpec((SB_TQ, LANES), lambda p, i: (i, p)),
        out_shape=jax.ShapeDtypeStruct((SEQ, SB_WIDTH), _BF16),
        scratch_shapes=[
            pltpu.VMEM((SB_DIAG,) + tile, _F32),
            pltpu.VMEM((SB_SUB, 2 * SB_SUB), _BF16),
            pltpu.VMEM((SEQ // SB_TK, LANES, SB_TK), _BF16),
            pltpu.VMEM((2, 2) + tile, _F32),
            pltpu.VMEM((2, 2, SB_TK // SB_SUB, 2 * SB_SUB, SB_TQ), _BF16),
            pltpu.VMEM((2, 2) + tile, _BF16),
            pltpu.VMEM((2, 1, SB_TQ), _F32),
            pltpu.VMEM((2, HEAD_DIM, SB_TQ), _F32),
        ],
        compiler_params=_cparams(2),
        name="sb_attention",
    )(gain_col, qkv, qkv, qkv)


BAND_TILES = 1 + (LEFT_CHUNKS * CHUNK) // ATT_T


BAND_TQ = 512
BAND_SUBS = BAND_TQ // ATT_T


def _band_kernel(bias_ref, gain_ref, q_ref, k_ref, v_ref, o_ref, vt_scr):
    i = pl.program_id(1)

    @pl.when(i == 0)
    def _():
        def transpose_block(b, carry):
            vt_scr[b] = _key_block(v_ref, b).T
            return carry

        lax.fori_loop(0, N_QBLK, transpose_block, 0)

    q_heads = _split_heads_t(q_ref)

    first = i * BAND_SUBS - (BAND_TILES - 1)
    tiles = [jnp.maximum(first + n, 0) for n in range(BAND_TILES + BAND_SUBS - 1)]
    k_tiles = [_key_block(k_ref, j) for j in tiles]
    vt_tiles = [vt_scr[j] for j in tiles]

    cols = []
    for sub in range(BAND_SUBS):
        heads = []
        for hh in range(2):
            q_sub = q_heads[hh][:, sub * ATT_T:(sub + 1) * ATT_T]
            scores = []
            for kk in range(BAND_TILES):
                s = jnp.dot(k_tiles[sub + kk], q_sub, preferred_element_type=_F32) + bias_ref[hh, kk]
                scores.append(jnp.where(first + sub + kk >= 0, s, -jnp.inf))
            m = functools.reduce(jnp.maximum,
                                 [jnp.max(s, axis=0, keepdims=True) for s in scores])
            probs = [jnp.exp2(s - m) for s in scores]
            denom = functools.reduce(jnp.add,
                                     [jnp.sum(p, axis=0, keepdims=True) for p in probs])
            o_t = functools.reduce(jnp.add, [
                jnp.dot(vt_tiles[sub + kk][hh * HEAD_DIM:(hh + 1) * HEAD_DIM],
                        probs[kk].astype(_BF16), preferred_element_type=_F32)
                for kk in range(BAND_TILES)])
            heads.append(o_t / denom)
        cols.append(jnp.concatenate(heads, axis=0))
    _pair_rmsnorm_store(jnp.concatenate(cols, axis=1), gain_ref[...], o_ref)


def _band_bias_tiles(rel_table):
    t = ATT_T
    key = np.arange(t)[:, None]
    qry = np.arange(t)[None, :]
    shift = np.arange(2 * t + 1) - t
    vec_idx, mask = [], []
    for kk in range(BAND_TILES):
        back = (BAND_TILES - 1 - kk) * t
        vec_idx.append(np.clip(back + shift, -(CHUNK - 1), MAX_REL) + (CHUNK - 1))
        dc = back // CHUNK + qry // CHUNK - key // CHUNK
        mask.append((dc >= 0) & (dc <= LEFT_CHUNKS))
    vec = rel_table[:, np.stack(vec_idx)] * LOG2E
    toe = jnp.tile(vec, (1, 1, t))[..., :t * 2 * t].reshape(N_HEADS, BAND_TILES, t, 2 * t)[..., t:]
    tiles = jnp.where(np.stack(mask), toe, -jnp.inf)
    return tiles.reshape(N_HEADS // 2, 2, BAND_TILES, t, t)


def _band_attention(qkv, bias_tiles, gain_col, layer):
    return pl.pallas_call(
        _band_kernel,
        grid=(N_HEADS // 2, SEQ // BAND_TQ),
        in_specs=[
            pl.BlockSpec((None, 2, BAND_TILES, ATT_T, ATT_T), lambda p, i: (p, 0, 0, 0, 0)),
            pl.BlockSpec((None, None, LANES, 1), lambda p, i: (layer, p, 0, 0)),
            pl.BlockSpec((BAND_TQ, LANES), lambda p, i: (i, QC + p)),
            pl.BlockSpec((SEQ, LANES), lambda p, i: (0, KC + p)),
            pl.BlockSpec((SEQ, LANES), lambda p, i: (0, VC + p)),
        ],
        out_specs=pl.BlockSpec((BAND_TQ, LANES), lambda p, i: (i, p)),
        out_shape=jax.ShapeDtypeStruct((SEQ, CH_WIDTH), _BF16),
        scratch_shapes=[pltpu.VMEM((N_QBLK, LANES, ATT_T), _BF16)],
        compiler_params=_cparams(2),
        name="band_attention",
    )(bias_tiles, gain_col, qkv, qkv, qkv)


def _oproj_kernel(ya_ref, yb_ref, yc_ref, w_ref, x_ref, o_ref, w_scr):
    @pl.when(pl.program_id(1) == 0)
    def _():
        w_scr[...] = w_ref[...].astype(_BF16)

    y = jnp.concatenate([ya_ref[...], yb_ref[...], yc_ref[...]], axis=1)
    o_ref[...] = x_ref[...] + jnp.dot(y, w_scr[...], preferred_element_type=_F32)


def _out_proj(y_a, y_b, y_c, w_all, x, layer, tm=1024, tn=512):
    return pl.pallas_call(
        _oproj_kernel,
        grid=(D_MODEL // tn, SEQ // tm),
        in_specs=[
            pl.BlockSpec((tm, DIFF_WIDTH), lambda j, i: (i, 0)),
            pl.BlockSpec((tm, SB_WIDTH), lambda j, i: (i, 0)),
            pl.BlockSpec((tm, CH_WIDTH), lambda j, i: (i, 0)),
            pl.BlockSpec((None, D_MODEL, tn), lambda j, i: (layer, 0, j)),
            pl.BlockSpec((tm, tn), lambda j, i: (i, j)),
        ],
        out_specs=pl.BlockSpec((tm, tn), lambda j, i: (i, j)),
        out_shape=jax.ShapeDtypeStruct((SEQ, D_MODEL), _F32),
        scratch_shapes=[pltpu.VMEM((D_MODEL, tn), _BF16)],
        compiler_params=_cparams(2),
        name="out_proj",
    )(y_a, y_b, y_c, w_all, x)


def _ffn_kernel(x_ref, halo_ref, g_ref, wg_ref, wu_ref, cw_ref, cb_ref, wd_ref, o_ref,
                h_scr, gate_scr, *, tm):
    i = pl.program_id(0)
    j = pl.program_id(1)

    def norm(x):
        ms = jnp.mean(x * x, axis=-1, keepdims=True)
        return (x * lax.rsqrt(ms + NORM_EPS) * g_ref[...]).astype(_BF16)

    @pl.when(j == 0)
    def _():
        x = x_ref[...]
        h_scr[HALO:, :] = norm(x)
        halo = jnp.where(i > 0, halo_ref[...], 0.0)
        h_scr[:HALO, :] = norm(halo)
        o_ref[...] = x

    gate_scr[...] = jnp.dot(h_scr[...], wg_ref[...], preferred_element_type=_F32)
    up = jnp.dot(h_scr[HALO:, :], wu_ref[...], preferred_element_type=_F32)
    cw = cw_ref[...]
    g = cb_ref[...]
    for t in range(CONV_WIDTH):
        lo = HALO - (CONV_WIDTH - 1) + t
        g = g + gate_scr[lo:lo + tm, :] * cw[t:t + 1]
    act = (g * jax.nn.sigmoid(g) * up).astype(_BF16)
    o_ref[...] += jnp.dot(act, wd_ref[...], preferred_element_type=_F32)


def _ffn(x, g_all, wg_all, wu_all, cw_all, cb_all, wd_all, layer, tm=512, tf=512):
    halo_blocks = tm // HALO
    return pl.pallas_call(
        functools.partial(_ffn_kernel, tm=tm),
        grid=(SEQ // tm, D_FF // tf),
        in_specs=[
            pl.BlockSpec((tm, D_MODEL), lambda i, j: (i, 0)),
            pl.BlockSpec((HALO, D_MODEL),
                         lambda i, j: (jnp.maximum(i * halo_blocks - 1, 0), 0)),
            pl.BlockSpec((None, 1, D_MODEL), lambda i, j: (layer, 0, 0)),
            pl.BlockSpec((None, D_MODEL, tf), lambda i, j: (layer, 0, j)),
            pl.BlockSpec((None, D_MODEL, tf), lambda i, j: (layer, 0, j)),
            pl.BlockSpec((None, CONV_WIDTH, tf), lambda i, j: (layer, 0, j)),
            pl.BlockSpec((None, 1, tf), lambda i, j: (layer, 0, j)),
            pl.BlockSpec((None, tf, D_MODEL), lambda i, j: (layer, j, 0)),
        ],
        out_specs=pl.BlockSpec((tm, D_MODEL), lambda i, j: (i, 0)),
        out_shape=jax.ShapeDtypeStruct((SEQ, D_MODEL), _F32),
        scratch_shapes=[
            pltpu.VMEM((tm + HALO, D_MODEL), _BF16),
            pltpu.VMEM((tm + HALO, tf), _F32),
        ],
        compiler_params=_cparams(2),
        name="gated_mlp",
    )(x, x, g_all, wg_all, wu_all, cw_all, cb_all, wd_all)


def _final_norm_kernel(x_ref, g_ref, o_ref):
    x = x_ref[...]
    ms = jnp.mean(x * x, axis=-1, keepdims=True)
    o_ref[...] = x * lax.rsqrt(ms + NORM_EPS) * g_ref[...]


def _final_norm(x, g, tm=512):
    return pl.pallas_call(
        _final_norm_kernel,
        grid=(SEQ // tm,),
        in_specs=[
            pl.BlockSpec((tm, D_MODEL), lambda i: (i, 0)),
            pl.BlockSpec((1, D_MODEL), lambda i: (0, 0)),
        ],
        out_specs=pl.BlockSpec((tm, D_MODEL), lambda i: (i, 0)),
        out_shape=jax.ShapeDtypeStruct((SEQ, D_MODEL), _F32),
        compiler_params=_cparams(1),
        name="final_norm",
    )(x, g)


def kernel(x, attn_norm, w_qkv, lambda_q1, lambda_k1, lambda_q2, lambda_k2, diff_subln,
           sb_norm, rel_bias, ch_norm, w_o, ffn_norm, w_gate, w_up, conv_w, conv_b, w_down,
           final_norm):
    batch, seq, d_model = x.shape
    assert (batch, seq, d_model) == (1, SEQ, D_MODEL)
    x2 = x.reshape(SEQ, D_MODEL)

    w_gate_b = w_gate.astype(_BF16)
    w_up_b = w_up.astype(_BF16)
    w_down_b = w_down.astype(_BF16)

    attn_g = attn_norm.reshape(DEPTH, 1, D_MODEL)
    ffn_g = ffn_norm.reshape(DEPTH, 1, D_MODEL)
    conv_b3 = conv_b.reshape(DEPTH, 1, D_FF)
    lam_vecs = jnp.stack([lambda_q1, lambda_k1, lambda_q2, lambda_k2], axis=1)
    subln_col = diff_subln.reshape(DEPTH, 2 * HEAD_DIM, 1)
    sb_col = sb_norm.reshape(DEPTH, N_HEADS // 2, 2 * HEAD_DIM, 1)
    ch_col = ch_norm.reshape(DEPTH, N_HEADS // 2, 2 * HEAD_DIM, 1)
    slopes = _alibi_slope_table()
    col_scale = jnp.asarray(_qkv_col_scale())

    for layer in range(DEPTH):
        lam_init = 0.8 - 0.6 * math.exp(-0.3 * layer)
        qkv = _qkv_proj(x2, attn_g, w_qkv, col_scale, layer)
        y_a = _diff_attention(qkv, slopes, lam_vecs, subln_col, layer, lam_init)
        y_b = _sb_attention(qkv, sb_col, layer)
        y_c = _band_attention(qkv, _band_bias_tiles(rel_bias[layer]), ch_col, layer)
        x2 = _out_proj(y_a, y_b, y_c, w_o, x2, layer)
        x2 = _ffn(x2, ffn_g, w_gate_b, w_up_b, conv_w, conv_b3, w_down_b, layer)
    out = _final_norm(x2, final_norm.reshape(1, D_MODEL))
    return out.reshape(1, SEQ, D_MODEL)
```

```python
import functools
import math

import jax
import jax.numpy as jnp
import numpy as np
from jax import lax
from jax.experimental import pallas as pl
from jax.experimental.pallas import tpu as pltpu

D_MODEL = 2048
SEQ = 8192
DEPTH = 4
CHUNK = 64
HEAD_DIM = 64
N_HEADS = 8
DIFF_WIDTH = 1024
SB_WIDTH = 512
CH_WIDTH = 512
QKV_WIDTH = 6144
LEFT_CHUNKS = 8
MAX_REL = 128
D_FF = 5632
CONV_WIDTH = 3
NORM_EPS = 1e-6
SUBLN_EPS = 1e-5
SCALE = HEAD_DIM ** -0.5
LOG2E = math.log2(math.e)

LANES = 128
ATT_T = 256
N_QBLK = SEQ // ATT_T
HALO = 16
VMEM_LIMIT = 48 * 1024 * 1024

QA, KA, VA = 0, 8, 16
QB, KB, VB = 24, 28, 32
QC, KC, VC = 36, 40, 44

_F32 = jnp.float32
_BF16 = jnp.bfloat16
_NEG = -1e30
_SIGN_BIT = np.int32(-2 ** 31)


def _cparams(n_axes):
    return pltpu.CompilerParams(
        dimension_semantics=("arbitrary",) * n_axes,
        vmem_limit_bytes=VMEM_LIMIT)


def _qkv_kernel(x_ref, g_ref, w_ref, cs_ref, o_ref, h_scr):
    @pl.when(pl.program_id(1) == 0)
    def _():
        x = x_ref[...]
        ms = jnp.mean(x * x, axis=-1, keepdims=True)
        h_scr[...] = (x * lax.rsqrt(ms + NORM_EPS) * g_ref[...]).astype(_BF16)

    acc = jnp.dot(h_scr[...], w_ref[...], preferred_element_type=_F32)
    o_ref[...] = (acc * cs_ref[...]).astype(_BF16)


def _qkv_col_scale():
    cs = np.ones((1, QKV_WIDTH), np.float32)
    for q_start, k_start in ((QA, KA), (QB, KB), (QC, KC)):
        cs[:, q_start * LANES:k_start * LANES] = SCALE * LOG2E
    return cs


def _qkv_proj(x, g_all, w_all, col_scale, layer, tm=1024, tn=1024):
    return pl.pallas_call(
        _qkv_kernel,
        grid=(SEQ // tm, QKV_WIDTH // tn),
        in_specs=[
            pl.BlockSpec((tm, D_MODEL), lambda i, j: (i, 0)),
            pl.BlockSpec((None, 1, D_MODEL), lambda i, j: (layer, 0, 0)),
            pl.BlockSpec((None, D_MODEL, tn), lambda i, j: (layer, 0, j)),
            pl.BlockSpec((1, tn), lambda i, j: (0, j)),
        ],
        out_specs=pl.BlockSpec((tm, tn), lambda i, j: (i, j)),
        out_shape=jax.ShapeDtypeStruct((SEQ, QKV_WIDTH), _BF16),
        scratch_shapes=[pltpu.VMEM((tm, D_MODEL), _BF16)],
        compiler_params=_cparams(2),
        name="qkv_proj",
    )(x, g_all, w_all, col_scale)


def _split_heads_t(q_ref):
    q_t = q_ref[...].T
    dim = lax.broadcasted_iota(jnp.int32, q_t.shape, 0)
    zero = jnp.zeros_like(q_t)
    return jnp.where(dim < HEAD_DIM, q_t, zero), jnp.where(dim >= HEAD_DIM, q_t, zero)


def _key_block(ref, j):
    return ref[pl.ds(pl.multiple_of(j * ATT_T, ATT_T), ATT_T), :]


def _pair_rmsnorm(o_t, gain_col):
    top, bot = o_t[:HEAD_DIM], o_t[HEAD_DIM:]
    top = top * lax.rsqrt(jnp.mean(top * top, axis=0, keepdims=True) + NORM_EPS)
    bot = bot * lax.rsqrt(jnp.mean(bot * bot, axis=0, keepdims=True) + NORM_EPS)
    return (jnp.concatenate([top, bot], axis=0) * gain_col).T


MIX_TQ = 512
MIX_TK = 256
MIX_DIAG = MIX_TQ // MIX_TK
assert MIX_DIAG == 2
N_KBLK = SEQ // MIX_TK
DIFF_PER_STEP = 2
DIFF_SPLIT = 3
DIFF_EXTRA = 16
SB_SUB = 128


def _transpose_blocks(v_ref, vt_ref, rows=None):
    def body(b, carry):
        blk = v_ref[pl.ds(pl.multiple_of(b * MIX_TK, MIX_TK), MIX_TK), :].T
        if rows is None:
            vt_ref[b] = blk
        else:
            vt_ref[b, :LANES] = blk
            vt_ref[b, LANES:] = rows
        return carry

    lax.fori_loop(0, N_KBLK, body, 0)


class _DiffStream:
    def __init__(self, i, slope_row, q_ref, k_ref, v_ref, diag, vt, s, mx, p, al, m, acc):
        self.i, self.k_ref, self.v_ref = i, k_ref, v_ref
        self.diag, self.vt, self.s, self.mx, self.p, self.al, self.m, self.acc = (
            diag, vt, s, mx, p, al, m, acc)
        self.slope2 = slope_row(0)
        dim = lax.broadcasted_iota(jnp.int32, (LANES, MIX_TQ), 0)
        slope_rows = functools.reduce(
            lambda a, n: jnp.where(dim == n, slope_row(1 + n), a), range(DIFF_SPLIT),
            jnp.zeros((LANES, MIX_TQ), _F32)).astype(_BF16)
        self.q_maps = tuple(jnp.concatenate([q_half, slope_rows], axis=0)
                            for q_half in _split_heads_t(q_ref))
        lane = lax.broadcasted_iota(jnp.int32, (MIX_TK, LANES), 1)
        key_offset = lax.broadcasted_iota(jnp.int32, (MIX_TK, LANES), 0).astype(_F32)
        self.key_feat = jnp.where(lane < DIFF_SPLIT, key_offset, 0.0).astype(_BF16)

    def init_head(self):
        row = lax.broadcasted_iota(jnp.int32, (MIX_TK, MIX_TQ), 0)
        col = lax.broadcasted_iota(jnp.int32, (MIX_TK, MIX_TQ), 1)
        for d in range(MIX_DIAG):
            key = row + d * MIX_TK
            allowed = (key // CHUNK) <= (col // CHUNK)
            lead = (col - row).astype(_F32) - jnp.abs(col - key).astype(_F32)
            self.diag[d] = jnp.where(allowed, self.slope2 * lead, -jnp.inf)
        rows = lax.broadcasted_iota(jnp.int32, (DIFF_EXTRA, MIX_TK), 0)
        _transpose_blocks(self.v_ref, self.vt, jnp.where(rows == 0, 1.0, 0.0).astype(_BF16))

    def init_tile(self):
        self.m[...] = jnp.full(self.m.shape, _NEG, _F32)
        self.acc[...] = jnp.zeros(self.acc.shape, _F32)

    def _offset(self, t, diagonal):
        if diagonal is not None:
            return jnp.zeros((), _F32)
        return -self.slope2 * (self.i * MIX_TQ - t * MIX_TK).astype(_F32)

    def stage_a(self, t, slot, diagonal=None):
        kb = self.k_ref[pl.ds(pl.multiple_of(t * MIX_TK, MIX_TK), MIX_TK), :]
        kx = jnp.concatenate([kb, self.key_feat], axis=1)
        offset = self._offset(t, diagonal)
        for c in range(2):
            s = jnp.dot(kx, self.q_maps[c], preferred_element_type=_F32)
            if diagonal is not None:
                s = s + self.diag[diagonal]
            self.s[slot, c] = s
            self.mx[slot, c] = jnp.max(s, axis=0, keepdims=True) + offset

    def stage_b(self, t, slot, diagonal=None):
        offset = self._offset(t, diagonal)
        for c in range(2):
            m_old = self.m[c]
            m_new = jnp.maximum(m_old, self.mx[slot, c])
            self.p[slot, c] = jnp.exp2(self.s[slot, c] - (m_new - offset)).astype(_BF16)
            self.al[slot, c] = jnp.exp2(m_old - m_new)
            self.m[c] = m_new

    def stage_c(self, t, slot):
        vt = self.vt[t]
        for c in range(2):
            pv = jnp.dot(vt, self.p[slot, c], preferred_element_type=_F32)
            self.acc[c] = self.al[slot, c] * self.acc[c] + pv

    def prologue(self, first_tile):
        self.stage_a(0, 0, diagonal=0 if first_tile else None)
        self.stage_b(0, 0, diagonal=0 if first_tile else None)
        self.stage_a(1, 1, diagonal=1 if first_tile else None)

    def two_steps(self, t, diagonal=False):
        self.stage_c(t - 2, 0)
        self.stage_b(t - 1, 1)
        self.stage_a(t, 0, diagonal=0 if diagonal else None)
        self.stage_c(t - 1, 1)
        self.stage_b(t, 0, diagonal=0 if diagonal else None)
        self.stage_a(t + 1, 1, diagonal=1 if diagonal else None)

    def drain(self, nb):
        self.stage_c(nb - 2, 0)
        self.stage_b(nb - 1, 1, diagonal=1)
        self.stage_c(nb - 1, 1)

    def result(self, lam, gain_col, lam_init):
        acc0, acc1 = self.acc[0], self.acc[1]
        o_t = (acc0[:LANES] / acc0[LANES:LANES + 1]
               - lam * (acc1[:LANES] / acc1[LANES:LANES + 1]))
        ms = jnp.mean(o_t * o_t, axis=0, keepdims=True)
        return (o_t * lax.rsqrt(ms + SUBLN_EPS) * gain_col * (1.0 - lam_init)).T


class _SbStream:
    def __init__(self, nb, q_ref, k_ref, v_ref, mask, tri, vt, z, hl, a, r, acc):
        self.nb, self.k_ref, self.v_ref = nb, k_ref, v_ref
        self.mask, self.tri, self.vt, self.z, self.hl, self.a, self.r, self.acc = (
            mask, tri, vt, z, hl, a, r, acc)
        self.q_heads = _split_heads_t(q_ref)

    def init_head(self):
        row = lax.broadcasted_iota(jnp.int32, (MIX_TK, MIX_TQ), 0)
        col = lax.broadcasted_iota(jnp.int32, (MIX_TK, MIX_TQ), 1)
        for d in range(MIX_DIAG):
            self.mask[d] = jnp.where(row + d * MIX_TK < col, 0.0, -jnp.inf)
        r2 = lax.broadcasted_iota(jnp.int32, (SB_SUB, 2 * SB_SUB), 0)
        c2 = lax.broadcasted_iota(jnp.int32, (SB_SUB, 2 * SB_SUB), 1) % SB_SUB
        self.tri[...] = jnp.where(c2 >= r2, 1.0, 0.0).astype(_BF16)
        _transpose_blocks(self.v_ref, self.vt)

    def init_tile(self):
        self.r[...] = jnp.zeros(self.r.shape, _F32)
        self.acc[...] = jnp.zeros(self.acc.shape, _F32)

    def stage_a(self, t, slot, diagonal=None):
        start = pl.multiple_of((self.nb - 1 - t) * MIX_TK, MIX_TK)
        kb = self.k_ref[pl.ds(start, MIX_TK), :]
        for hh in range(2):
            z = jnp.dot(kb, self.q_heads[hh], preferred_element_type=_F32)
            if diagonal is not None:
                z = z + self.mask[diagonal]
            neg_abs = pltpu.bitcast(pltpu.bitcast(z, jnp.int32) | _SIGN_BIT, _F32)
            nl = jnp.maximum(z, 0.0) + jnp.log2(1.0 + jnp.exp2(neg_abs))
            hi = nl.astype(_BF16)
            lo = (nl - hi.astype(_F32)).astype(_BF16)
            self.z[slot, hh] = z
            for sub in range(MIX_TK // SB_SUB):
                rows = slice(sub * SB_SUB, (sub + 1) * SB_SUB)
                self.hl[slot, hh, sub, :SB_SUB] = hi[rows]
                self.hl[slot, hh, sub, SB_SUB:] = lo[rows]

    def stage_b(self, slot):
        tri = self.tri[...]
        for hh in range(2):
            right = self.r[hh]
            for sub in reversed(range(MIX_TK // SB_SUB)):
                rows = slice(sub * SB_SUB, (sub + 1) * SB_SUB)
                incl = jnp.dot(tri, self.hl[slot, hh, sub], preferred_element_type=_F32)
                self.a[slot, hh, rows] = jnp.exp2(self.z[slot, hh, rows] - incl - right).astype(_BF16)
                right = right + incl[0:1]
            self.r[hh] = right

    def stage_c(self, t, slot):
        vt = self.vt[self.nb - 1 - t]
        for hh in range(2):
            self.acc[hh] += jnp.dot(vt[hh * HEAD_DIM:(hh + 1) * HEAD_DIM], self.a[slot, hh],
                                    preferred_element_type=_F32)

    def prologue(self):
        self.stage_a(0, 0, diagonal=1)
        self.stage_b(0)
        self.stage_a(1, 1, diagonal=0)

    def two_steps(self, t):
        self.stage_c(t - 2, 0)
        self.stage_b(1)
        self.stage_a(t, 0)
        self.stage_c(t - 1, 1)
        self.stage_b(0)
        self.stage_a(t + 1, 1)

    def drain(self):
        self.stage_c(self.nb - 2, 0)
        self.stage_b(1)
        self.stage_c(self.nb - 1, 1)

    def result(self, gain_col):
        o_t = jnp.concatenate([self.acc[0], self.acc[1]], axis=0)
        return _pair_rmsnorm(o_t, gain_col)


def _mix_kernel(slope_ref, lam_ref, subln_ref, sbgain_ref,
                dq0_ref, dq1_ref, dk0_ref, dk1_ref, dv0_ref, dv1_ref, sq_ref, sk_ref, sv_ref,
                ya_ref, yb_ref,
                diag_scr, dvt_scr, s_scr, mx_scr, p_scr, al_scr, m_scr, dacc_scr,
                mask_scr, tri_scr, svt_scr, z_scr, hl_scr, a_scr, r_scr, sacc_scr, *, lam_init):
    g = pl.program_id(0)
    i = pl.program_id(1)
    nb = (i + 1) * MIX_DIAG

    diffs = []
    for hd, (q_ref, k_ref, v_ref) in enumerate(
            ((dq0_ref, dk0_ref, dv0_ref), (dq1_ref, dk1_ref, dv1_ref))):
        head = g * DIFF_PER_STEP + hd
        diffs.append(_DiffStream(
            i, functools.partial(lambda n, head: slope_ref[head, n], head=head),
            q_ref, k_ref, v_ref, diag_scr.at[hd], dvt_scr.at[hd], s_scr.at[hd], mx_scr.at[hd],
            p_scr.at[hd], al_scr.at[hd], m_scr.at[hd], dacc_scr.at[hd]))
    sb = _SbStream(nb, sq_ref, sk_ref, sv_ref, mask_scr, tri_scr, svt_scr, z_scr, hl_scr, a_scr,
                   r_scr, sacc_scr)
    streams = diffs + [sb]

    @pl.when(i == 0)
    def _():
        for stream in streams:
            stream.init_head()

    for stream in streams:
        stream.init_tile()

    def drain():
        sb.drain()
        for d in diffs:
            d.drain(nb)

    @pl.when(i == 0)
    def _():
        sb.prologue()
        for d in diffs:
            d.prologue(first_tile=True)
        drain()

    @pl.when(i > 0)
    def _():
        sb.prologue()
        for d in diffs:
            d.prologue(first_tile=False)

        def two_steps(t):
            sb.two_steps(t)
            for d in diffs:
                d.two_steps(t)

        odd = (i - 1) % 2

        @pl.when(odd == 1)
        def _():
            two_steps(2)

        def body(u, carry):
            t = 2 + 2 * odd + 4 * u
            two_steps(t)
            two_steps(t + 2)
            return carry

        lax.fori_loop(0, (i - 1) // 2, body, 0)
        sb.two_steps(nb - 2)
        for d in diffs:
            d.two_steps(nb - 2, diagonal=True)
        drain()

    lam_v = lam_ref[...]
    lam = (jnp.exp(jnp.sum(lam_v[0:1] * lam_v[1:2], axis=-1, keepdims=True))
           - jnp.exp(jnp.sum(lam_v[2:3] * lam_v[3:4], axis=-1, keepdims=True))
           + lam_init)
    for hd, d in enumerate(diffs):
        ya_ref[:, hd * LANES:(hd + 1) * LANES] = d.result(
            lam, subln_ref[...], lam_init).astype(ya_ref.dtype)
    yb_ref[...] = sb.result(sbgain_ref[...]).astype(yb_ref.dtype)


def _alibi_slope_table():
    slope = 2.0 ** (-8.0 * np.arange(1, N_HEADS + 1, dtype=np.float32) / N_HEADS)
    slope2 = (slope * np.float32(LOG2E)).astype(np.float32)
    cols, rest = [slope2], slope2
    for _ in range(DIFF_SPLIT):
        term = rest.astype(_BF16).astype(np.float32)
        cols.append(term)
        rest = rest - term
    return jnp.asarray(np.stack(cols, axis=1))


def _mix_attention(qkv, slopes, lam_vecs, subln_col, sb_col, layer, lam_init):
    tile = (MIX_TK, MIX_TQ)
    nd = DIFF_PER_STEP

    def q_spec(col):
        return pl.BlockSpec((MIX_TQ, LANES), lambda g, i: (i, col(g)))

    def kv_spec(col):
        return pl.BlockSpec((SEQ, LANES), lambda g, i: (0, col(g)), pipeline_mode=pl.Buffered(1))

    return pl.pallas_call(
        functools.partial(_mix_kernel, lam_init=lam_init),
        grid=(N_HEADS // nd, SEQ // MIX_TQ),
        in_specs=[
            pl.BlockSpec(memory_space=pltpu.SMEM),
            pl.BlockSpec((None, 4, HEAD_DIM), lambda g, i: (layer, 0, 0)),
            pl.BlockSpec((None, LANES, 1), lambda g, i: (layer, 0, 0)),
            pl.BlockSpec((None, None, LANES, 1), lambda g, i: (layer, g, 0, 0)),
            q_spec(lambda g: QA + nd * g), q_spec(lambda g: QA + nd * g + 1),
            kv_spec(lambda g: KA + nd * g), kv_spec(lambda g: KA + nd * g + 1),
            kv_spec(lambda g: VA + nd * g), kv_spec(lambda g: VA + nd * g + 1),
            q_spec(lambda g: QB + g), kv_spec(lambda g: KB + g), kv_spec(lambda g: VB + g),
        ],
        out_specs=[
            pl.BlockSpec((MIX_TQ, nd * LANES), lambda g, i: (i, g)),
            pl.BlockSpec((MIX_TQ, LANES), lambda g, i: (i, g)),
        ],
        out_shape=[
            jax.ShapeDtypeStruct((SEQ, DIFF_WIDTH), _BF16),
            jax.ShapeDtypeStruct((SEQ, SB_WIDTH), _BF16),
        ],
        scratch_shapes=[
            pltpu.VMEM((nd, MIX_DIAG) + tile, _F32),
            pltpu.VMEM((nd, N_KBLK, LANES + DIFF_EXTRA, MIX_TK), _BF16),
            pltpu.VMEM((nd, 2, 2) + tile, _F32),
            pltpu.VMEM((nd, 2, 2, 1, MIX_TQ), _F32),
            pltpu.VMEM((nd, 2, 2) + tile, _BF16),
            pltpu.VMEM((nd, 2, 2, 1, MIX_TQ), _F32),
            pltpu.VMEM((nd, 2, 1, MIX_TQ), _F32),
            pltpu.VMEM((nd, 2, LANES + DIFF_EXTRA, MIX_TQ), _F32),
            pltpu.VMEM((MIX_DIAG,) + tile, _F32),
            pltpu.VMEM((SB_SUB, 2 * SB_SUB), _BF16),
            pltpu.VMEM((N_KBLK, LANES, MIX_TK), _BF16),
            pltpu.VMEM((2, 2) + tile, _F32),
            pltpu.VMEM((2, 2, MIX_TK // SB_SUB, 2 * SB_SUB, MIX_TQ), _BF16),
            pltpu.VMEM((2, 2) + tile, _BF16),
            pltpu.VMEM((2, 1, MIX_TQ), _F32),
            pltpu.VMEM((2, HEAD_DIM, MIX_TQ), _F32),
        ],
        compiler_params=_cparams(2),
        name="diff_sb_attention",
    )(slopes, lam_vecs, subln_col, sb_col, *([qkv] * 9))


BAND_TILES = 1 + (LEFT_CHUNKS * CHUNK) // ATT_T
BAND_TQ = 512
BAND_SUBS = BAND_TQ // ATT_T


def _band_kernel(bias_ref, gain_ref, q_ref, k_ref, v_ref, o_ref, vt_scr):
    i = pl.program_id(1)

    @pl.when(i == 0)
    def _():
        def transpose_block(b, carry):
            vt_scr[b] = _key_block(v_ref, b).T
            return carry

        lax.fori_loop(0, N_QBLK, transpose_block, 0)

    q_heads = _split_heads_t(q_ref)

    first = i * BAND_SUBS - (BAND_TILES - 1)
    tiles = [jnp.maximum(first + n, 0) for n in range(BAND_TILES + BAND_SUBS - 1)]
    k_tiles = [_key_block(k_ref, j) for j in tiles]
    vt_tiles = [vt_scr[j] for j in tiles]

    cols = []
    for sub in range(BAND_SUBS):
        heads = []
        for hh in range(2):
            q_sub = q_heads[hh][:, sub * ATT_T:(sub + 1) * ATT_T]
            scores = []
            for kk in range(BAND_TILES):
                s = jnp.dot(k_tiles[sub + kk], q_sub, preferred_element_type=_F32) + bias_ref[hh, kk]
                scores.append(jnp.where(first + sub + kk >= 0, s, -jnp.inf))
            m = functools.reduce(jnp.maximum,
                                 [jnp.max(s, axis=0, keepdims=True) for s in scores])
            probs = [jnp.exp2(s - m) for s in scores]
            denom = functools.reduce(jnp.add,
                                     [jnp.sum(p, axis=0, keepdims=True) for p in probs])
            o_t = functools.reduce(jnp.add, [
                jnp.dot(vt_tiles[sub + kk][hh * HEAD_DIM:(hh + 1) * HEAD_DIM],
                        probs[kk].astype(_BF16), preferred_element_type=_F32)
                for kk in range(BAND_TILES)])
            heads.append(o_t / denom)
        cols.append(jnp.concatenate(heads, axis=0))
    o_ref[...] = _pair_rmsnorm(jnp.concatenate(cols, axis=1), gain_ref[...]).astype(o_ref.dtype)


def _band_bias_tiles(rel_table):
    t = ATT_T
    key = np.arange(t)[:, None]
    qry = np.arange(t)[None, :]
    shift = np.arange(2 * t + 1) - t
    vec_idx, mask = [], []
    for kk in range(BAND_TILES):
        back = (BAND_TILES - 1 - kk) * t
        vec_idx.append(np.clip(back + shift, -(CHUNK - 1), MAX_REL) + (CHUNK - 1))
        dc = back // CHUNK + qry // CHUNK - key // CHUNK
        mask.append((dc >= 0) & (dc <= LEFT_CHUNKS))
    vec = rel_table[:, np.stack(vec_idx)] * LOG2E
    toe = jnp.tile(vec, (1, 1, t))[..., :t * 2 * t].reshape(N_HEADS, BAND_TILES, t, 2 * t)[..., t:]
    tiles = jnp.where(np.stack(mask), toe, -jnp.inf)
    return tiles.reshape(N_HEADS // 2, 2, BAND_TILES, t, t)


def _band_attention(qkv, bias_tiles, gain_col, layer):
    return pl.pallas_call(
        _band_kernel,
        grid=(N_HEADS // 2, SEQ // BAND_TQ),
        in_specs=[
            pl.BlockSpec((None, 2, BAND_TILES, ATT_T, ATT_T), lambda p, i: (p, 0, 0, 0, 0)),
            pl.BlockSpec((None, None, LANES, 1), lambda p, i: (layer, p, 0, 0)),
            pl.BlockSpec((BAND_TQ, LANES), lambda p, i: (i, QC + p)),
            pl.BlockSpec((SEQ, LANES), lambda p, i: (0, KC + p)),
            pl.BlockSpec((SEQ, LANES), lambda p, i: (0, VC + p)),
        ],
        out_specs=pl.BlockSpec((BAND_TQ, LANES), lambda p, i: (i, p)),
        out_shape=jax.ShapeDtypeStruct((SEQ, CH_WIDTH), _BF16),
        scratch_shapes=[pltpu.VMEM((N_QBLK, LANES, ATT_T), _BF16)],
        compiler_params=_cparams(2),
        name="band_attention",
    )(bias_tiles, gain_col, qkv, qkv, qkv)


def _oproj_kernel(ya_ref, yb_ref, yc_ref, w_ref, x_ref, o_ref, w_scr):
    @pl.when(pl.program_id(1) == 0)
    def _():
        w_scr[...] = w_ref[...].astype(_BF16)

    y = jnp.concatenate([ya_ref[...], yb_ref[...], yc_ref[...]], axis=1)
    o_ref[...] = x_ref[...] + jnp.dot(y, w_scr[...], preferred_element_type=_F32)


def _out_proj(y_a, y_b, y_c, w_all, x, layer, tm=1024, tn=512):
    return pl.pallas_call(
        _oproj_kernel,
        grid=(D_MODEL // tn, SEQ // tm),
        in_specs=[
            pl.BlockSpec((tm, DIFF_WIDTH), lambda j, i: (i, 0)),
            pl.BlockSpec((tm, SB_WIDTH), lambda j, i: (i, 0)),
            pl.BlockSpec((tm, CH_WIDTH), lambda j, i: (i, 0)),
            pl.BlockSpec((None, D_MODEL, tn), lambda j, i: (layer, 0, j)),
            pl.BlockSpec((tm, tn), lambda j, i: (i, j)),
        ],
        out_specs=pl.BlockSpec((tm, tn), lambda j, i: (i, j)),
        out_shape=jax.ShapeDtypeStruct((SEQ, D_MODEL), _F32),
        scratch_shapes=[pltpu.VMEM((D_MODEL, tn), _BF16)],
        compiler_params=_cparams(2),
        name="out_proj",
    )(y_a, y_b, y_c, w_all, x)


def _ffn_kernel(x_ref, halo_ref, g_ref, wg_ref, wu_ref, cw_ref, cb_ref, wd_ref, o_ref,
                h_scr, gate_scr, *, tm):
    i = pl.program_id(0)
    j = pl.program_id(1)

    def norm(x):
        ms = jnp.mean(x * x, axis=-1, keepdims=True)
        return (x * lax.rsqrt(ms + NORM_EPS) * g_ref[...]).astype(_BF16)

    @pl.when(j == 0)
    def _():
        x = x_ref[...]
        h_scr[HALO:, :] = norm(x)
        halo = jnp.where(i > 0, halo_ref[...], 0.0)
        h_scr[:HALO, :] = norm(halo)
        o_ref[...] = x

    gate_scr[...] = jnp.dot(h_scr[...], wg_ref[...], preferred_element_type=_F32)
    up = jnp.dot(h_scr[HALO:, :], wu_ref[...], preferred_element_type=_F32)
    cw = cw_ref[...]
    g = cb_ref[...]
    for t in range(CONV_WIDTH):
        lo = HALO - (CONV_WIDTH - 1) + t
        g = g + gate_scr[lo:lo + tm, :] * cw[t:t + 1]
    act = (g * jax.nn.sigmoid(g) * up).astype(_BF16)
    o_ref[...] += jnp.dot(act, wd_ref[...], preferred_element_type=_F32)


def _ffn(x, g_all, wg_all, wu_all, cw_all, cb_all, wd_all, layer, tm=512, tf=512):
    halo_blocks = tm // HALO
    return pl.pallas_call(
        functools.partial(_ffn_kernel, tm=tm),
        grid=(SEQ // tm, D_FF // tf),
        in_specs=[
            pl.BlockSpec((tm, D_MODEL), lambda i, j: (i, 0)),
            pl.BlockSpec((HALO, D_MODEL),
                         lambda i, j: (jnp.maximum(i * halo_blocks - 1, 0), 0)),
            pl.BlockSpec((None, 1, D_MODEL), lambda i, j: (layer, 0, 0)),
            pl.BlockSpec((None, D_MODEL, tf), lambda i, j: (layer, 0, j)),
            pl.BlockSpec((None, D_MODEL, tf), lambda i, j: (layer, 0, j)),
            pl.BlockSpec((None, CONV_WIDTH, tf), lambda i, j: (layer, 0, j)),
            pl.BlockSpec((None, 1, tf), lambda i, j: (layer, 0, j)),
            pl.BlockSpec((None, tf, D_MODEL), lambda i, j: (layer, j, 0)),
        ],
        out_specs=pl.BlockSpec((tm, D_MODEL), lambda i, j: (i, 0)),
        out_shape=jax.ShapeDtypeStruct((SEQ, D_MODEL), _F32),
        scratch_shapes=[
            pltpu.VMEM((tm + HALO, D_MODEL), _BF16),
            pltpu.VMEM((tm + HALO, tf), _F32),
        ],
        compiler_params=_cparams(2),
        name="gated_mlp",
    )(x, x, g_all, wg_all, wu_all, cw_all, cb_all, wd_all)


def _final_norm_kernel(x_ref, g_ref, o_ref):
    x = x_ref[...]
    ms = jnp.mean(x * x, axis=-1, keepdims=True)
    o_ref[...] = x * lax.rsqrt(ms + NORM_EPS) * g_ref[...]


def _final_norm(x, g, tm=512):
    return pl.pallas_call(
        _final_norm_kernel,
        grid=(SEQ // tm,),
        in_specs=[
            pl.BlockSpec((tm, D_MODEL), lambda i: (i, 0)),
            pl.BlockSpec((1, D_MODEL), lambda i: (0, 0)),
        ],
        out_specs=pl.BlockSpec((tm, D_MODEL), lambda i: (i, 0)),
        out_shape=jax.ShapeDtypeStruct((SEQ, D_MODEL), _F32),
        compiler_params=_cparams(1),
        name="final_norm",
    )(x, g)


def kernel(x, attn_norm, w_qkv, lambda_q1, lambda_k1, lambda_q2, lambda_k2, diff_subln,
           sb_norm, rel_bias, ch_norm, w_o, ffn_norm, w_gate, w_up, conv_w, conv_b, w_down,
           final_norm):
    batch, seq, d_model = x.shape
    assert (batch, seq, d_model) == (1, SEQ, D_MODEL)
    x2 = x.reshape(SEQ, D_MODEL)

    w_qkv_b = w_qkv.astype(_BF16)
    w_gate_b = w_gate.astype(_BF16)
    w_up_b = w_up.astype(_BF16)
    w_down_b = w_down.astype(_BF16)

    attn_g = attn_norm.reshape(DEPTH, 1, D_MODEL)
    ffn_g = ffn_norm.reshape(DEPTH, 1, D_MODEL)
    conv_b3 = conv_b.reshape(DEPTH, 1, D_FF)
    lam_vecs = jnp.stack([lambda_q1, lambda_k1, lambda_q2, lambda_k2], axis=1)
    subln_col = diff_subln.reshape(DEPTH, 2 * HEAD_DIM, 1)
    sb_col = sb_norm.reshape(DEPTH, N_HEADS // 2, 2 * HEAD_DIM, 1)
    ch_col = ch_norm.reshape(DEPTH, N_HEADS // 2, 2 * HEAD_DIM, 1)
    slopes = _alibi_slope_table()
    col_scale = jnp.asarray(_qkv_col_scale())

    for layer in range(DEPTH):
        lam_init = 0.8 - 0.6 * math.exp(-0.3 * layer)
        qkv = _qkv_proj(x2, attn_g, w_qkv_b, col_scale, layer)
        y_a, y_b = _mix_attention(qkv, slopes, lam_vecs, subln_col, sb_col, layer, lam_init)
        y_c = _band_attention(qkv, _band_bias_tiles(rel_bias[layer]), ch_col, layer)
        x2 = _out_proj(y_a, y_b, y_c, w_o, x2, layer)
        x2 = _ffn(x2, ffn_g, w_gate_b, w_up_b, conv_w, conv_b3, w_down_b, layer)
    out = _final_norm(x2, final_norm.reshape(1, D_MODEL))
    return out.reshape(1, SEQ, D_MODEL)
```

```python
import functools
import math

import jax
import jax.numpy as jnp
import numpy as np
from jax import lax
from jax.experimental import pallas as pl
from jax.experimental.pallas import tpu as pltpu

D_MODEL = 2048
SEQ = 8192
DEPTH = 4
CHUNK = 64
HEAD_DIM = 64
N_HEADS = 8
DIFF_WIDTH = 1024
SB_WIDTH = 512
CH_WIDTH = 512
QKV_WIDTH = 6144
LEFT_CHUNKS = 8
MAX_REL = 128
D_FF = 5632
CONV_WIDTH = 3
NORM_EPS = 1e-6
SUBLN_EPS = 1e-5
SCALE = HEAD_DIM ** -0.5
LOG2E = math.log2(math.e)

LANES = 128
ATT_T = 256
N_QBLK = SEQ // ATT_T
HALO = 16
VMEM_LIMIT = 48 * 1024 * 1024
MIX_VMEM_LIMIT = 56 * 1024 * 1024

QA, KA, VA = 0, 8, 16
QB, KB, VB = 24, 28, 32
QC, KC, VC = 36, 40, 44

_F32 = jnp.float32
_BF16 = jnp.bfloat16
_NEG = -1e30
_SIGN_BIT = np.int32(-2 ** 31)


def _cparams(n_axes):
    return pltpu.CompilerParams(
        dimension_semantics=("arbitrary",) * n_axes,
        vmem_limit_bytes=VMEM_LIMIT)


def _qkv_kernel(x_ref, g_ref, w_ref, cs_ref, o_ref, h_scr):
    @pl.when(pl.program_id(1) == 0)
    def _():
        x = x_ref[...]
        ms = jnp.mean(x * x, axis=-1, keepdims=True)
        h_scr[...] = (x * lax.rsqrt(ms + NORM_EPS) * g_ref[...]).astype(_BF16)

    acc = jnp.dot(h_scr[...], w_ref[...], preferred_element_type=_F32)
    o_ref[...] = (acc * cs_ref[...]).astype(_BF16)


def _qkv_col_scale():
    cs = np.ones((1, QKV_WIDTH), np.float32)
    for q_start, k_start in ((QA, KA), (QB, KB), (QC, KC)):
        cs[:, q_start * LANES:k_start * LANES] = SCALE * LOG2E
    return cs


def _qkv_proj(x, g_all, w_all, col_scale, layer, tm=1024, tn=1024):
    return pl.pallas_call(
        _qkv_kernel,
        grid=(SEQ // tm, QKV_WIDTH // tn),
        in_specs=[
            pl.BlockSpec((tm, D_MODEL), lambda i, j: (i, 0)),
            pl.BlockSpec((None, 1, D_MODEL), lambda i, j: (layer, 0, 0)),
            pl.BlockSpec((None, D_MODEL, tn), lambda i, j: (layer, 0, j)),
            pl.BlockSpec((1, tn), lambda i, j: (0, j)),
        ],
        out_specs=pl.BlockSpec((tm, tn), lambda i, j: (i, j)),
        out_shape=jax.ShapeDtypeStruct((SEQ, QKV_WIDTH), _BF16),
        scratch_shapes=[pltpu.VMEM((tm, D_MODEL), _BF16)],
        compiler_params=_cparams(2),
        name="qkv_proj",
    )(x, g_all, w_all, col_scale)


def _split_heads_t(q_ref):
    q_t = q_ref[...].T
    dim = lax.broadcasted_iota(jnp.int32, q_t.shape, 0)
    zero = jnp.zeros_like(q_t)
    return jnp.where(dim < HEAD_DIM, q_t, zero), jnp.where(dim >= HEAD_DIM, q_t, zero)


def _key_block(ref, j):
    return ref[pl.ds(pl.multiple_of(j * ATT_T, ATT_T), ATT_T), :]


def _pair_rmsnorm(o_t, gain_col):
    top, bot = o_t[:HEAD_DIM], o_t[HEAD_DIM:]
    top = top * lax.rsqrt(jnp.mean(top * top, axis=0, keepdims=True) + NORM_EPS)
    bot = bot * lax.rsqrt(jnp.mean(bot * bot, axis=0, keepdims=True) + NORM_EPS)
    return (jnp.concatenate([top, bot], axis=0) * gain_col).T


MIX_TQ = 512
MIX_TK = 256
MIX_DIAG = MIX_TQ // MIX_TK
assert MIX_DIAG == 2
N_KBLK = SEQ // MIX_TK
DIFF_PER_STEP = 2
DIFF_SPLIT = 3
DIFF_EXTRA = 16
SB_SUB = 128


def _transpose_blocks(v_ref, vt_ref, rows=None):
    def body(b, carry):
        blk = v_ref[pl.ds(pl.multiple_of(b * MIX_TK, MIX_TK), MIX_TK), :].T
        if rows is None:
            vt_ref[b] = blk
        else:
            vt_ref[b, :LANES] = blk
            vt_ref[b, LANES:] = rows
        return carry

    lax.fori_loop(0, N_KBLK, body, 0)


class _DiffStream:
    def __init__(self, i, slope_row, q_ref, k_ref, v_ref, diag, vt, s, mx, p, al, m, acc):
        self.i, self.k_ref, self.v_ref = i, k_ref, v_ref
        self.diag, self.vt, self.s, self.mx, self.p, self.al, self.m, self.acc = (
            diag, vt, s, mx, p, al, m, acc)
        self.slope2 = slope_row(0)
        dim = lax.broadcasted_iota(jnp.int32, (LANES, MIX_TQ), 0)
        slope_rows = functools.reduce(
            lambda a, n: jnp.where(dim == n, slope_row(1 + n), a), range(DIFF_SPLIT),
            jnp.zeros((LANES, MIX_TQ), _F32)).astype(_BF16)
        self.q_maps = tuple(jnp.concatenate([q_half, slope_rows], axis=0)
                            for q_half in _split_heads_t(q_ref))
        lane = lax.broadcasted_iota(jnp.int32, (MIX_TK, LANES), 1)
        key_offset = lax.broadcasted_iota(jnp.int32, (MIX_TK, LANES), 0).astype(_F32)
        self.key_feat = jnp.where(lane < DIFF_SPLIT, key_offset, 0.0).astype(_BF16)

    def init_head(self):
        row = lax.broadcasted_iota(jnp.int32, (MIX_TK, MIX_TQ), 0)
        col = lax.broadcasted_iota(jnp.int32, (MIX_TK, MIX_TQ), 1)
        for d in range(MIX_DIAG):
            key = row + d * MIX_TK
            allowed = (key // CHUNK) <= (col // CHUNK)
            lead = (col - row).astype(_F32) - jnp.abs(col - key).astype(_F32)
            self.diag[d] = jnp.where(allowed, self.slope2 * lead, -jnp.inf)
        rows = lax.broadcasted_iota(jnp.int32, (DIFF_EXTRA, MIX_TK), 0)
        _transpose_blocks(self.v_ref, self.vt, jnp.where(rows == 0, 1.0, 0.0).astype(_BF16))

    def init_tile(self):
        self.m[...] = jnp.full(self.m.shape, _NEG, _F32)
        self.acc[...] = jnp.zeros(self.acc.shape, _F32)

    def _offset(self, t, diagonal):
        if diagonal is not None:
            return jnp.zeros((), _F32)
        return -self.slope2 * (self.i * MIX_TQ - t * MIX_TK).astype(_F32)

    def stage_a(self, t, slot, diagonal=None):
        kb = self.k_ref[pl.ds(pl.multiple_of(t * MIX_TK, MIX_TK), MIX_TK), :]
        kx = jnp.concatenate([kb, self.key_feat], axis=1)
        offset = self._offset(t, diagonal)
        for c in range(2):
            s = jnp.dot(kx, self.q_maps[c], preferred_element_type=_F32)
            if diagonal is not None:
                s = s + self.diag[diagonal]
            self.s[slot, c] = s
            self.mx[slot, c] = jnp.max(s, axis=0, keepdims=True) + offset

    def stage_b(self, t, slot, diagonal=None):
        offset = self._offset(t, diagonal)
        for c in range(2):
            m_old = self.m[c]
            m_new = jnp.maximum(m_old, self.mx[slot, c])
            self.p[slot, c] = jnp.exp2(self.s[slot, c] - (m_new - offset)).astype(_BF16)
            self.al[slot, c] = jnp.exp2(m_old - m_new)
            self.m[c] = m_new

    def stage_c(self, t, slot):
        vt = self.vt[t]
        for c in range(2):
            pv = jnp.dot(vt, self.p[slot, c], preferred_element_type=_F32)
            self.acc[c] = self.al[slot, c] * self.acc[c] + pv

    def prologue(self, first_tile):
        self.stage_a(0, 0, diagonal=0 if first_tile else None)
        self.stage_b(0, 0, diagonal=0 if first_tile else None)
        self.stage_a(1, 1, diagonal=1 if first_tile else None)

    def two_steps(self, t, diagonal=False):
        self.stage_c(t - 2, 0)
        self.stage_b(t - 1, 1)
        self.stage_a(t, 0, diagonal=0 if diagonal else None)
        self.stage_c(t - 1, 1)
        self.stage_b(t, 0, diagonal=0 if diagonal else None)
        self.stage_a(t + 1, 1, diagonal=1 if diagonal else None)

    def drain(self, nb):
        self.stage_c(nb - 2, 0)
        self.stage_b(nb - 1, 1, diagonal=1)
        self.stage_c(nb - 1, 1)

    def result(self, lam, gain_col, lam_init):
        acc0, acc1 = self.acc[0], self.acc[1]
        o_t = (acc0[:LANES] / acc0[LANES:LANES + 1]
               - lam * (acc1[:LANES] / acc1[LANES:LANES + 1]))
        ms = jnp.mean(o_t * o_t, axis=0, keepdims=True)
        return (o_t * lax.rsqrt(ms + SUBLN_EPS) * gain_col * (1.0 - lam_init)).T


class _SbStream:
    def __init__(self, nb, q_ref, k_ref, v_ref, mask, tri, vt, z, hl, a, r, acc):
        self.nb, self.k_ref, self.v_ref = nb, k_ref, v_ref
        self.mask, self.tri, self.vt, self.z, self.hl, self.a, self.r, self.acc = (
            mask, tri, vt, z, hl, a, r, acc)
        self.q_heads = _split_heads_t(q_ref)

    def init_head(self):
        row = lax.broadcasted_iota(jnp.int32, (MIX_TK, MIX_TQ), 0)
        col = lax.broadcasted_iota(jnp.int32, (MIX_TK, MIX_TQ), 1)
        for d in range(MIX_DIAG):
            self.mask[d] = jnp.where(row + d * MIX_TK < col, 0.0, -jnp.inf)
        r2 = lax.broadcasted_iota(jnp.int32, (SB_SUB, 2 * SB_SUB), 0)
        c2 = lax.broadcasted_iota(jnp.int32, (SB_SUB, 2 * SB_SUB), 1) % SB_SUB
        self.tri[...] = jnp.where(c2 >= r2, 1.0, 0.0).astype(_BF16)
        _transpose_blocks(self.v_ref, self.vt)

    def init_tile(self):
        self.r[...] = jnp.zeros(self.r.shape, _F32)
        self.acc[...] = jnp.zeros(self.acc.shape, _F32)

    def stage_a(self, t, slot, diagonal=None):
        start = pl.multiple_of((self.nb - 1 - t) * MIX_TK, MIX_TK)
        kb = self.k_ref[pl.ds(start, MIX_TK), :]
        for hh in range(2):
            z = jnp.dot(kb, self.q_heads[hh], preferred_element_type=_F32)
            if diagonal is not None:
                z = z + self.mask[diagonal]
            neg_abs = pltpu.bitcast(pltpu.bitcast(z, jnp.int32) | _SIGN_BIT, _F32)
            nl = jnp.maximum(z, 0.0) + jnp.log2(1.0 + jnp.exp2(neg_abs))
            hi = nl.astype(_BF16)
            lo = (nl - hi.astype(_F32)).astype(_BF16)
            self.z[slot, hh] = z
            for sub in range(MIX_TK // SB_SUB):
                rows = slice(sub * SB_SUB, (sub + 1) * SB_SUB)
                self.hl[slot, hh, sub, :SB_SUB] = hi[rows]
                self.hl[slot, hh, sub, SB_SUB:] = lo[rows]

    def stage_b(self, slot):
        tri = self.tri[...]
        for hh in range(2):
            right = self.r[hh]
            for sub in reversed(range(MIX_TK // SB_SUB)):
                rows = slice(sub * SB_SUB, (sub + 1) * SB_SUB)
                incl = jnp.dot(tri, self.hl[slot, hh, sub], preferred_element_type=_F32)
                self.a[slot, hh, rows] = jnp.exp2(self.z[slot, hh, rows] - incl - right).astype(_BF16)
                right = right + incl[0:1]
            self.r[hh] = right

    def stage_c(self, t, slot):
        vt = self.vt[self.nb - 1 - t]
        for hh in range(2):
            self.acc[hh] += jnp.dot(vt[hh * HEAD_DIM:(hh + 1) * HEAD_DIM], self.a[slot, hh],
                                    preferred_element_type=_F32)

    def prologue(self):
        self.stage_a(0, 0, diagonal=1)
        self.stage_b(0)
        self.stage_a(1, 1, diagonal=0)

    def two_steps(self, t):
        self.stage_c(t - 2, 0)
        self.stage_b(1)
        self.stage_a(t, 0)
        self.stage_c(t - 1, 1)
        self.stage_b(0)
        self.stage_a(t + 1, 1)

    def drain(self):
        self.stage_c(self.nb - 2, 0)
        self.stage_b(1)
        self.stage_c(self.nb - 1, 1)

    def result(self, gain_col):
        o_t = jnp.concatenate([self.acc[0], self.acc[1]], axis=0)
        return _pair_rmsnorm(o_t, gain_col)


BAND_TILES = 1 + (LEFT_CHUNKS * CHUNK) // ATT_T
BAND_SUBS = MIX_TQ // ATT_T
assert ATT_T == MIX_TK


def _band_tile(i, bias_ref, q_ref, k_ref, vt_ref):
    q_heads = _split_heads_t(q_ref)
    first = i * BAND_SUBS - (BAND_TILES - 1)
    tiles = [jnp.maximum(first + n, 0) for n in range(BAND_TILES + BAND_SUBS - 1)]
    k_tiles = [_key_block(k_ref, j) for j in tiles]
    vt_tiles = [vt_ref[j] for j in tiles]

    cols = []
    for sub in range(BAND_SUBS):
        heads = []
        for hh in range(2):
            q_sub = q_heads[hh][:, sub * ATT_T:(sub + 1) * ATT_T]
            scores = []
            for kk in range(BAND_TILES):
                s = jnp.dot(k_tiles[sub + kk], q_sub, preferred_element_type=_F32) + bias_ref[hh, kk]
                scores.append(jnp.where(first + sub + kk >= 0, s, -jnp.inf))
            m = functools.reduce(jnp.maximum,
                                 [jnp.max(s, axis=0, keepdims=True) for s in scores])
            probs = [jnp.exp2(s - m) for s in scores]
            denom = functools.reduce(jnp.add,
                                     [jnp.sum(p, axis=0, keepdims=True) for p in probs])
            o_t = functools.reduce(jnp.add, [
                jnp.dot(vt_tiles[sub + kk][hh * HEAD_DIM:(hh + 1) * HEAD_DIM],
                        probs[kk].astype(_BF16), preferred_element_type=_F32)
                for kk in range(BAND_TILES)])
            heads.append(o_t / denom)
        cols.append(jnp.concatenate(heads, axis=0))
    return jnp.concatenate(cols, axis=1)


def _mix_kernel(slope_ref, lam_ref, subln_ref, sbgain_ref, chgain_ref, bias_ref,
                dq0_ref, dq1_ref, dk0_ref, dk1_ref, dv0_ref, dv1_ref, sq_ref, sk_ref, sv_ref,
                cq_ref, ck_ref, cv_ref,
                ya_ref, yb_ref, yc_ref,
                diag_scr, dvt_scr, s_scr, mx_scr, p_scr, al_scr, m_scr, dacc_scr,
                mask_scr, tri_scr, svt_scr, z_scr, hl_scr, a_scr, r_scr, sacc_scr, cvt_scr,
                *, lam_init):
    g = pl.program_id(0)
    i = pl.program_id(1)
    nb = (i + 1) * MIX_DIAG

    diffs = []
    for hd, (q_ref, k_ref, v_ref) in enumerate(
            ((dq0_ref, dk0_ref, dv0_ref), (dq1_ref, dk1_ref, dv1_ref))):
        head = g * DIFF_PER_STEP + hd
        diffs.append(_DiffStream(
            i, functools.partial(lambda n, head: slope_ref[head, n], head=head),
            q_ref, k_ref, v_ref, diag_scr.at[hd], dvt_scr.at[hd], s_scr.at[hd], mx_scr.at[hd],
            p_scr.at[hd], al_scr.at[hd], m_scr.at[hd], dacc_scr.at[hd]))
    sb = _SbStream(nb, sq_ref, sk_ref, sv_ref, mask_scr, tri_scr, svt_scr, z_scr, hl_scr, a_scr,
                   r_scr, sacc_scr)
    streams = diffs + [sb]

    @pl.when(i == 0)
    def _():
        for stream in streams:
            stream.init_head()
        _transpose_blocks(cv_ref, cvt_scr)

    for stream in streams:
        stream.init_tile()

    def band():
        o_t = _band_tile(i, bias_ref, cq_ref, ck_ref, cvt_scr)
        yc_ref[...] = _pair_rmsnorm(o_t, chgain_ref[...]).astype(yc_ref.dtype)

    def drain():
        sb.drain()
        for d in diffs:
            d.drain(nb)

    @pl.when(i == 0)
    def _():
        sb.prologue()
        for d in diffs:
            d.prologue(first_tile=True)
        band()
        drain()

    @pl.when(i > 0)
    def _():
        sb.prologue()
        for d in diffs:
            d.prologue(first_tile=False)
        band()

        def two_steps(t):
            sb.two_steps(t)
            for d in diffs:
                d.two_steps(t)

        odd = (i - 1) % 2

        @pl.when(odd == 1)
        def _():
            two_steps(2)

        def body(u, carry):
            t = 2 + 2 * odd + 4 * u
            two_steps(t)
            two_steps(t + 2)
            return carry

        lax.fori_loop(0, (i - 1) // 2, body, 0)
        sb.two_steps(nb - 2)
        for d in diffs:
            d.two_steps(nb - 2, diagonal=True)
        drain()

    lam_v = lam_ref[...]
    lam = (jnp.exp(jnp.sum(lam_v[0:1] * lam_v[1:2], axis=-1, keepdims=True))
           - jnp.exp(jnp.sum(lam_v[2:3] * lam_v[3:4], axis=-1, keepdims=True))
           + lam_init)
    for hd, d in enumerate(diffs):
        ya_ref[:, hd * LANES:(hd + 1) * LANES] = d.result(
            lam, subln_ref[...], lam_init).astype(ya_ref.dtype)
    yb_ref[...] = sb.result(sbgain_ref[...]).astype(yb_ref.dtype)


def _alibi_slope_table():
    slope = 2.0 ** (-8.0 * np.arange(1, N_HEADS + 1, dtype=np.float32) / N_HEADS)
    slope2 = (slope * np.float32(LOG2E)).astype(np.float32)
    cols, rest = [slope2], slope2
    for _ in range(DIFF_SPLIT):
        term = rest.astype(_BF16).astype(np.float32)
        cols.append(term)
        rest = rest - term
    return jnp.asarray(np.stack(cols, axis=1))


def _mix_attention(qkv, slopes, lam_vecs, subln_col, sb_col, ch_col, bias_tiles, layer, lam_init):
    tile = (MIX_TK, MIX_TQ)
    nd = DIFF_PER_STEP
    once = pl.Buffered(1)

    def q_spec(col):
        return pl.BlockSpec((MIX_TQ, LANES), lambda g, i: (i, col(g)))

    def kv_spec(col):
        return pl.BlockSpec((SEQ, LANES), lambda g, i: (0, col(g)), pipeline_mode=once)

    def pair_gain_spec():
        return pl.BlockSpec((None, None, LANES, 1), lambda g, i: (layer, g, 0, 0))

    return pl.pallas_call(
        functools.partial(_mix_kernel, lam_init=lam_init),
        grid=(N_HEADS // nd, SEQ // MIX_TQ),
        in_specs=[
            pl.BlockSpec(memory_space=pltpu.SMEM),
            pl.BlockSpec((None, 4, HEAD_DIM), lambda g, i: (layer, 0, 0)),
            pl.BlockSpec((None, LANES, 1), lambda g, i: (layer, 0, 0)),
            pair_gain_spec(), pair_gain_spec(),
            pl.BlockSpec((None, 2, BAND_TILES, ATT_T, ATT_T), lambda g, i: (g, 0, 0, 0, 0),
                         pipeline_mode=once),
            q_spec(lambda g: QA + nd * g), q_spec(lambda g: QA + nd * g + 1),
            kv_spec(lambda g: KA + nd * g), kv_spec(lambda g: KA + nd * g + 1),
            kv_spec(lambda g: VA + nd * g), kv_spec(lambda g: VA + nd * g + 1),
            q_spec(lambda g: QB + g), kv_spec(lambda g: KB + g), kv_spec(lambda g: VB + g),
            q_spec(lambda g: QC + g), kv_spec(lambda g: KC + g), kv_spec(lambda g: VC + g),
        ],
        out_specs=[
            pl.BlockSpec((MIX_TQ, nd * LANES), lambda g, i: (i, g)),
            pl.BlockSpec((MIX_TQ, LANES), lambda g, i: (i, g)),
            pl.BlockSpec((MIX_TQ, LANES), lambda g, i: (i, g)),
        ],
        out_shape=[
            jax.ShapeDtypeStruct((SEQ, DIFF_WIDTH), _BF16),
            jax.ShapeDtypeStruct((SEQ, SB_WIDTH), _BF16),
            jax.ShapeDtypeStruct((SEQ, CH_WIDTH), _BF16),
        ],
        scratch_shapes=[
            pltpu.VMEM((nd, MIX_DIAG) + tile, _F32),
            pltpu.VMEM((nd, N_KBLK, LANES + DIFF_EXTRA, MIX_TK), _BF16),
            pltpu.VMEM((nd, 2, 2) + tile, _F32),
            pltpu.VMEM((nd, 2, 2, 1, MIX_TQ), _F32),
            pltpu.VMEM((nd, 2, 2) + tile, _BF16),
            pltpu.VMEM((nd, 2, 2, 1, MIX_TQ), _F32),
            pltpu.VMEM((nd, 2, 1, MIX_TQ), _F32),
            pltpu.VMEM((nd, 2, LANES + DIFF_EXTRA, MIX_TQ), _F32),
            pltpu.VMEM((MIX_DIAG,) + tile, _F32),
            pltpu.VMEM((SB_SUB, 2 * SB_SUB), _BF16),
            pltpu.VMEM((N_KBLK, LANES, MIX_TK), _BF16),
            pltpu.VMEM((2, 2) + tile, _F32),
            pltpu.VMEM((2, 2, MIX_TK // SB_SUB, 2 * SB_SUB, MIX_TQ), _BF16),
            pltpu.VMEM((2, 2) + tile, _BF16),
            pltpu.VMEM((2, 1, MIX_TQ), _F32),
            pltpu.VMEM((2, HEAD_DIM, MIX_TQ), _F32),
            pltpu.VMEM((N_KBLK, LANES, MIX_TK), _BF16),
        ],
        compiler_params=pltpu.CompilerParams(
            dimension_semantics=("arbitrary", "arbitrary"), vmem_limit_bytes=MIX_VMEM_LIMIT),
        name="token_mixers",
    )(slopes, lam_vecs, subln_col, sb_col, ch_col, bias_tiles, *([qkv] * 12))


def _band_bias_tiles(rel_table):
    t = ATT_T
    key = np.arange(t)[:, None]
    qry = np.arange(t)[None, :]
    shift = np.arange(2 * t + 1) - t
    vec_idx, mask = [], []
    for kk in range(BAND_TILES):
        back = (BAND_TILES - 1 - kk) * t
        vec_idx.append(np.clip(back + shift, -(CHUNK - 1), MAX_REL) + (CHUNK - 1))
        dc = back // CHUNK + qry // CHUNK - key // CHUNK
        mask.append((dc >= 0) & (dc <= LEFT_CHUNKS))
    vec = rel_table[:, np.stack(vec_idx)] * LOG2E
    toe = jnp.tile(vec, (1, 1, t))[..., :t * 2 * t].reshape(N_HEADS, BAND_TILES, t, 2 * t)[..., t:]
    tiles = jnp.where(np.stack(mask), toe, -jnp.inf)
    return tiles.reshape(N_HEADS // 2, 2, BAND_TILES, t, t)


def _oproj_kernel(ya_ref, yb_ref, yc_ref, w_ref, x_ref, o_ref, w_scr):
    @pl.when(pl.program_id(1) == 0)
    def _():
        w_scr[...] = w_ref[...].astype(_BF16)

    y = jnp.concatenate([ya_ref[...], yb_ref[...], yc_ref[...]], axis=1)
    o_ref[...] = x_ref[...] + jnp.dot(y, w_scr[...], preferred_element_type=_F32)


def _out_proj(y_a, y_b, y_c, w_all, x, layer, tm=1024, tn=512):
    return pl.pallas_call(
        _oproj_kernel,
        grid=(D_MODEL // tn, SEQ // tm),
        in_specs=[
            pl.BlockSpec((tm, DIFF_WIDTH), lambda j, i: (i, 0)),
            pl.BlockSpec((tm, SB_WIDTH), lambda j, i: (i, 0)),
            pl.BlockSpec((tm, CH_WIDTH), lambda j, i: (i, 0)),
            pl.BlockSpec((None, D_MODEL, tn), lambda j, i: (layer, 0, j)),
            pl.BlockSpec((tm, tn), lambda j, i: (i, j)),
        ],
        out_specs=pl.BlockSpec((tm, tn), lambda j, i: (i, j)),
        out_shape=jax.ShapeDtypeStruct((SEQ, D_MODEL), _F32),
        scratch_shapes=[pltpu.VMEM((D_MODEL, tn), _BF16)],
        compiler_params=_cparams(2),
        name="out_proj",
    )(y_a, y_b, y_c, w_all, x)


def _ffn_kernel(x_ref, halo_ref, g_ref, wg_ref, wu_ref, cw_ref, cb_ref, wd_ref, o_ref,
                h_scr, gate_scr, *, tm):
    i = pl.program_id(0)
    j = pl.program_id(1)

    def norm(x):
        ms = jnp.mean(x * x, axis=-1, keepdims=True)
        return (x * lax.rsqrt(ms + NORM_EPS) * g_ref[...]).astype(_BF16)

    @pl.when(j == 0)
    def _():
        x = x_ref[...]
        h_scr[HALO:, :] = norm(x)
        halo = jnp.where(i > 0, halo_ref[...], 0.0)
        h_scr[:HALO, :] = norm(halo)
        o_ref[...] = x

    gate_scr[...] = jnp.dot(h_scr[...], wg_ref[...], preferred_element_type=_F32)
    up = jnp.dot(h_scr[HALO:, :], wu_ref[...], preferred_element_type=_F32)
    cw = cw_ref[...]
    g = cb_ref[...]
    for t in range(CONV_WIDTH):
        lo = HALO - (CONV_WIDTH - 1) + t
        g = g + gate_scr[lo:lo + tm, :] * cw[t:t + 1]
    act = (g * jax.nn.sigmoid(g) * up).astype(_BF16)
    o_ref[...] += jnp.dot(act, wd_ref[...], preferred_element_type=_F32)


def _ffn(x, g_all, wg_all, wu_all, cw_all, cb_all, wd_all, layer, tm=512, tf=512):
    halo_blocks = tm // HALO
    return pl.pallas_call(
        functools.partial(_ffn_kernel, tm=tm),
        grid=(SEQ // tm, D_FF // tf),
        in_specs=[
            pl.BlockSpec((tm, D_MODEL), lambda i, j: (i, 0)),
            pl.BlockSpec((HALO, D_MODEL),
                         lambda i, j: (jnp.maximum(i * halo_blocks - 1, 0), 0)),
            pl.BlockSpec((None, 1, D_MODEL), lambda i, j: (layer, 0, 0)),
            pl.BlockSpec((None, D_MODEL, tf), lambda i, j: (layer, 0, j)),
            pl.BlockSpec((None, D_MODEL, tf), lambda i, j: (layer, 0, j)),
            pl.BlockSpec((None, CONV_WIDTH, tf), lambda i, j: (layer, 0, j)),
            pl.BlockSpec((None, 1, tf), lambda i, j: (layer, 0, j)),
            pl.BlockSpec((None, tf, D_MODEL), lambda i, j: (layer, j, 0)),
        ],
        out_specs=pl.BlockSpec((tm, D_MODEL), lambda i, j: (i, 0)),
        out_shape=jax.ShapeDtypeStruct((SEQ, D_MODEL), _F32),
        scratch_shapes=[
            pltpu.VMEM((tm + HALO, D_MODEL), _BF16),
            pltpu.VMEM((tm + HALO, tf), _F32),
        ],
        compiler_params=_cparams(2),
        name="gated_mlp",
    )(x, x, g_all, wg_all, wu_all, cw_all, cb_all, wd_all)


def _final_norm_kernel(x_ref, g_ref, o_ref):
    x = x_ref[...]
    ms = jnp.mean(x * x, axis=-1, keepdims=True)
    o_ref[...] = x * lax.rsqrt(ms + NORM_EPS) * g_ref[...]


def _final_norm(x, g, tm=512):
    return pl.pallas_call(
        _final_norm_kernel,
        grid=(SEQ // tm,),
        in_specs=[
            pl.BlockSpec((tm, D_MODEL), lambda i: (i, 0)),
            pl.BlockSpec((1, D_MODEL), lambda i: (0, 0)),
        ],
        out_specs=pl.BlockSpec((tm, D_MODEL), lambda i: (i, 0)),
        out_shape=jax.ShapeDtypeStruct((SEQ, D_MODEL), _F32),
        compiler_params=_cparams(1),
        name="final_norm",
    )(x, g)


def kernel(x, attn_norm, w_qkv, lambda_q1, lambda_k1, lambda_q2, lambda_k2, diff_subln,
           sb_norm, rel_bias, ch_norm, w_o, ffn_norm, w_gate, w_up, conv_w, conv_b, w_down,
           final_norm):
    batch, seq, d_model = x.shape
    assert (batch, seq, d_model) == (1, SEQ, D_MODEL)
    x2 = x.reshape(SEQ, D_MODEL)

    w_qkv_b = w_qkv.astype(_BF16)
    w_gate_b = w_gate.astype(_BF16)
    w_up_b = w_up.astype(_BF16)
    w_down_b = w_down.astype(_BF16)

    attn_g = attn_norm.reshape(DEPTH, 1, D_MODEL)
    ffn_g = ffn_norm.reshape(DEPTH, 1, D_MODEL)
    conv_b3 = conv_b.reshape(DEPTH, 1, D_FF)
    lam_vecs = jnp.stack([lambda_q1, lambda_k1, lambda_q2, lambda_k2], axis=1)
    subln_col = diff_subln.reshape(DEPTH, 2 * HEAD_DIM, 1)
    sb_col = sb_norm.reshape(DEPTH, N_HEADS // 2, 2 * HEAD_DIM, 1)
    ch_col = ch_norm.reshape(DEPTH, N_HEADS // 2, 2 * HEAD_DIM, 1)
    slopes = _alibi_slope_table()
    col_scale = jnp.asarray(_qkv_col_scale())

    for layer in range(DEPTH):
        lam_init = 0.8 - 0.6 * math.exp(-0.3 * layer)
        qkv = _qkv_proj(x2, attn_g, w_qkv_b, col_scale, layer)
        y_a, y_b, y_c = _mix_attention(qkv, slopes, lam_vecs, subln_col, sb_col, ch_col,
                                       _band_bias_tiles(rel_bias[layer]), layer, lam_init)
        x2 = _out_proj(y_a, y_b, y_c, w_o, x2, layer)
        x2 = _ffn(x2, ffn_g, w_gate_b, w_up_b, conv_w, conv_b3, w_down_b, layer)
    out = _final_norm(x2, final_norm.reshape(1, D_MODEL))
    return out.reshape(1, SEQ, D_MODEL)
```

```python
import functools
import math

import jax
import jax.numpy as jnp
import numpy as np
from jax import lax
from jax.experimental import pallas as pl
from jax.experimental.pallas import tpu as pltpu

D_MODEL = 2048
SEQ = 8192
DEPTH = 4
CHUNK = 64
HEAD_DIM = 64
N_HEADS = 8
DIFF_WIDTH = 1024
SB_WIDTH = 512
CH_WIDTH = 512
QKV_WIDTH = 6144
LEFT_CHUNKS = 8
MAX_REL = 128
D_FF = 5632
CONV_WIDTH = 3
NORM_EPS = 1e-6
SUBLN_EPS = 1e-5
SCALE = HEAD_DIM ** -0.5
LOG2E = math.log2(math.e)

LANES = 128
ATT_T = 256
N_QBLK = SEQ // ATT_T
HALO = 16
VMEM_LIMIT = 48 * 1024 * 1024
MIX_VMEM_LIMIT = 56 * 1024 * 1024

QA, KA, VA = 0, 8, 16
QB, KB, VB = 24, 28, 32
QC, KC, VC = 36, 40, 44

_F32 = jnp.float32
_BF16 = jnp.bfloat16
_NEG = -1e30
_SIGN_BIT = np.int32(-2 ** 31)


def _cparams(n_axes):
    return pltpu.CompilerParams(
        dimension_semantics=("arbitrary",) * n_axes,
        vmem_limit_bytes=VMEM_LIMIT)


def _qkv_kernel(x_ref, g_ref, w_ref, cs_ref, o_ref, h_scr):
    @pl.when(pl.program_id(1) == 0)
    def _():
        x = x_ref[...]
        ms = jnp.mean(x * x, axis=-1, keepdims=True)
        h_scr[...] = (x * lax.rsqrt(ms + NORM_EPS) * g_ref[...]).astype(_BF16)

    acc = jnp.dot(h_scr[...], w_ref[...], preferred_element_type=_F32)
    o_ref[...] = (acc * cs_ref[...]).astype(_BF16)


def _qkv_col_scale():
    cs = np.ones((1, QKV_WIDTH), np.float32)
    for q_start, k_start in ((QA, KA), (QB, KB), (QC, KC)):
        cs[:, q_start * LANES:k_start * LANES] = SCALE * LOG2E
    return cs


def _qkv_proj(x, g_all, w_all, col_scale, layer, tm=1024, tn=1024):
    return pl.pallas_call(
        _qkv_kernel,
        grid=(SEQ // tm, QKV_WIDTH // tn),
        in_specs=[
            pl.BlockSpec((tm, D_MODEL), lambda i, j: (i, 0)),
            pl.BlockSpec((None, 1, D_MODEL), lambda i, j: (layer, 0, 0)),
            pl.BlockSpec((None, D_MODEL, tn), lambda i, j: (layer, 0, j)),
            pl.BlockSpec((1, tn), lambda i, j: (0, j)),
        ],
        out_specs=pl.BlockSpec((tm, tn), lambda i, j: (i, j)),
        out_shape=jax.ShapeDtypeStruct((SEQ, QKV_WIDTH), _BF16),
        scratch_shapes=[pltpu.VMEM((tm, D_MODEL), _BF16)],
        compiler_params=_cparams(2),
        name="qkv_proj",
    )(x, g_all, w_all, col_scale)


def _split_heads_t(q_ref):
    q_t = q_ref[...].T
    dim = lax.broadcasted_iota(jnp.int32, q_t.shape, 0)
    zero = jnp.zeros_like(q_t)
    return jnp.where(dim < HEAD_DIM, q_t, zero), jnp.where(dim >= HEAD_DIM, q_t, zero)


def _key_block(ref, j):
    return ref[pl.ds(pl.multiple_of(j * ATT_T, ATT_T), ATT_T), :]


def _pair_rmsnorm(o_t, gain_col):
    top, bot = o_t[:HEAD_DIM], o_t[HEAD_DIM:]
    top = top * lax.rsqrt(jnp.mean(top * top, axis=0, keepdims=True) + NORM_EPS)
    bot = bot * lax.rsqrt(jnp.mean(bot * bot, axis=0, keepdims=True) + NORM_EPS)
    return (jnp.concatenate([top, bot], axis=0) * gain_col).T


MIX_TQ = 512
MIX_TK = 256
MIX_DIAG = MIX_TQ // MIX_TK
assert MIX_DIAG == 2
N_KBLK = SEQ // MIX_TK
DIFF_PER_STEP = 2
DIFF_SPLIT = 3
DIFF_EXTRA = 16
SB_SUB = 128


def _transpose_blocks(v_ref, vt_ref, rows=None):
    def body(b, carry):
        blk = v_ref[pl.ds(pl.multiple_of(b * MIX_TK, MIX_TK), MIX_TK), :].T
        if rows is None:
            vt_ref[b] = blk
        else:
            vt_ref[b, :LANES] = blk
            vt_ref[b, LANES:] = rows
        return carry

    lax.fori_loop(0, N_KBLK, body, 0)


class _DiffStream:
    def __init__(self, i, slope_row, q_ref, k_ref, v_ref, diag, vt, s, mx, p, al, m, acc):
        self.i, self.k_ref, self.v_ref = i, k_ref, v_ref
        self.diag, self.vt, self.s, self.mx, self.p, self.al, self.m, self.acc = (
            diag, vt, s, mx, p, al, m, acc)
        self.slope2 = slope_row(0)
        dim = lax.broadcasted_iota(jnp.int32, (LANES, MIX_TQ), 0)
        slope_rows = functools.reduce(
            lambda a, n: jnp.where(dim == n, slope_row(1 + n), a), range(DIFF_SPLIT),
            jnp.zeros((LANES, MIX_TQ), _F32)).astype(_BF16)
        self.q_maps = tuple(jnp.concatenate([q_half, slope_rows], axis=0)
                            for q_half in _split_heads_t(q_ref))
        lane = lax.broadcasted_iota(jnp.int32, (MIX_TK, LANES), 1)
        key_offset = lax.broadcasted_iota(jnp.int32, (MIX_TK, LANES), 0).astype(_F32)
        self.key_feat = jnp.where(lane < DIFF_SPLIT, key_offset, 0.0).astype(_BF16)

    def init_head(self):
        row = lax.broadcasted_iota(jnp.int32, (MIX_TK, MIX_TQ), 0)
        col = lax.broadcasted_iota(jnp.int32, (MIX_TK, MIX_TQ), 1)
        for d in range(MIX_DIAG):
            key = row + d * MIX_TK
            allowed = (key // CHUNK) <= (col // CHUNK)
            lead = (col - row).astype(_F32) - jnp.abs(col - key).astype(_F32)
            self.diag[d] = jnp.where(allowed, self.slope2 * lead, -jnp.inf)
        rows = lax.broadcasted_iota(jnp.int32, (DIFF_EXTRA, MIX_TK), 0)
        _transpose_blocks(self.v_ref, self.vt, jnp.where(rows == 0, 1.0, 0.0).astype(_BF16))

    def init_tile(self):
        self.m[...] = jnp.full(self.m.shape, _NEG, _F32)
        self.acc[...] = jnp.zeros(self.acc.shape, _F32)

    def _offset(self, t, diagonal):
        if diagonal is not None:
            return jnp.zeros((), _F32)
        return -self.slope2 * (self.i * MIX_TQ - t * MIX_TK).astype(_F32)

    def stage_a(self, t, slot, diagonal=None):
        kb = self.k_ref[pl.ds(pl.multiple_of(t * MIX_TK, MIX_TK), MIX_TK), :]
        kx = jnp.concatenate([kb, self.key_feat], axis=1)
        offset = self._offset(t, diagonal)
        for c in range(2):
            s = jnp.dot(kx, self.q_maps[c], preferred_element_type=_F32)
            if diagonal is not None:
                s = s + self.diag[diagonal]
            self.s[slot, c] = s
            self.mx[slot, c] = jnp.max(s, axis=0, keepdims=True) + offset

    def stage_b(self, t, slot, diagonal=None):
        offset = self._offset(t, diagonal)
        for c in range(2):
            m_old = self.m[c]
            m_new = jnp.maximum(m_old, self.mx[slot, c])
            self.p[slot, c] = jnp.exp2(self.s[slot, c] - (m_new - offset)).astype(_BF16)
            self.al[slot, c] = jnp.exp2(m_old - m_new)
            self.m[c] = m_new

    def stage_c(self, t, slot):
        vt = self.vt[t]
        for c in range(2):
            pv = jnp.dot(vt, self.p[slot, c], preferred_element_type=_F32)
            self.acc[c] = self.al[slot, c] * self.acc[c] + pv

    def prologue(self, first_tile):
        self.stage_a(0, 0, diagonal=0 if first_tile else None)
        self.stage_b(0, 0, diagonal=0 if first_tile else None)
        self.stage_a(1, 1, diagonal=1 if first_tile else None)

    def two_steps(self, t, diagonal=False):
        self.stage_c(t - 2, 0)
        self.stage_b(t - 1, 1)
        self.stage_a(t, 0, diagonal=0 if diagonal else None)
        self.stage_c(t - 1, 1)
        self.stage_b(t, 0, diagonal=0 if diagonal else None)
        self.stage_a(t + 1, 1, diagonal=1 if diagonal else None)

    def drain(self, nb):
        self.stage_c(nb - 2, 0)
        self.stage_b(nb - 1, 1, diagonal=1)
        self.stage_c(nb - 1, 1)

    def result(self, lam, gain_col, lam_init):
        acc0, acc1 = self.acc[0], self.acc[1]
        o_t = (acc0[:LANES] / acc0[LANES:LANES + 1]
               - lam * (acc1[:LANES] / acc1[LANES:LANES + 1]))
        ms = jnp.mean(o_t * o_t, axis=0, keepdims=True)
        return (o_t * lax.rsqrt(ms + SUBLN_EPS) * gain_col * (1.0 - lam_init)).T


class _SbStream:
    def __init__(self, nb, q_ref, k_ref, v_ref, mask, tri, vt, z, hl, a, r, acc):
        self.nb, self.k_ref, self.v_ref = nb, k_ref, v_ref
        self.mask, self.tri, self.vt, self.z, self.hl, self.a, self.r, self.acc = (
            mask, tri, vt, z, hl, a, r, acc)
        self.q_heads = _split_heads_t(q_ref)

    def init_head(self):
        row = lax.broadcasted_iota(jnp.int32, (MIX_TK, MIX_TQ), 0)
        col = lax.broadcasted_iota(jnp.int32, (MIX_TK, MIX_TQ), 1)
        for d in range(MIX_DIAG):
            self.mask[d] = jnp.where(row + d * MIX_TK < col, 0.0, -jnp.inf)
        r2 = lax.broadcasted_iota(jnp.int32, (SB_SUB, 2 * SB_SUB), 0)
        c2 = lax.broadcasted_iota(jnp.int32, (SB_SUB, 2 * SB_SUB), 1) % SB_SUB
        self.tri[...] = jnp.where(c2 >= r2, 1.0, 0.0).astype(_BF16)
        _transpose_blocks(self.v_ref, self.vt)

    def init_tile(self):
        self.r[...] = jnp.zeros(self.r.shape, _F32)
        self.acc[...] = jnp.zeros(self.acc.shape, _F32)

    def stage_a(self, t, slot, diagonal=None):
        start = pl.multiple_of((self.nb - 1 - t) * MIX_TK, MIX_TK)
        kb = self.k_ref[pl.ds(start, MIX_TK), :]
        for hh in range(2):
            z = jnp.dot(kb, self.q_heads[hh], preferred_element_type=_F32)
            if diagonal is not None:
                z = z + self.mask[diagonal]
            neg_abs = pltpu.bitcast(pltpu.bitcast(z, jnp.int32) | _SIGN_BIT, _F32)
            nl = jnp.maximum(z, 0.0) + jnp.log2(1.0 + jnp.exp2(neg_abs))
            hi = nl.astype(_BF16)
            lo = (nl - hi.astype(_F32)).astype(_BF16)
            self.z[slot, hh] = z
            for sub in range(MIX_TK // SB_SUB):
                rows = slice(sub * SB_SUB, (sub + 1) * SB_SUB)
                self.hl[slot, hh, sub, :SB_SUB] = hi[rows]
                self.hl[slot, hh, sub, SB_SUB:] = lo[rows]

    def stage_b(self, slot):
        tri = self.tri[...]
        for hh in range(2):
            right = self.r[hh]
            for sub in reversed(range(MIX_TK // SB_SUB)):
                rows = slice(sub * SB_SUB, (sub + 1) * SB_SUB)
                incl = jnp.dot(tri, self.hl[slot, hh, sub], preferred_element_type=_F32)
                self.a[slot, hh, rows] = jnp.exp2(self.z[slot, hh, rows] - incl - right).astype(_BF16)
                right = right + incl[0:1]
            self.r[hh] = right

    def stage_c(self, t, slot):
        vt = self.vt[self.nb - 1 - t]
        for hh in range(2):
            self.acc[hh] += jnp.dot(vt[hh * HEAD_DIM:(hh + 1) * HEAD_DIM], self.a[slot, hh],
                                    preferred_element_type=_F32)

    def prologue(self):
        self.stage_a(0, 0, diagonal=1)
        self.stage_b(0)
        self.stage_a(1, 1, diagonal=0)

    def two_steps(self, t):
        self.stage_c(t - 2, 0)
        self.stage_b(1)
        self.stage_a(t, 0)
        self.stage_c(t - 1, 1)
        self.stage_b(0)
        self.stage_a(t + 1, 1)

    def drain(self):
        self.stage_c(self.nb - 2, 0)
        self.stage_b(1)
        self.stage_c(self.nb - 1, 1)

    def result(self, gain_col):
        o_t = jnp.concatenate([self.acc[0], self.acc[1]], axis=0)
        return _pair_rmsnorm(o_t, gain_col)


BAND_TILES = 1 + (LEFT_CHUNKS * CHUNK) // ATT_T
BAND_SUBS = MIX_TQ // ATT_T
assert ATT_T == MIX_TK


def _band_tile(i, bias_ref, q_ref, k_ref, vt_ref):
    q_heads = _split_heads_t(q_ref)
    first = i * BAND_SUBS - (BAND_TILES - 1)
    tiles = [jnp.maximum(first + n, 0) for n in range(BAND_TILES + BAND_SUBS - 1)]
    k_tiles = [_key_block(k_ref, j) for j in tiles]
    vt_tiles = [vt_ref[j] for j in tiles]

    cols = []
    for sub in range(BAND_SUBS):
        heads = []
        for hh in range(2):
            q_sub = q_heads[hh][:, sub * ATT_T:(sub + 1) * ATT_T]
            scores = []
            for kk in range(BAND_TILES):
                s = jnp.dot(k_tiles[sub + kk], q_sub, preferred_element_type=_F32) + bias_ref[hh, kk]
                scores.append(jnp.where(first + sub + kk >= 0, s, -jnp.inf))
            m = functools.reduce(jnp.maximum,
                                 [jnp.max(s, axis=0, keepdims=True) for s in scores])
            probs = [jnp.exp2(s - m) for s in scores]
            denom = functools.reduce(jnp.add,
                                     [jnp.sum(p, axis=0, keepdims=True) for p in probs])
            o_t = functools.reduce(jnp.add, [
                jnp.dot(vt_tiles[sub + kk][hh * HEAD_DIM:(hh + 1) * HEAD_DIM],
                        probs[kk].astype(_BF16), preferred_element_type=_F32)
                for kk in range(BAND_TILES)])
            heads.append(o_t / denom)
        cols.append(jnp.concatenate(heads, axis=0))
    return jnp.concatenate(cols, axis=1)


def _mix_kernel(slope_ref, lam_ref, subln_ref, sbgain_ref, chgain_ref, biasvec_ref,
                dq0_ref, dq1_ref, dk0_ref, dk1_ref, dv0_ref, dv1_ref, sq_ref, sk_ref, sv_ref,
                cq_ref, ck_ref, cv_ref,
                ya_ref, yb_ref, yc_ref,
                diag_scr, dvt_scr, s_scr, mx_scr, p_scr, al_scr, m_scr, dacc_scr,
                mask_scr, tri_scr, svt_scr, z_scr, hl_scr, a_scr, r_scr, sacc_scr, cvt_scr, cbias_scr,
                *, lam_init):
    g = pl.program_id(0)
    i = pl.program_id(1)
    nb = (i + 1) * MIX_DIAG

    diffs = []
    for hd, (q_ref, k_ref, v_ref) in enumerate(
            ((dq0_ref, dk0_ref, dv0_ref), (dq1_ref, dk1_ref, dv1_ref))):
        head = g * DIFF_PER_STEP + hd
        diffs.append(_DiffStream(
            i, functools.partial(lambda n, head: slope_ref[head, n], head=head),
            q_ref, k_ref, v_ref, diag_scr.at[hd], dvt_scr.at[hd], s_scr.at[hd], mx_scr.at[hd],
            p_scr.at[hd], al_scr.at[hd], m_scr.at[hd], dacc_scr.at[hd]))
    sb = _SbStream(nb, sq_ref, sk_ref, sv_ref, mask_scr, tri_scr, svt_scr, z_scr, hl_scr, a_scr,
                   r_scr, sacc_scr)
    streams = diffs + [sb]

    @pl.when(i == 0)
    def _():
        for stream in streams:
            stream.init_head()
        _transpose_blocks(cv_ref, cvt_scr)
        _build_band_bias(biasvec_ref, cbias_scr)

    for stream in streams:
        stream.init_tile()

    def band():
        o_t = _band_tile(i, cbias_scr, cq_ref, ck_ref, cvt_scr)
        yc_ref[...] = _pair_rmsnorm(o_t, chgain_ref[...]).astype(yc_ref.dtype)

    def drain():
        sb.drain()
        for d in diffs:
            d.drain(nb)

    @pl.when(i == 0)
    def _():
        sb.prologue()
        for d in diffs:
            d.prologue(first_tile=True)
        band()
        drain()

    @pl.when(i > 0)
    def _():
        sb.prologue()
        for d in diffs:
            d.prologue(first_tile=False)
        band()

        def two_steps(t):
            sb.two_steps(t)
            for d in diffs:
                d.two_steps(t)

        odd = (i - 1) % 2

        @pl.when(odd == 1)
        def _():
            two_steps(2)

        def body(u, carry):
            t = 2 + 2 * odd + 4 * u
            two_steps(t)
            two_steps(t + 2)
            return carry

        lax.fori_loop(0, (i - 1) // 2, body, 0)
        sb.two_steps(nb - 2)
        for d in diffs:
            d.two_steps(nb - 2, diagonal=True)
        drain()

    lam_v = lam_ref[...]
    lam = (jnp.exp(jnp.sum(lam_v[0:1] * lam_v[1:2], axis=-1, keepdims=True))
           - jnp.exp(jnp.sum(lam_v[2:3] * lam_v[3:4], axis=-1, keepdims=True))
           + lam_init)
    for hd, d in enumerate(diffs):
        ya_ref[:, hd * LANES:(hd + 1) * LANES] = d.result(
            lam, subln_ref[...], lam_init).astype(ya_ref.dtype)
    yb_ref[...] = sb.result(sbgain_ref[...]).astype(yb_ref.dtype)


def _alibi_slope_table():
    slope = 2.0 ** (-8.0 * np.arange(1, N_HEADS + 1, dtype=np.float32) / N_HEADS)
    slope2 = (slope * np.float32(LOG2E)).astype(np.float32)
    cols, rest = [slope2], slope2
    for _ in range(DIFF_SPLIT):
        term = rest.astype(_BF16).astype(np.float32)
        cols.append(term)
        rest = rest - term
    return jnp.asarray(np.stack(cols, axis=1))


def _mix_attention(qkv, slopes, lam_vecs, subln_col, sb_col, ch_col, bias_vecs, layer, lam_init):
    tile = (MIX_TK, MIX_TQ)
    nd = DIFF_PER_STEP
    once = pl.Buffered(1)

    def q_spec(col):
        return pl.BlockSpec((MIX_TQ, LANES), lambda g, i: (i, col(g)))

    def kv_spec(col):
        return pl.BlockSpec((SEQ, LANES), lambda g, i: (0, col(g)), pipeline_mode=once)

    def pair_gain_spec():
        return pl.BlockSpec((None, None, LANES, 1), lambda g, i: (layer, g, 0, 0))

    return pl.pallas_call(
        functools.partial(_mix_kernel, lam_init=lam_init),
        grid=(N_HEADS // nd, SEQ // MIX_TQ),
        in_specs=[
            pl.BlockSpec(memory_space=pltpu.SMEM),
            pl.BlockSpec((None, 4, HEAD_DIM), lambda g, i: (layer, 0, 0)),
            pl.BlockSpec((None, LANES, 1), lambda g, i: (layer, 0, 0)),
            pair_gain_spec(), pair_gain_spec(),
            pl.BlockSpec((None, 2, BAND_TILES, 1, 2 * ATT_T), lambda g, i: (g, 0, 0, 0, 0)),
            q_spec(lambda g: QA + nd * g), q_spec(lambda g: QA + nd * g + 1),
            kv_spec(lambda g: KA + nd * g), kv_spec(lambda g: KA + nd * g + 1),
            kv_spec(lambda g: VA + nd * g), kv_spec(lambda g: VA + nd * g + 1),
            q_spec(lambda g: QB + g), kv_spec(lambda g: KB + g), kv_spec(lambda g: VB + g),
            q_spec(lambda g: QC + g), kv_spec(lambda g: KC + g), kv_spec(lambda g: VC + g),
        ],
        out_specs=[
            pl.BlockSpec((MIX_TQ, nd * LANES), lambda g, i: (i, g)),
            pl.BlockSpec((MIX_TQ, LANES), lambda g, i: (i, g)),
            pl.BlockSpec((MIX_TQ, LANES), lambda g, i: (i, g)),
        ],
        out_shape=[
            jax.ShapeDtypeStruct((SEQ, DIFF_WIDTH), _BF16),
            jax.ShapeDtypeStruct((SEQ, SB_WIDTH), _BF16),
            jax.ShapeDtypeStruct((SEQ, CH_WIDTH), _BF16),
        ],
        scratch_shapes=[
            pltpu.VMEM((nd, MIX_DIAG) + tile, _F32),
            pltpu.VMEM((nd, N_KBLK, LANES + DIFF_EXTRA, MIX_TK), _BF16),
            pltpu.VMEM((nd, 2, 2) + tile, _F32),
            pltpu.VMEM((nd, 2, 2, 1, MIX_TQ), _F32),
            pltpu.VMEM((nd, 2, 2) + tile, _BF16),
            pltpu.VMEM((nd, 2, 2, 1, MIX_TQ), _F32),
            pltpu.VMEM((nd, 2, 1, MIX_TQ), _F32),
            pltpu.VMEM((nd, 2, LANES + DIFF_EXTRA, MIX_TQ), _F32),
            pltpu.VMEM((MIX_DIAG,) + tile, _F32),
            pltpu.VMEM((SB_SUB, 2 * SB_SUB), _BF16),
            pltpu.VMEM((N_KBLK, LANES, MIX_TK), _BF16),
            pltpu.VMEM((2, 2) + tile, _F32),
            pltpu.VMEM((2, 2, MIX_TK // SB_SUB, 2 * SB_SUB, MIX_TQ), _BF16),
            pltpu.VMEM((2, 2) + tile, _BF16),
            pltpu.VMEM((2, 1, MIX_TQ), _F32),
            pltpu.VMEM((2, HEAD_DIM, MIX_TQ), _F32),
            pltpu.VMEM((N_KBLK, LANES, MIX_TK), _BF16),
            pltpu.VMEM((2, BAND_TILES, ATT_T, ATT_T), _F32),
        ],
        compiler_params=pltpu.CompilerParams(
            dimension_semantics=("arbitrary", "arbitrary"), vmem_limit_bytes=MIX_VMEM_LIMIT),
        name="token_mixers",
    )(slopes, lam_vecs, subln_col, sb_col, ch_col, bias_vecs, *([qkv] * 12))


def _band_bias_vectors(rel_table):
    t = ATT_T
    shift = np.arange(2 * t) - t
    idx = np.stack([np.clip((BAND_TILES - 1 - kk) * t + shift, -(CHUNK - 1), MAX_REL) + (CHUNK - 1)
                    for kk in range(BAND_TILES)])
    vec = rel_table[:, idx] * LOG2E
    return vec.reshape(N_HEADS // 2, 2, BAND_TILES, 1, 2 * t)


def _build_band_bias(vec_ref, bias_scr):
    t = ATT_T
    key = lax.broadcasted_iota(jnp.int32, (t, t), 0)
    qry = lax.broadcasted_iota(jnp.int32, (t, t), 1)
    for kk in range(BAND_TILES):
        back = (BAND_TILES - 1 - kk) * t
        dc = back // CHUNK + qry // CHUNK - key // CHUNK
        in_band = (dc >= 0) & (dc <= LEFT_CHUNKS)
        for hh in range(2):
            rows = jnp.broadcast_to(vec_ref[hh, kk], (t, 2 * t))
            toeplitz = pltpu.roll(rows, 0, 1, stride=1, stride_axis=0)[:, t:]
            bias_scr[hh, kk] = jnp.where(in_band, toeplitz, -jnp.inf)


def _oproj_kernel(ya_ref, yb_ref, yc_ref, w_ref, x_ref, o_ref, w_scr):
    @pl.when(pl.program_id(1) == 0)
    def _():
        w_scr[...] = w_ref[...].astype(_BF16)

    y = jnp.concatenate([ya_ref[...], yb_ref[...], yc_ref[...]], axis=1)
    o_ref[...] = x_ref[...] + jnp.dot(y, w_scr[...], preferred_element_type=_F32)


def _out_proj(y_a, y_b, y_c, w_all, x, layer, tm=1024, tn=512):
    return pl.pallas_call(
        _oproj_kernel,
        grid=(D_MODEL // tn, SEQ // tm),
        in_specs=[
            pl.BlockSpec((tm, DIFF_WIDTH), lambda j, i: (i, 0)),
            pl.BlockSpec((tm, SB_WIDTH), lambda j, i: (i, 0)),
            pl.BlockSpec((tm, CH_WIDTH), lambda j, i: (i, 0)),
            pl.BlockSpec((None, D_MODEL, tn), lambda j, i: (layer, 0, j)),
            pl.BlockSpec((tm, tn), lambda j, i: (i, j)),
        ],
        out_specs=pl.BlockSpec((tm, tn), lambda j, i: (i, j)),
        out_shape=jax.ShapeDtypeStruct((SEQ, D_MODEL), _F32),
        scratch_shapes=[pltpu.VMEM((D_MODEL, tn), _BF16)],
        compiler_params=_cparams(2),
        name="out_proj",
    )(y_a, y_b, y_c, w_all, x)


def _ffn_kernel(x_ref, halo_ref, g_ref, wg_ref, wu_ref, cw_ref, cb_ref, wd_ref, o_ref,
                h_scr, gate_scr, *, tm):
    i = pl.program_id(0)
    j = pl.program_id(1)

    def norm(x):
        ms = jnp.mean(x * x, axis=-1, keepdims=True)
        return (x * lax.rsqrt(ms + NORM_EPS) * g_ref[...]).astype(_BF16)

    @pl.when(j == 0)
    def _():
        x = x_ref[...]
        h_scr[HALO:, :] = norm(x)
        halo = jnp.where(i > 0, halo_ref[...], 0.0)
        h_scr[:HALO, :] = norm(halo)
        o_ref[...] = x

    gate_scr[...] = jnp.dot(h_scr[...], wg_ref[...], preferred_element_type=_F32)
    up = jnp.dot(h_scr[HALO:, :], wu_ref[...], preferred_element_type=_F32)
    cw = cw_ref[...]
    g = cb_ref[...]
    for t in range(CONV_WIDTH):
        lo = HALO - (CONV_WIDTH - 1) + t
        g = g + gate_scr[lo:lo + tm, :] * cw[t:t + 1]
    act = (g * jax.nn.sigmoid(g) * up).astype(_BF16)
    o_ref[...] += jnp.dot(act, wd_ref[...], preferred_element_type=_F32)


def _ffn(x, g_all, wg_all, wu_all, cw_all, cb_all, wd_all, layer, tm=512, tf=512):
    halo_blocks = tm // HALO
    return pl.pallas_call(
        functools.partial(_ffn_kernel, tm=tm),
        grid=(SEQ // tm, D_FF // tf),
        in_specs=[
            pl.BlockSpec((tm, D_MODEL), lambda i, j: (i, 0)),
            pl.BlockSpec((HALO, D_MODEL),
                         lambda i, j: (jnp.maximum(i * halo_blocks - 1, 0), 0)),
            pl.BlockSpec((None, 1, D_MODEL), lambda i, j: (layer, 0, 0)),
            pl.BlockSpec((None, D_MODEL, tf), lambda i, j: (layer, 0, j)),
            pl.BlockSpec((None, D_MODEL, tf), lambda i, j: (layer, 0, j)),
            pl.BlockSpec((None, CONV_WIDTH, tf), lambda i, j: (layer, 0, j)),
            pl.BlockSpec((None, 1, tf), lambda i, j: (layer, 0, j)),
            pl.BlockSpec((None, tf, D_MODEL), lambda i, j: (layer, j, 0)),
        ],
        out_specs=pl.BlockSpec((tm, D_MODEL), lambda i, j: (i, 0)),
        out_shape=jax.ShapeDtypeStruct((SEQ, D_MODEL), _F32),
        scratch_shapes=[
            pltpu.VMEM((tm + HALO, D_MODEL), _BF16),
            pltpu.VMEM((tm + HALO, tf), _F32),
        ],
        compiler_params=_cparams(2),
        name="gated_mlp",
    )(x, x, g_all, wg_all, wu_all, cw_all, cb_all, wd_all)


def _final_norm_kernel(x_ref, g_ref, o_ref):
    x = x_ref[...]
    ms = jnp.mean(x * x, axis=-1, keepdims=True)
    o_ref[...] = x * lax.rsqrt(ms + NORM_EPS) * g_ref[...]


def _final_norm(x, g, tm=512):
    return pl.pallas_call(
        _final_norm_kernel,
        grid=(SEQ // tm,),
        in_specs=[
            pl.BlockSpec((tm, D_MODEL), lambda i: (i, 0)),
            pl.BlockSpec((1, D_MODEL), lambda i: (0, 0)),
        ],
        out_specs=pl.BlockSpec((tm, D_MODEL), lambda i: (i, 0)),
        out_shape=jax.ShapeDtypeStruct((SEQ, D_MODEL), _F32),
        compiler_params=_cparams(1),
        name="final_norm",
    )(x, g)


def kernel(x, attn_norm, w_qkv, lambda_q1, lambda_k1, lambda_q2, lambda_k2, diff_subln,
           sb_norm, rel_bias, ch_norm, w_o, ffn_norm, w_gate, w_up, conv_w, conv_b, w_down,
           final_norm):
    batch, seq, d_model = x.shape
    assert (batch, seq, d_model) == (1, SEQ, D_MODEL)
    x2 = x.reshape(SEQ, D_MODEL)

    w_qkv_b = w_qkv.astype(_BF16)
    w_gate_b = w_gate.astype(_BF16)
    w_up_b = w_up.astype(_BF16)
    w_down_b = w_down.astype(_BF16)

    attn_g = attn_norm.reshape(DEPTH, 1, D_MODEL)
    ffn_g = ffn_norm.reshape(DEPTH, 1, D_MODEL)
    conv_b3 = conv_b.reshape(DEPTH, 1, D_FF)
    lam_vecs = jnp.stack([lambda_q1, lambda_k1, lambda_q2, lambda_k2], axis=1)
    subln_col = diff_subln.reshape(DEPTH, 2 * HEAD_DIM, 1)
    sb_col = sb_norm.reshape(DEPTH, N_HEADS // 2, 2 * HEAD_DIM, 1)
    ch_col = ch_norm.reshape(DEPTH, N_HEADS // 2, 2 * HEAD_DIM, 1)
    slopes = _alibi_slope_table()
    col_scale = jnp.asarray(_qkv_col_scale())

    for layer in range(DEPTH):
        lam_init = 0.8 - 0.6 * math.exp(-0.3 * layer)
        qkv = _qkv_proj(x2, attn_g, w_qkv_b, col_scale, layer)
        y_a, y_b, y_c = _mix_attention(qkv, slopes, lam_vecs, subln_col, sb_col, ch_col,
                                       _band_bias_vectors(rel_bias[layer]), layer, lam_init)
        x2 = _out_proj(y_a, y_b, y_c, w_o, x2, layer)
        x2 = _ffn(x2, ffn_g, w_gate_b, w_up_b, conv_w, conv_b3, w_down_b, layer)
    out = _final_norm(x2, final_norm.reshape(1, D_MODEL))
    return out.reshape(1, SEQ, D_MODEL)
```

```python
import functools
import math

import jax
import jax.numpy as jnp
import numpy as np
from jax import lax
from jax.experimental import pallas as pl
from jax.experimental.pallas import tpu as pltpu

D_MODEL = 2048
SEQ = 8192
DEPTH = 4
CHUNK = 64
HEAD_DIM = 64
N_HEADS = 8
DIFF_WIDTH = 1024
SB_WIDTH = 512
CH_WIDTH = 512
QKV_WIDTH = 6144
LEFT_CHUNKS = 8
MAX_REL = 128
D_FF = 5632
CONV_WIDTH = 3
NORM_EPS = 1e-6
SUBLN_EPS = 1e-5
SCALE = HEAD_DIM ** -0.5
LOG2E = math.log2(math.e)

LANES = 128
ATT_T = 256
N_QBLK = SEQ // ATT_T
HALO = 16
VMEM_LIMIT = 48 * 1024 * 1024
MIX_VMEM_LIMIT = 56 * 1024 * 1024

QA, KA, VA = 0, 8, 16
QB, KB, VB = 24, 28, 32
QC, KC, VC = 36, 40, 44

_F32 = jnp.float32
_BF16 = jnp.bfloat16
_NEG = -1e30
_SIGN_BIT = np.int32(-2 ** 31)


def _cparams(n_axes):
    return pltpu.CompilerParams(
        dimension_semantics=("arbitrary",) * n_axes,
        vmem_limit_bytes=VMEM_LIMIT)


def _qkv_kernel(x_ref, g_ref, w_ref, cs_ref, o_ref, h_scr):
    @pl.when(pl.program_id(1) == 0)
    def _():
        x = x_ref[...]
        ms = jnp.mean(x * x, axis=-1, keepdims=True)
        h_scr[...] = (x * lax.rsqrt(ms + NORM_EPS) * g_ref[...]).astype(_BF16)

    acc = jnp.dot(h_scr[...], w_ref[...], preferred_element_type=_F32)
    o_ref[...] = (acc * cs_ref[...]).astype(_BF16)


def _qkv_col_scale():
    cs = np.ones((1, QKV_WIDTH), np.float32)
    for q_start, k_start in ((QA, KA), (QB, KB), (QC, KC)):
        cs[:, q_start * LANES:k_start * LANES] = SCALE * LOG2E
    return cs


def _qkv_proj(x, g_all, w_all, col_scale, layer, tm=1024, tn=1024):
    return pl.pallas_call(
        _qkv_kernel,
        grid=(SEQ // tm, QKV_WIDTH // tn),
        in_specs=[
            pl.BlockSpec((tm, D_MODEL), lambda i, j: (i, 0)),
            pl.BlockSpec((None, 1, D_MODEL), lambda i, j: (layer, 0, 0)),
            pl.BlockSpec((None, D_MODEL, tn), lambda i, j: (layer, 0, j)),
            pl.BlockSpec((1, tn), lambda i, j: (0, j)),
        ],
        out_specs=pl.BlockSpec((tm, tn), lambda i, j: (i, j)),
        out_shape=jax.ShapeDtypeStruct((SEQ, QKV_WIDTH), _BF16),
        scratch_shapes=[pltpu.VMEM((tm, D_MODEL), _BF16)],
        compiler_params=_cparams(2),
        name="qkv_proj",
    )(x, g_all, w_all, col_scale)


def _split_heads_t(q_ref):
    q_t = q_ref[...].T
    dim = lax.broadcasted_iota(jnp.int32, q_t.shape, 0)
    zero = jnp.zeros_like(q_t)
    return jnp.where(dim < HEAD_DIM, q_t, zero), jnp.where(dim >= HEAD_DIM, q_t, zero)


def _key_block(ref, j):
    return ref[pl.ds(pl.multiple_of(j * ATT_T, ATT_T), ATT_T), :]


def _pair_rmsnorm(o_t, gain_col):
    top, bot = o_t[:HEAD_DIM], o_t[HEAD_DIM:]
    top = top * lax.rsqrt(jnp.mean(top * top, axis=0, keepdims=True) + NORM_EPS)
    bot = bot * lax.rsqrt(jnp.mean(bot * bot, axis=0, keepdims=True) + NORM_EPS)
    return (jnp.concatenate([top, bot], axis=0) * gain_col).T


MIX_TQ = 512
MIX_TK = 256
MIX_DIAG = MIX_TQ // MIX_TK
assert MIX_DIAG == 2
N_KBLK = SEQ // MIX_TK
DIFF_PER_STEP = 2
DIFF_SPLIT = 3
DIFF_EXTRA = 16
SB_SUB = 128
PIPE_PERIOD = 4
SCORE_SLOTS = 2
assert PIPE_PERIOD % SCORE_SLOTS == 0 and PIPE_PERIOD % 2 == 0


def _transpose_blocks(v_ref, vt_ref, rows=None):
    def body(b, carry):
        blk = v_ref[pl.ds(pl.multiple_of(b * MIX_TK, MIX_TK), MIX_TK), :].T
        if rows is None:
            vt_ref[b] = blk
        else:
            vt_ref[b, :LANES] = blk
            vt_ref[b, LANES:] = rows
        return carry

    lax.fori_loop(0, N_KBLK, body, 0)


class _DiffStream:
    def __init__(self, i, slope_row, q_ref, k_ref, v_ref, diag, vt, s, mx, p, al, m, acc):
        self.i, self.k_ref, self.v_ref = i, k_ref, v_ref
        self.diag, self.vt, self.s, self.mx, self.p, self.al, self.m, self.acc = (
            diag, vt, s, mx, p, al, m, acc)
        self.slope2 = slope_row(0)
        dim = lax.broadcasted_iota(jnp.int32, (LANES, MIX_TQ), 0)
        slope_rows = functools.reduce(
            lambda a, n: jnp.where(dim == n, slope_row(1 + n), a), range(DIFF_SPLIT),
            jnp.zeros((LANES, MIX_TQ), _F32)).astype(_BF16)
        self.q_maps = tuple(jnp.concatenate([q_half, slope_rows], axis=0)
                            for q_half in _split_heads_t(q_ref))
        lane = lax.broadcasted_iota(jnp.int32, (MIX_TK, LANES), 1)
        key_offset = lax.broadcasted_iota(jnp.int32, (MIX_TK, LANES), 0).astype(_F32)
        self.key_feat = jnp.where(lane < DIFF_SPLIT, key_offset, 0.0).astype(_BF16)

    def init_head(self):
        row = lax.broadcasted_iota(jnp.int32, (MIX_TK, MIX_TQ), 0)
        col = lax.broadcasted_iota(jnp.int32, (MIX_TK, MIX_TQ), 1)
        for d in range(MIX_DIAG):
            key = row + d * MIX_TK
            allowed = (key // CHUNK) <= (col // CHUNK)
            lead = (col - row).astype(_F32) - jnp.abs(col - key).astype(_F32)
            self.diag[d] = jnp.where(allowed, self.slope2 * lead, -jnp.inf)
        rows = lax.broadcasted_iota(jnp.int32, (DIFF_EXTRA, MIX_TK), 0)
        _transpose_blocks(self.v_ref, self.vt, jnp.where(rows == 0, 1.0, 0.0).astype(_BF16))

    def init_tile(self):
        self.m[...] = jnp.full(self.m.shape, _NEG, _F32)
        self.acc[...] = jnp.zeros(self.acc.shape, _F32)

    def _offset(self, t, diagonal):
        if diagonal is not None:
            return jnp.zeros((), _F32)
        return -self.slope2 * (self.i * MIX_TQ - t * MIX_TK).astype(_F32)

    def stage_a(self, t, ts, diagonal=None):
        slot = ts % SCORE_SLOTS
        kb = self.k_ref[pl.ds(pl.multiple_of(t * MIX_TK, MIX_TK), MIX_TK), :]
        kx = jnp.concatenate([kb, self.key_feat], axis=1)
        offset = self._offset(t, diagonal)
        for c in range(2):
            s = jnp.dot(kx, self.q_maps[c], preferred_element_type=_F32)
            if diagonal is not None:
                s = s + self.diag[diagonal]
            self.s[slot, c] = s
            self.mx[slot, c] = jnp.max(s, axis=0, keepdims=True) + offset

    def stage_b(self, t, ts, diagonal=None):
        slot, out = ts % SCORE_SLOTS, ts % 2
        offset = self._offset(t, diagonal)
        for c in range(2):
            m_old = self.m[c]
            m_new = jnp.maximum(m_old, self.mx[slot, c])
            self.p[out, c] = jnp.exp2(self.s[slot, c] - (m_new - offset)).astype(_BF16)
            self.al[out, c] = jnp.exp2(m_old - m_new)
            self.m[c] = m_new

    def stage_c(self, t, ts):
        slot = ts % 2
        vt = self.vt[t]
        for c in range(2):
            pv = jnp.dot(vt, self.p[slot, c], preferred_element_type=_F32)
            self.acc[c] = self.al[slot, c] * self.acc[c] + pv

    def result(self, lam, gain_col, lam_init):
        acc0, acc1 = self.acc[0], self.acc[1]
        o_t = (acc0[:LANES] / acc0[LANES:LANES + 1]
               - lam * (acc1[:LANES] / acc1[LANES:LANES + 1]))
        ms = jnp.mean(o_t * o_t, axis=0, keepdims=True)
        return (o_t * lax.rsqrt(ms + SUBLN_EPS) * gain_col * (1.0 - lam_init)).T


class _SbStream:
    def __init__(self, nb, q_ref, k_ref, v_ref, mask, tri, vt, z, hl, a, r, acc):
        self.nb, self.k_ref, self.v_ref = nb, k_ref, v_ref
        self.mask, self.tri, self.vt, self.z, self.hl, self.a, self.r, self.acc = (
            mask, tri, vt, z, hl, a, r, acc)
        self.q_heads = _split_heads_t(q_ref)

    def init_head(self):
        row = lax.broadcasted_iota(jnp.int32, (MIX_TK, MIX_TQ), 0)
        col = lax.broadcasted_iota(jnp.int32, (MIX_TK, MIX_TQ), 1)
        for d in range(MIX_DIAG):
            self.mask[d] = jnp.where(row + d * MIX_TK < col, 0.0, -jnp.inf)
        r2 = lax.broadcasted_iota(jnp.int32, (SB_SUB, 2 * SB_SUB), 0)
        c2 = lax.broadcasted_iota(jnp.int32, (SB_SUB, 2 * SB_SUB), 1) % SB_SUB
        self.tri[...] = jnp.where(c2 >= r2, 1.0, 0.0).astype(_BF16)
        _transpose_blocks(self.v_ref, self.vt)

    def init_tile(self):
        self.r[...] = jnp.zeros(self.r.shape, _F32)
        self.acc[...] = jnp.zeros(self.acc.shape, _F32)

    def stage_a(self, t, ts, diagonal=None):
        slot = ts % 2
        if diagonal is not None:
            diagonal = MIX_DIAG - 1 - diagonal
        start = pl.multiple_of((self.nb - 1 - t) * MIX_TK, MIX_TK)
        kb = self.k_ref[pl.ds(start, MIX_TK), :]
        for hh in range(2):
            z = jnp.dot(kb, self.q_heads[hh], preferred_element_type=_F32)
            if diagonal is not None:
                z = z + self.mask[diagonal]
            neg_abs = pltpu.bitcast(pltpu.bitcast(z, jnp.int32) | _SIGN_BIT, _F32)
            nl = jnp.maximum(z, 0.0) + jnp.log2(1.0 + jnp.exp2(neg_abs))
            hi = nl.astype(_BF16)
            lo = (nl - hi.astype(_F32)).astype(_BF16)
            self.z[slot, hh] = z
            for sub in range(MIX_TK // SB_SUB):
                rows = slice(sub * SB_SUB, (sub + 1) * SB_SUB)
                self.hl[slot, hh, sub, :SB_SUB] = hi[rows]
                self.hl[slot, hh, sub, SB_SUB:] = lo[rows]

    def stage_b(self, t, ts, diagonal=None):
        del t, diagonal
        slot = ts % 2
        tri = self.tri[...]
        for hh in range(2):
            right = self.r[hh]
            for sub in reversed(range(MIX_TK // SB_SUB)):
                rows = slice(sub * SB_SUB, (sub + 1) * SB_SUB)
                incl = jnp.dot(tri, self.hl[slot, hh, sub], preferred_element_type=_F32)
                self.a[slot, hh, rows] = jnp.exp2(self.z[slot, hh, rows] - incl - right).astype(_BF16)
                right = right + incl[0:1]
            self.r[hh] = right

    def stage_c(self, t, ts):
        slot = ts % 2
        vt = self.vt[self.nb - 1 - t]
        for hh in range(2):
            self.acc[hh] += jnp.dot(vt[hh * HEAD_DIM:(hh + 1) * HEAD_DIM], self.a[slot, hh],
                                    preferred_element_type=_F32)

    def result(self, gain_col):
        o_t = jnp.concatenate([self.acc[0], self.acc[1]], axis=0)
        return _pair_rmsnorm(o_t, gain_col)


BAND_TILES = 1 + (LEFT_CHUNKS * CHUNK) // ATT_T
BAND_SUBS = MIX_TQ // ATT_T
assert ATT_T == MIX_TK


def _band_sub_tile(i, sub, q_heads, bias_ref, k_ref, vt_ref):
    first = i * BAND_SUBS + sub - (BAND_TILES - 1)
    tiles = [jnp.maximum(first + kk, 0) for kk in range(BAND_TILES)]
    heads = []
    for hh in range(2):
        q_sub = q_heads[hh][:, sub * ATT_T:(sub + 1) * ATT_T]
        scores = []
        for kk in range(BAND_TILES):
            s = jnp.dot(_key_block(k_ref, tiles[kk]), q_sub,
                        preferred_element_type=_F32) + bias_ref[hh, kk]
            scores.append(jnp.where(first + kk >= 0, s, -jnp.inf))
        m = functools.reduce(jnp.maximum, [jnp.max(s, axis=0, keepdims=True) for s in scores])
        probs = [jnp.exp2(s - m) for s in scores]
        denom = functools.reduce(jnp.add, [jnp.sum(p, axis=0, keepdims=True) for p in probs])
        o_t = functools.reduce(jnp.add, [
            jnp.dot(vt_ref[tiles[kk]][hh * HEAD_DIM:(hh + 1) * HEAD_DIM],
                    probs[kk].astype(_BF16), preferred_element_type=_F32)
            for kk in range(BAND_TILES)])
        heads.append(o_t / denom)
    return jnp.concatenate(heads, axis=0)


def _mix_kernel(slope_ref, lam_ref, subln_ref, sbgain_ref, chgain_ref, biasvec_ref,
                dq0_ref, dq1_ref, dk0_ref, dk1_ref, dv0_ref, dv1_ref, sq_ref, sk_ref, sv_ref,
                cq_ref, ck_ref, cv_ref,
                ya_ref, yb_ref, yc_ref,
                diag_scr, dvt_scr, s_scr, mx_scr, p_scr, al_scr, m_scr, dacc_scr,
                mask_scr, tri_scr, svt_scr, z_scr, hl_scr, a_scr, r_scr, sacc_scr, cvt_scr, cbias_scr,
                *, lam_init):
    g = pl.program_id(0)
    i = pl.program_id(1)
    nb = (i + 1) * MIX_DIAG

    diffs = []
    for hd, (q_ref, k_ref, v_ref) in enumerate(
            ((dq0_ref, dk0_ref, dv0_ref), (dq1_ref, dk1_ref, dv1_ref))):
        head = g * DIFF_PER_STEP + hd
        diffs.append(_DiffStream(
            i, functools.partial(lambda n, head: slope_ref[head, n], head=head),
            q_ref, k_ref, v_ref, diag_scr.at[hd], dvt_scr.at[hd], s_scr.at[hd], mx_scr.at[hd],
            p_scr.at[hd], al_scr.at[hd], m_scr.at[hd], dacc_scr.at[hd]))
    sb = _SbStream(nb, sq_ref, sk_ref, sv_ref, mask_scr, tri_scr, svt_scr, z_scr, hl_scr, a_scr,
                   r_scr, sacc_scr)
    streams = diffs + [sb]

    @pl.when(i == 0)
    def _():
        for stream in streams:
            stream.init_head()
        _transpose_blocks(cv_ref, cvt_scr)
        _build_band_bias(biasvec_ref, cbias_scr)

    for stream in streams:
        stream.init_tile()

    band_q = _split_heads_t(cq_ref)

    def band(sub):
        o_t = _band_sub_tile(i, sub, band_q, cbias_scr, ck_ref, cvt_scr)
        yc_ref[sub * ATT_T:(sub + 1) * ATT_T, :] = _pair_rmsnorm(
            o_t, chgain_ref[...]).astype(yc_ref.dtype)

    def prologue(first_tile):
        assert BAND_SUBS == 2
        for step in range(2):
            sb.stage_a(step, step, diagonal=step)
            for d in diffs:
                d.stage_a(step, step, diagonal=step if first_tile else None)
            band(step)
            if step == 0:
                sb.stage_b(0, 0)
                for d in diffs:
                    d.stage_b(0, 0, diagonal=0 if first_tile else None)

    def pipeline_step(t, ts, a_diag=None, b_diag=None):
        for s in streams:
            s.stage_c(t - 2, (ts - 2) % PIPE_PERIOD)
        sb.stage_b(t - 1, (ts - 1) % PIPE_PERIOD)
        for d in diffs:
            d.stage_b(t - 1, (ts - 1) % PIPE_PERIOD, diagonal=b_diag)
        sb.stage_a(t, ts)
        for d in diffs:
            d.stage_a(t, ts, diagonal=a_diag)

    def two_steps(t, ts, last=False):
        pipeline_step(t, ts, a_diag=0 if last else None)
        pipeline_step(t + 1, ts + 1, a_diag=1 if last else None, b_diag=0 if last else None)

    def drain(ts_end):
        for s in streams:
            s.stage_c(nb - 2, (ts_end - 1) % PIPE_PERIOD)
        sb.stage_b(nb - 1, ts_end)
        for d in diffs:
            d.stage_b(nb - 1, ts_end, diagonal=1)
        for s in streams:
            s.stage_c(nb - 1, ts_end)

    @pl.when(i == 0)
    def _():
        prologue(first_tile=True)
        drain(1)

    @pl.when(i > 0)
    def _():
        prologue(first_tile=False)

        def body(u, carry):
            t = 2 + 4 * u
            two_steps(t, 2)
            two_steps(t + 2, 0)
            return carry

        lax.fori_loop(0, (i - 1) // 2, body, 0)

        @pl.when(i % 2 == 0)
        def _():
            two_steps(nb - 4, 2)
            two_steps(nb - 2, 0, last=True)
            drain(1)

        @pl.when(i % 2 == 1)
        def _():
            two_steps(nb - 2, 2, last=True)
            drain(3)

    lam_v = lam_ref[...]
    lam = (jnp.exp(jnp.sum(lam_v[0:1] * lam_v[1:2], axis=-1, keepdims=True))
           - jnp.exp(jnp.sum(lam_v[2:3] * lam_v[3:4], axis=-1, keepdims=True))
           + lam_init)
    for hd, d in enumerate(diffs):
        ya_ref[:, hd * LANES:(hd + 1) * LANES] = d.result(
            lam, subln_ref[...], lam_init).astype(ya_ref.dtype)
    yb_ref[...] = sb.result(sbgain_ref[...]).astype(yb_ref.dtype)


def _alibi_slope_table():
    slope = 2.0 ** (-8.0 * np.arange(1, N_HEADS + 1, dtype=np.float32) / N_HEADS)
    slope2 = (slope * np.float32(LOG2E)).astype(np.float32)
    cols, rest = [slope2], slope2
    for _ in range(DIFF_SPLIT):
        term = rest.astype(_BF16).astype(np.float32)
        cols.append(term)
        rest = rest - term
    return jnp.asarray(np.stack(cols, axis=1))


def _mix_attention(qkv, slopes, lam_vecs, subln_col, sb_col, ch_col, bias_vecs, layer, lam_init):
    tile = (MIX_TK, MIX_TQ)
    nd = DIFF_PER_STEP
    once = pl.Buffered(1)

    def q_spec(col):
        return pl.BlockSpec((MIX_TQ, LANES), lambda g, i: (i, col(g)))

    def kv_spec(col):
        return pl.BlockSpec((SEQ, LANES), lambda g, i: (0, col(g)), pipeline_mode=once)

    def pair_gain_spec():
        return pl.BlockSpec((None, None, LANES, 1), lambda g, i: (layer, g, 0, 0))

    return pl.pallas_call(
        functools.partial(_mix_kernel, lam_init=lam_init),
        grid=(N_HEADS // nd, SEQ // MIX_TQ),
        in_specs=[
            pl.BlockSpec(memory_space=pltpu.SMEM),
            pl.BlockSpec((None, 4, HEAD_DIM), lambda g, i: (layer, 0, 0)),
            pl.BlockSpec((None, LANES, 1), lambda g, i: (layer, 0, 0)),
            pair_gain_spec(), pair_gain_spec(),
            pl.BlockSpec((None, 2, BAND_TILES, 1, 2 * ATT_T), lambda g, i: (g, 0, 0, 0, 0)),
            q_spec(lambda g: QA + nd * g), q_spec(lambda g: QA + nd * g + 1),
            kv_spec(lambda g: KA + nd * g), kv_spec(lambda g: KA + nd * g + 1),
            kv_spec(lambda g: VA + nd * g), kv_spec(lambda g: VA + nd * g + 1),
            q_spec(lambda g: QB + g), kv_spec(lambda g: KB + g), kv_spec(lambda g: VB + g),
            q_spec(lambda g: QC + g), kv_spec(lambda g: KC + g), kv_spec(lambda g: VC + g),
        ],
        out_specs=[
            pl.BlockSpec((MIX_TQ, nd * LANES), lambda g, i: (i, g)),
            pl.BlockSpec((MIX_TQ, LANES), lambda g, i: (i, g)),
            pl.BlockSpec((MIX_TQ, LANES), lambda g, i: (i, g)),
        ],
        out_shape=[
            jax.ShapeDtypeStruct((SEQ, DIFF_WIDTH), _BF16),
            jax.ShapeDtypeStruct((SEQ, SB_WIDTH), _BF16),
            jax.ShapeDtypeStruct((SEQ, CH_WIDTH), _BF16),
        ],
        scratch_shapes=[
            pltpu.VMEM((nd, MIX_DIAG) + tile, _F32),
            pltpu.VMEM((nd, N_KBLK, LANES + DIFF_EXTRA, MIX_TK), _BF16),
            pltpu.VMEM((nd, SCORE_SLOTS, 2) + tile, _F32),
            pltpu.VMEM((nd, SCORE_SLOTS, 2, 1, MIX_TQ), _F32),
            pltpu.VMEM((nd, 2, 2) + tile, _BF16),
            pltpu.VMEM((nd, 2, 2, 1, MIX_TQ), _F32),
            pltpu.VMEM((nd, 2, 1, MIX_TQ), _F32),
            pltpu.VMEM((nd, 2, LANES + DIFF_EXTRA, MIX_TQ), _F32),
            pltpu.VMEM((MIX_DIAG,) + tile, _F32),
            pltpu.VMEM((SB_SUB, 2 * SB_SUB), _BF16),
            pltpu.VMEM((N_KBLK, LANES, MIX_TK), _BF16),
            pltpu.VMEM((2, 2) + tile, _F32),
            pltpu.VMEM((2, 2, MIX_TK // SB_SUB, 2 * SB_SUB, MIX_TQ), _BF16),
            pltpu.VMEM((2, 2) + tile, _BF16),
            pltpu.VMEM((2, 1, MIX_TQ), _F32),
            pltpu.VMEM((2, HEAD_DIM, MIX_TQ), _F32),
            pltpu.VMEM((N_KBLK, LANES, MIX_TK), _BF16),
            pltpu.VMEM((2, BAND_TILES, ATT_T, ATT_T), _F32),
        ],
        compiler_params=pltpu.CompilerParams(
            dimension_semantics=("arbitrary", "arbitrary"), vmem_limit_bytes=MIX_VMEM_LIMIT),
        name="token_mixers",
    )(slopes, lam_vecs, subln_col, sb_col, ch_col, bias_vecs, *([qkv] * 12))


def _band_bias_vectors(rel_table):
    t = ATT_T
    shift = np.arange(2 * t) - t
    idx = np.stack([np.clip((BAND_TILES - 1 - kk) * t + shift, -(CHUNK - 1), MAX_REL) + (CHUNK - 1)
                    for kk in range(BAND_TILES)])
    vec = rel_table[:, idx] * LOG2E
    return vec.reshape(N_HEADS // 2, 2, BAND_TILES, 1, 2 * t)


def _build_band_bias(vec_ref, bias_scr):
    t = ATT_T
    key = lax.broadcasted_iota(jnp.int32, (t, t), 0)
    qry = lax.broadcasted_iota(jnp.int32, (t, t), 1)
    for kk in range(BAND_TILES):
        back = (BAND_TILES - 1 - kk) * t
        dc = back // CHUNK + qry // CHUNK - key // CHUNK
        in_band = (dc >= 0) & (dc <= LEFT_CHUNKS)
        for hh in range(2):
            rows = jnp.broadcast_to(vec_ref[hh, kk], (t, 2 * t))
            toeplitz = pltpu.roll(rows, 0, 1, stride=1, stride_axis=0)[:, t:]
            bias_scr[hh, kk] = jnp.where(in_band, toeplitz, -jnp.inf)


def _oproj_kernel(ya_ref, yb_ref, yc_ref, w_ref, x_ref, o_ref, w_scr):
    @pl.when(pl.program_id(1) == 0)
    def _():
        w_scr[...] = w_ref[...].astype(_BF16)

    y = jnp.concatenate([ya_ref[...], yb_ref[...], yc_ref[...]], axis=1)
    o_ref[...] = x_ref[...] + jnp.dot(y, w_scr[...], preferred_element_type=_F32)


def _out_proj(y_a, y_b, y_c, w_all, x, layer, tm=1024, tn=512):
    return pl.pallas_call(
        _oproj_kernel,
        grid=(D_MODEL // tn, SEQ // tm),
        in_specs=[
            pl.BlockSpec((tm, DIFF_WIDTH), lambda j, i: (i, 0)),
            pl.BlockSpec((tm, SB_WIDTH), lambda j, i: (i, 0)),
            pl.BlockSpec((tm, CH_WIDTH), lambda j, i: (i, 0)),
            pl.BlockSpec((None, D_MODEL, tn), lambda j, i: (layer, 0, j)),
            pl.BlockSpec((tm, tn), lambda j, i: (i, j)),
        ],
        out_specs=pl.BlockSpec((tm, tn), lambda j, i: (i, j)),
        out_shape=jax.ShapeDtypeStruct((SEQ, D_MODEL), _F32),
        scratch_shapes=[pltpu.VMEM((D_MODEL, tn), _BF16)],
        compiler_params=_cparams(2),
        name="out_proj",
    )(y_a, y_b, y_c, w_all, x)


def _ffn_kernel(x_ref, halo_ref, g_ref, wg_ref, wu_ref, cw_ref, cb_ref, wd_ref, o_ref,
                h_scr, gate_scr, *, tm):
    i = pl.program_id(0)
    j = pl.program_id(1)

    def norm(x):
        ms = jnp.mean(x * x, axis=-1, keepdims=True)
        return (x * lax.rsqrt(ms + NORM_EPS) * g_ref[...]).astype(_BF16)

    @pl.when(j == 0)
    def _():
        x = x_ref[...]
        h_scr[HALO:, :] = norm(x)
        halo = jnp.where(i > 0, halo_ref[...], 0.0)
        h_scr[:HALO, :] = norm(halo)
        o_ref[...] = x

    gate_scr[...] = jnp.dot(h_scr[...], wg_ref[...], preferred_element_type=_F32)
    up = jnp.dot(h_scr[HALO:, :], wu_ref[...], preferred_element_type=_F32)
    cw = cw_ref[...]
    g = cb_ref[...]
    for t in range(CONV_WIDTH):
        lo = HALO - (CONV_WIDTH - 1) + t
        g = g + gate_scr[lo:lo + tm, :] * cw[t:t + 1]
    act = (g * jax.nn.sigmoid(g) * up).astype(_BF16)
    o_ref[...] += jnp.dot(act, wd_ref[...], preferred_element_type=_F32)


def _ffn(x, g_all, wg_all, wu_all, cw_all, cb_all, wd_all, layer, tm=512, tf=512):
    halo_blocks = tm // HALO
    return pl.pallas_call(
        functools.partial(_ffn_kernel, tm=tm),
        grid=(SEQ // tm, D_FF // tf),
        in_specs=[
            pl.BlockSpec((tm, D_MODEL), lambda i, j: (i, 0)),
            pl.BlockSpec((HALO, D_MODEL),
                         lambda i, j: (jnp.maximum(i * halo_blocks - 1, 0), 0)),
            pl.BlockSpec((None, 1, D_MODEL), lambda i, j: (layer, 0, 0)),
            pl.BlockSpec((None, D_MODEL, tf), lambda i, j: (layer, 0, j)),
            pl.BlockSpec((None, D_MODEL, tf), lambda i, j: (layer, 0, j)),
            pl.BlockSpec((None, CONV_WIDTH, tf), lambda i, j: (layer, 0, j)),
            pl.BlockSpec((None, 1, tf), lambda i, j: (layer, 0, j)),
            pl.BlockSpec((None, tf, D_MODEL), lambda i, j: (layer, j, 0)),
        ],
        out_specs=pl.BlockSpec((tm, D_MODEL), lambda i, j: (i, 0)),
        out_shape=jax.ShapeDtypeStruct((SEQ, D_MODEL), _F32),
        scratch_shapes=[
            pltpu.VMEM((tm + HALO, D_MODEL), _BF16),
            pltpu.VMEM((tm + HALO, tf), _F32),
        ],
        compiler_params=_cparams(2),
        name="gated_mlp",
    )(x, x, g_all, wg_all, wu_all, cw_all, cb_all, wd_all)


def _final_norm_kernel(x_ref, g_ref, o_ref):
    x = x_ref[...]
    ms = jnp.mean(x * x, axis=-1, keepdims=True)
    o_ref[...] = x * lax.rsqrt(ms + NORM_EPS) * g_ref[...]


def _final_norm(x, g, tm=512):
    return pl.pallas_call(
        _final_norm_kernel,
        grid=(SEQ // tm,),
        in_specs=[
            pl.BlockSpec((tm, D_MODEL), lambda i: (i, 0)),
            pl.BlockSpec((1, D_MODEL), lambda i: (0, 0)),
        ],
        out_specs=pl.BlockSpec((tm, D_MODEL), lambda i: (i, 0)),
        out_shape=jax.ShapeDtypeStruct((SEQ, D_MODEL), _F32),
        compiler_params=_cparams(1),
        name="final_norm",
    )(x, g)


def kernel(x, attn_norm, w_qkv, lambda_q1, lambda_k1, lambda_q2, lambda_k2, diff_subln,
           sb_norm, rel_bias, ch_norm, w_o, ffn_norm, w_gate, w_up, conv_w, conv_b, w_down,
           final_norm):
    batch, seq, d_model = x.shape
    assert (batch, seq, d_model) == (1, SEQ, D_MODEL)
    x2 = x.reshape(SEQ, D_MODEL)

    w_qkv_b = w_qkv.astype(_BF16)
    w_gate_b = w_gate.astype(_BF16)
    w_up_b = w_up.astype(_BF16)
    w_down_b = w_down.astype(_BF16)

    attn_g = attn_norm.reshape(DEPTH, 1, D_MODEL)
    ffn_g = ffn_norm.reshape(DEPTH, 1, D_MODEL)
    conv_b3 = conv_b.reshape(DEPTH, 1, D_FF)
    lam_vecs = jnp.stack([lambda_q1, lambda_k1, lambda_q2, lambda_k2], axis=1)
    subln_col = diff_subln.reshape(DEPTH, 2 * HEAD_DIM, 1)
    sb_col = sb_norm.reshape(DEPTH, N_HEADS // 2, 2 * HEAD_DIM, 1)
    ch_col = ch_norm.reshape(DEPTH, N_HEADS // 2, 2 * HEAD_DIM, 1)
    slopes = _alibi_slope_table()
    col_scale = jnp.asarray(_qkv_col_scale())

    for layer in range(DEPTH):
        lam_init = 0.8 - 0.6 * math.exp(-0.3 * layer)
        qkv = _qkv_proj(x2, attn_g, w_qkv_b, col_scale, layer)
        y_a, y_b, y_c = _mix_attention(qkv, slopes, lam_vecs, subln_col, sb_col, ch_col,
                                       _band_bias_vectors(rel_bias[layer]), layer, lam_init)
        x2 = _out_proj(y_a, y_b, y_c, w_o, x2, layer)
        x2 = _ffn(x2, ffn_g, w_gate_b, w_up_b, conv_w, conv_b3, w_down_b, layer)
    out = _final_norm(x2, final_norm.reshape(1, D_MODEL))
    return out.reshape(1, SEQ, D_MODEL)
```

```python
import functools
import math

import jax
import jax.numpy as jnp
import numpy as np
from jax import lax
from jax.experimental import pallas as pl
from jax.experimental.pallas import tpu as pltpu

D_MODEL = 2048
SEQ = 8192
DEPTH = 4
CHUNK = 64
HEAD_DIM = 64
N_HEADS = 8
DIFF_WIDTH = 1024
SB_WIDTH = 512
CH_WIDTH = 512
QKV_WIDTH = 6144
LEFT_CHUNKS = 8
MAX_REL = 128
D_FF = 5632
CONV_WIDTH = 3
NORM_EPS = 1e-6
SUBLN_EPS = 1e-5
SCALE = HEAD_DIM ** -0.5
LOG2E = math.log2(math.e)

LANES = 128
ATT_T = 256
N_QBLK = SEQ // ATT_T
HALO = 16
VMEM_LIMIT = 48 * 1024 * 1024
MIX_VMEM_LIMIT = 56 * 1024 * 1024

QA, KA, VA = 0, 8, 16
QB, KB, VB = 24, 28, 32
QC, KC, VC = 36, 40, 44

_F32 = jnp.float32
_BF16 = jnp.bfloat16
_NEG = -1e30
_SIGN_BIT = np.int32(-2 ** 31)


def _cparams(n_axes):
    return pltpu.CompilerParams(
        dimension_semantics=("arbitrary",) * n_axes,
        vmem_limit_bytes=VMEM_LIMIT)


def _qkv_kernel(x_ref, g_ref, w_ref, cs_ref, o_ref, h_scr):
    @pl.when(pl.program_id(1) == 0)
    def _():
        x = x_ref[...]
        ms = jnp.mean(x * x, axis=-1, keepdims=True)
        h_scr[...] = (x * lax.rsqrt(ms + NORM_EPS) * g_ref[...]).astype(_BF16)

    acc = jnp.dot(h_scr[...], w_ref[...], preferred_element_type=_F32)
    o_ref[...] = (acc * cs_ref[...]).astype(_BF16)


def _qkv_col_scale():
    cs = np.ones((1, QKV_WIDTH), np.float32)
    for q_start, k_start in ((QA, KA), (QB, KB), (QC, KC)):
        cs[:, q_start * LANES:k_start * LANES] = SCALE * LOG2E
    return cs


def _qkv_proj(x, g_all, w_all, col_scale, layer, tm=1024, tn=1024):
    return pl.pallas_call(
        _qkv_kernel,
        grid=(SEQ // tm, QKV_WIDTH // tn),
        in_specs=[
            pl.BlockSpec((tm, D_MODEL), lambda i, j: (i, 0)),
            pl.BlockSpec((None, 1, D_MODEL), lambda i, j: (layer, 0, 0)),
            pl.BlockSpec((None, D_MODEL, tn), lambda i, j: (layer, 0, j)),
            pl.BlockSpec((1, tn), lambda i, j: (0, j)),
        ],
        out_specs=pl.BlockSpec((tm, tn), lambda i, j: (i, j)),
        out_shape=jax.ShapeDtypeStruct((SEQ, QKV_WIDTH), _BF16),
        scratch_shapes=[pltpu.VMEM((tm, D_MODEL), _BF16)],
        compiler_params=_cparams(2),
        name="qkv_proj",
    )(x, g_all, w_all, col_scale)


def _split_heads_t(q_ref):
    q_t = q_ref[...].T
    dim = lax.broadcasted_iota(jnp.int32, q_t.shape, 0)
    zero = jnp.zeros_like(q_t)
    return jnp.where(dim < HEAD_DIM, q_t, zero), jnp.where(dim >= HEAD_DIM, q_t, zero)


def _key_block(ref, j):
    return ref[pl.ds(pl.multiple_of(j * ATT_T, ATT_T), ATT_T), :]


def _pair_rmsnorm(o_t, gain_col):
    top, bot = o_t[:HEAD_DIM], o_t[HEAD_DIM:]
    top = top * lax.rsqrt(jnp.mean(top * top, axis=0, keepdims=True) + NORM_EPS)
    bot = bot * lax.rsqrt(jnp.mean(bot * bot, axis=0, keepdims=True) + NORM_EPS)
    return (jnp.concatenate([top, bot], axis=0) * gain_col).T


MIX_TQ = 512
MIX_TK = 256
MIX_DIAG = MIX_TQ // MIX_TK
assert MIX_DIAG == 2
N_KBLK = SEQ // MIX_TK
DIFF_PER_STEP = 2
DIFF_SPLIT = 3
DIFF_EXTRA = 16
SB_SUB = 128
PIPE_PERIOD = 4
SCORE_SLOTS = 2
assert PIPE_PERIOD % SCORE_SLOTS == 0 and PIPE_PERIOD % 2 == 0


def _transpose_blocks(v_ref, vt_ref, rows=None):
    def body(b, carry):
        blk = v_ref[pl.ds(pl.multiple_of(b * MIX_TK, MIX_TK), MIX_TK), :].T
        if rows is None:
            vt_ref[b] = blk
        else:
            vt_ref[b, :LANES] = blk
            vt_ref[b, LANES:] = rows
        return carry

    lax.fori_loop(0, N_KBLK, body, 0)


class _DiffStream:
    def __init__(self, i, slope_row, q_ref, k_ref, v_ref, diag, vt, s, mx, p, al, m, acc):
        self.i, self.k_ref, self.v_ref = i, k_ref, v_ref
        self.diag, self.vt, self.s, self.mx, self.p, self.al, self.m, self.acc = (
            diag, vt, s, mx, p, al, m, acc)
        self.slope2 = slope_row(0)
        dim = lax.broadcasted_iota(jnp.int32, (LANES, MIX_TQ), 0)
        slope_rows = functools.reduce(
            lambda a, n: jnp.where(dim == n, slope_row(1 + n), a), range(DIFF_SPLIT),
            jnp.zeros((LANES, MIX_TQ), _F32)).astype(_BF16)
        self.q_maps = tuple(jnp.concatenate([q_half, slope_rows], axis=0)
                            for q_half in _split_heads_t(q_ref))
        lane = lax.broadcasted_iota(jnp.int32, (MIX_TK, LANES), 1)
        key_offset = lax.broadcasted_iota(jnp.int32, (MIX_TK, LANES), 0).astype(_F32)
        self.key_feat = jnp.where(lane < DIFF_SPLIT, key_offset, 0.0).astype(_BF16)

    def init_head(self):
        row = lax.broadcasted_iota(jnp.int32, (MIX_TK, MIX_TQ), 0)
        col = lax.broadcasted_iota(jnp.int32, (MIX_TK, MIX_TQ), 1)
        for d in range(MIX_DIAG):
            key = row + d * MIX_TK
            allowed = (key // CHUNK) <= (col // CHUNK)
            lead = (col - row).astype(_F32) - jnp.abs(col - key).astype(_F32)
            self.diag[d] = jnp.where(allowed, self.slope2 * lead, -jnp.inf)
        rows = lax.broadcasted_iota(jnp.int32, (DIFF_EXTRA, MIX_TK), 0)
        _transpose_blocks(self.v_ref, self.vt, jnp.where(rows == 0, 1.0, 0.0).astype(_BF16))

    def init_tile(self):
        self.m[...] = jnp.full(self.m.shape, _NEG, _F32)
        self.acc[...] = jnp.zeros(self.acc.shape, _F32)

    def _offset(self, t, diagonal):
        if diagonal is not None:
            return jnp.zeros((), _F32)
        return -self.slope2 * (self.i * MIX_TQ - t * MIX_TK).astype(_F32)

    def stage_a(self, t, ts, diagonal=None):
        slot = ts % SCORE_SLOTS
        kb = self.k_ref[pl.ds(pl.multiple_of(t * MIX_TK, MIX_TK), MIX_TK), :]
        kx = jnp.concatenate([kb, self.key_feat], axis=1)
        offset = self._offset(t, diagonal)
        for c in range(2):
            s = jnp.dot(kx, self.q_maps[c], preferred_element_type=_F32)
            if diagonal is not None:
                s = s + self.diag[diagonal]
            self.s[slot, c] = s
            self.mx[slot, c] = jnp.max(s, axis=0, keepdims=True) + offset

    def stage_b(self, t, ts, diagonal=None):
        slot, out = ts % SCORE_SLOTS, ts % 2
        offset = self._offset(t, diagonal)
        for c in range(2):
            m_old = self.m[c]
            m_new = jnp.maximum(m_old, self.mx[slot, c])
            self.p[out, c] = jnp.exp2(self.s[slot, c] - (m_new - offset)).astype(_BF16)
            self.al[out, c] = jnp.exp2(m_old - m_new)
            self.m[c] = m_new

    def stage_c(self, t, ts):
        slot = ts % 2
        vt = self.vt[t]
        for c in range(2):
            pv = jnp.dot(vt, self.p[slot, c], preferred_element_type=_F32)
            self.acc[c] = self.al[slot, c] * self.acc[c] + pv

    def result(self, lam, gain_col, lam_init):
        acc0, acc1 = self.acc[0], self.acc[1]
        o_t = (acc0[:LANES] / acc0[LANES:LANES + 1]
               - lam * (acc1[:LANES] / acc1[LANES:LANES + 1]))
        ms = jnp.mean(o_t * o_t, axis=0, keepdims=True)
        return (o_t * lax.rsqrt(ms + SUBLN_EPS) * gain_col * (1.0 - lam_init)).T


class _SbStream:
    def __init__(self, nb, q_ref, k_ref, v_ref, mask, tri, vt, z, hl, a, r, acc):
        self.nb, self.k_ref, self.v_ref = nb, k_ref, v_ref
        self.mask, self.tri, self.vt, self.z, self.hl, self.a, self.r, self.acc = (
            mask, tri, vt, z, hl, a, r, acc)
        self.q_heads = _split_heads_t(q_ref)

    def init_head(self):
        row = lax.broadcasted_iota(jnp.int32, (MIX_TK, MIX_TQ), 0)
        col = lax.broadcasted_iota(jnp.int32, (MIX_TK, MIX_TQ), 1)
        for d in range(MIX_DIAG):
            self.mask[d] = jnp.where(row + d * MIX_TK < col, 0.0, -jnp.inf)
        r2 = lax.broadcasted_iota(jnp.int32, (SB_SUB, 2 * SB_SUB), 0)
        c2 = lax.broadcasted_iota(jnp.int32, (SB_SUB, 2 * SB_SUB), 1) % SB_SUB
        self.tri[...] = jnp.where(c2 >= r2, 1.0, 0.0).astype(_BF16)
        _transpose_blocks(self.v_ref, self.vt)

    def init_tile(self):
        self.r[...] = jnp.zeros(self.r.shape, _F32)
        self.acc[...] = jnp.zeros(self.acc.shape, _F32)

    def stage_a(self, t, ts, diagonal=None):
        slot = ts % 2
        if diagonal is not None:
            diagonal = MIX_DIAG - 1 - diagonal
        start = pl.multiple_of((self.nb - 1 - t) * MIX_TK, MIX_TK)
        kb = self.k_ref[pl.ds(start, MIX_TK), :]
        for hh in range(2):
            z = jnp.dot(kb, self.q_heads[hh], preferred_element_type=_F32)
            if diagonal is not None:
                z = z + self.mask[diagonal]
            neg_abs = pltpu.bitcast(pltpu.bitcast(z, jnp.int32) | _SIGN_BIT, _F32)
            nl = jnp.maximum(z, 0.0) + jnp.log2(1.0 + jnp.exp2(neg_abs))
            hi = nl.astype(_BF16)
            lo = (nl - hi.astype(_F32)).astype(_BF16)
            self.z[slot, hh] = z
            for sub in range(MIX_TK // SB_SUB):
                rows = slice(sub * SB_SUB, (sub + 1) * SB_SUB)
                self.hl[slot, hh, sub, :SB_SUB] = hi[rows]
                self.hl[slot, hh, sub, SB_SUB:] = lo[rows]

    def stage_b(self, t, ts, diagonal=None):
        del t, diagonal
        slot = ts % 2
        tri = self.tri[...]
        for hh in range(2):
            right = self.r[hh]
            for sub in reversed(range(MIX_TK // SB_SUB)):
                rows = slice(sub * SB_SUB, (sub + 1) * SB_SUB)
                incl = jnp.dot(tri, self.hl[slot, hh, sub], preferred_element_type=_F32)
                self.a[slot, hh, rows] = jnp.exp2(self.z[slot, hh, rows] - incl - right).astype(_BF16)
                right = right + incl[0:1]
            self.r[hh] = right

    def stage_c(self, t, ts):
        slot = ts % 2
        vt = self.vt[self.nb - 1 - t]
        for hh in range(2):
            self.acc[hh] += jnp.dot(vt[hh * HEAD_DIM:(hh + 1) * HEAD_DIM], self.a[slot, hh],
                                    preferred_element_type=_F32)

    def result(self, gain_col):
        o_t = jnp.concatenate([self.acc[0], self.acc[1]], axis=0)
        return _pair_rmsnorm(o_t, gain_col)


BAND_TILES = 1 + (LEFT_CHUNKS * CHUNK) // ATT_T
BAND_SUBS = MIX_TQ // ATT_T
assert ATT_T == MIX_TK


def _band_sub_tile(i, sub, q_heads, bias_ref, k_ref, vt_ref):
    first = i * BAND_SUBS + sub - (BAND_TILES - 1)
    tiles = [jnp.maximum(first + kk, 0) for kk in range(BAND_TILES)]
    heads = []
    for hh in range(2):
        q_sub = q_heads[hh][:, sub * ATT_T:(sub + 1) * ATT_T]
        scores = []
        for kk in range(BAND_TILES):
            s = jnp.dot(_key_block(k_ref, tiles[kk]), q_sub,
                        preferred_element_type=_F32) + bias_ref[hh, kk]
            scores.append(jnp.where(first + kk >= 0, s, -jnp.inf))
        m = functools.reduce(jnp.maximum, [jnp.max(s, axis=0, keepdims=True) for s in scores])
        probs = [jnp.exp2(s - m) for s in scores]
        denom = functools.reduce(jnp.add, [jnp.sum(p, axis=0, keepdims=True) for p in probs])
        o_t = functools.reduce(jnp.add, [
            jnp.dot(vt_ref[tiles[kk]][hh * HEAD_DIM:(hh + 1) * HEAD_DIM],
                    probs[kk].astype(_BF16), preferred_element_type=_F32)
            for kk in range(BAND_TILES)])
        heads.append(o_t / denom)
    return jnp.concatenate(heads, axis=0)


def _mix_kernel(slope_ref, lam_ref, subln_ref, sbgain_ref, chgain_ref, biasvec_ref,
                dq0_ref, dq1_ref, dk0_ref, dk1_ref, dv0_ref, dv1_ref, sq_ref, sk_ref, sv_ref,
                cq_ref, ck_ref, cv_ref,
                ya_ref, yb_ref, yc_ref,
                diag_scr, dvt_scr, s_scr, mx_scr, p_scr, al_scr, m_scr, dacc_scr,
                mask_scr, tri_scr, svt_scr, z_scr, hl_scr, a_scr, r_scr, sacc_scr, cvt_scr, cbias_scr,
                *, lam_init):
    g = pl.program_id(0)
    i = pl.program_id(1)
    nb = (i + 1) * MIX_DIAG

    diffs = []
    for hd, (q_ref, k_ref, v_ref) in enumerate(
            ((dq0_ref, dk0_ref, dv0_ref), (dq1_ref, dk1_ref, dv1_ref))):
        head = g * DIFF_PER_STEP + hd
        diffs.append(_DiffStream(
            i, functools.partial(lambda n, head: slope_ref[head, n], head=head),
            q_ref, k_ref, v_ref, diag_scr.at[hd], dvt_scr.at[hd], s_scr.at[hd], mx_scr.at[hd],
            p_scr.at[hd], al_scr.at[hd], m_scr.at[hd], dacc_scr.at[hd]))
    sb = _SbStream(nb, sq_ref, sk_ref, sv_ref, mask_scr, tri_scr, svt_scr, z_scr, hl_scr, a_scr,
                   r_scr, sacc_scr)
    streams = diffs + [sb]

    @pl.when(i == 0)
    def _():
        for stream in streams:
            stream.init_head()
        _transpose_blocks(cv_ref, cvt_scr)
        _build_band_bias(biasvec_ref, cbias_scr)

    for stream in streams:
        stream.init_tile()

    band_q = _split_heads_t(cq_ref)

    def band(sub):
        o_t = _band_sub_tile(i, sub, band_q, cbias_scr, ck_ref, cvt_scr)
        yc_ref[sub * ATT_T:(sub + 1) * ATT_T, :] = _pair_rmsnorm(
            o_t, chgain_ref[...]).astype(yc_ref.dtype)

    def prologue(first_tile):
        assert BAND_SUBS == 2
        for step in range(2):
            sb.stage_a(step, step, diagonal=step)
            for d in diffs:
                d.stage_a(step, step, diagonal=step if first_tile else None)
            band(step)
            if step == 0:
                sb.stage_b(0, 0)
                for d in diffs:
                    d.stage_b(0, 0, diagonal=0 if first_tile else None)

    def pipeline_step(t, ts, a_diag=None, b_diag=None):
        for s in streams:
            s.stage_c(t - 2, (ts - 2) % PIPE_PERIOD)
        sb.stage_b(t - 1, (ts - 1) % PIPE_PERIOD)
        for d in diffs:
            d.stage_b(t - 1, (ts - 1) % PIPE_PERIOD, diagonal=b_diag)
        sb.stage_a(t, ts)
        for d in diffs:
            d.stage_a(t, ts, diagonal=a_diag)

    def two_steps(t, ts, last=False):
        pipeline_step(t, ts, a_diag=0 if last else None)
        pipeline_step(t + 1, ts + 1, a_diag=1 if last else None, b_diag=0 if last else None)

    def drain(ts_end):
        for s in streams:
            s.stage_c(nb - 2, (ts_end - 1) % PIPE_PERIOD)
        sb.stage_b(nb - 1, ts_end)
        for d in diffs:
            d.stage_b(nb - 1, ts_end, diagonal=1)
        for s in streams:
            s.stage_c(nb - 1, ts_end)

    @pl.when(i == 0)
    def _():
        prologue(first_tile=True)
        drain(1)

    @pl.when(i > 0)
    def _():
        prologue(first_tile=False)

        def body(u, carry):
            t = 2 + 4 * u
            two_steps(t, 2)
            two_steps(t + 2, 0)
            return carry

        lax.fori_loop(0, (i - 1) // 2, body, 0)

        @pl.when(i % 2 == 0)
        def _():
            two_steps(nb - 4, 2)
            two_steps(nb - 2, 0, last=True)
            drain(1)

        @pl.when(i % 2 == 1)
        def _():
            two_steps(nb - 2, 2, last=True)
            drain(3)

    lam_v = lam_ref[...]
    lam = (jnp.exp(jnp.sum(lam_v[0:1] * lam_v[1:2], axis=-1, keepdims=True))
           - jnp.exp(jnp.sum(lam_v[2:3] * lam_v[3:4], axis=-1, keepdims=True))
           + lam_init)
    for hd, d in enumerate(diffs):
        ya_ref[:, hd * LANES:(hd + 1) * LANES] = d.result(
            lam, subln_ref[...], lam_init).astype(ya_ref.dtype)
    yb_ref[...] = sb.result(sbgain_ref[...]).astype(yb_ref.dtype)


def _alibi_slope_table():
    slope = 2.0 ** (-8.0 * np.arange(1, N_HEADS + 1, dtype=np.float32) / N_HEADS)
    slope2 = (slope * np.float32(LOG2E)).astype(np.float32)
    cols, rest = [slope2], slope2
    for _ in range(DIFF_SPLIT):
        term = rest.astype(_BF16).astype(np.float32)
        cols.append(term)
        rest = rest - term
    return jnp.asarray(np.stack(cols, axis=1))


def _mix_attention(qkv, slopes, lam_vecs, subln_col, sb_col, ch_col, bias_vecs, layer, lam_init):
    tile = (MIX_TK, MIX_TQ)
    nd = DIFF_PER_STEP
    once = pl.Buffered(1)

    def q_spec(col):
        return pl.BlockSpec((MIX_TQ, LANES), lambda g, i: (i, col(g)))

    def kv_spec(col):
        return pl.BlockSpec((SEQ, LANES), lambda g, i: (0, col(g)), pipeline_mode=once)

    def pair_gain_spec():
        return pl.BlockSpec((None, None, LANES, 1), lambda g, i: (layer, g, 0, 0))

    return pl.pallas_call(
        functools.partial(_mix_kernel, lam_init=lam_init),
        grid=(N_HEADS // nd, SEQ // MIX_TQ),
        in_specs=[
            pl.BlockSpec(memory_space=pltpu.SMEM),
            pl.BlockSpec((None, 4, HEAD_DIM), lambda g, i: (layer, 0, 0)),
            pl.BlockSpec((None, LANES, 1), lambda g, i: (layer, 0, 0)),
            pair_gain_spec(), pair_gain_spec(),
            pl.BlockSpec((None, 2, BAND_TILES, 1, 2 * ATT_T), lambda g, i: (g, 0, 0, 0, 0)),
            q_spec(lambda g: QA + nd * g), q_spec(lambda g: QA + nd * g + 1),
            kv_spec(lambda g: KA + nd * g), kv_spec(lambda g: KA + nd * g + 1),
            kv_spec(lambda g: VA + nd * g), kv_spec(lambda g: VA + nd * g + 1),
            q_spec(lambda g: QB + g), kv_spec(lambda g: KB + g), kv_spec(lambda g: VB + g),
            q_spec(lambda g: QC + g), kv_spec(lambda g: KC + g), kv_spec(lambda g: VC + g),
        ],
        out_specs=[
            pl.BlockSpec((MIX_TQ, nd * LANES), lambda g, i: (i, g)),
            pl.BlockSpec((MIX_TQ, LANES), lambda g, i: (i, g)),
            pl.BlockSpec((MIX_TQ, LANES), lambda g, i: (i, g)),
        ],
        out_shape=[
            jax.ShapeDtypeStruct((SEQ, DIFF_WIDTH), _BF16),
            jax.ShapeDtypeStruct((SEQ, SB_WIDTH), _BF16),
            jax.ShapeDtypeStruct((SEQ, CH_WIDTH), _BF16),
        ],
        scratch_shapes=[
            pltpu.VMEM((nd, MIX_DIAG) + tile, _F32),
            pltpu.VMEM((nd, N_KBLK, LANES + DIFF_EXTRA, MIX_TK), _BF16),
            pltpu.VMEM((nd, SCORE_SLOTS, 2) + tile, _F32),
            pltpu.VMEM((nd, SCORE_SLOTS, 2, 1, MIX_TQ), _F32),
            pltpu.VMEM((nd, 2, 2) + tile, _BF16),
            pltpu.VMEM((nd, 2, 2, 1, MIX_TQ), _F32),
            pltpu.VMEM((nd, 2, 1, MIX_TQ), _F32),
            pltpu.VMEM((nd, 2, LANES + DIFF_EXTRA, MIX_TQ), _F32),
            pltpu.VMEM((MIX_DIAG,) + tile, _F32),
            pltpu.VMEM((SB_SUB, 2 * SB_SUB), _BF16),
            pltpu.VMEM((N_KBLK, LANES, MIX_TK), _BF16),
            pltpu.VMEM((2, 2) + tile, _F32),
            pltpu.VMEM((2, 2, MIX_TK // SB_SUB, 2 * SB_SUB, MIX_TQ), _BF16),
            pltpu.VMEM((2, 2) + tile, _BF16),
            pltpu.VMEM((2, 1, MIX_TQ), _F32),
            pltpu.VMEM((2, HEAD_DIM, MIX_TQ), _F32),
            pltpu.VMEM((N_KBLK, LANES, MIX_TK), _BF16),
            pltpu.VMEM((2, BAND_TILES, ATT_T, ATT_T), _F32),
        ],
        compiler_params=pltpu.CompilerParams(
            dimension_semantics=("arbitrary", "arbitrary"), vmem_limit_bytes=MIX_VMEM_LIMIT),
        name="token_mixers",
    )(slopes, lam_vecs, subln_col, sb_col, ch_col, bias_vecs, *([qkv] * 12))


def _band_bias_vectors(rel_table):
    t = ATT_T
    shift = np.arange(2 * t) - t
    idx = np.stack([np.clip((BAND_TILES - 1 - kk) * t + shift, -(CHUNK - 1), MAX_REL) + (CHUNK - 1)
                    for kk in range(BAND_TILES)])
    vec = rel_table[:, idx] * LOG2E
    return vec.reshape(N_HEADS // 2, 2, BAND_TILES, 1, 2 * t)


def _build_band_bias(vec_ref, bias_scr):
    t = ATT_T
    key = lax.broadcasted_iota(jnp.int32, (t, t), 0)
    qry = lax.broadcasted_iota(jnp.int32, (t, t), 1)
    for kk in range(BAND_TILES):
        back = (BAND_TILES - 1 - kk) * t
        dc = back // CHUNK + qry // CHUNK - key // CHUNK
        in_band = (dc >= 0) & (dc <= LEFT_CHUNKS)
        for hh in range(2):
            rows = jnp.broadcast_to(vec_ref[hh, kk], (t, 2 * t))
            toeplitz = pltpu.roll(rows, 0, 1, stride=1, stride_axis=0)[:, t:]
            bias_scr[hh, kk] = jnp.where(in_band, toeplitz, -jnp.inf)


def _oproj_kernel(ya_ref, yb_ref, yc_ref, w_ref, x_ref, o_ref, w_scr):
    @pl.when(pl.program_id(1) == 0)
    def _():
        w_scr[...] = w_ref[...].astype(_BF16)

    y = jnp.concatenate([ya_ref[...], yb_ref[...], yc_ref[...]], axis=1)
    o_ref[...] = x_ref[...] + jnp.dot(y, w_scr[...], preferred_element_type=_F32)


def _out_proj(y_a, y_b, y_c, w_all, x, layer, tm=512, tn=D_MODEL):
    return pl.pallas_call(
        _oproj_kernel,
        grid=(D_MODEL // tn, SEQ // tm),
        in_specs=[
            pl.BlockSpec((tm, DIFF_WIDTH), lambda j, i: (i, 0)),
            pl.BlockSpec((tm, SB_WIDTH), lambda j, i: (i, 0)),
            pl.BlockSpec((tm, CH_WIDTH), lambda j, i: (i, 0)),
            pl.BlockSpec((None, D_MODEL, tn), lambda j, i: (layer, 0, j),
                         pipeline_mode=pl.Buffered(1)),
            pl.BlockSpec((tm, tn), lambda j, i: (i, j)),
        ],
        out_specs=pl.BlockSpec((tm, tn), lambda j, i: (i, j)),
        out_shape=jax.ShapeDtypeStruct((SEQ, D_MODEL), _F32),
        scratch_shapes=[pltpu.VMEM((D_MODEL, tn), _BF16)],
        compiler_params=_cparams(2),
        name="out_proj",
    )(y_a, y_b, y_c, w_all, x)


def _ffn_kernel(x_ref, halo_ref, g_ref, wg_ref, wu_ref, cw_ref, cb_ref, wd_ref, og_ref, o_ref,
                h_scr, gate_scr, *, tm, norm_output):
    i = pl.program_id(0)
    j = pl.program_id(1)

    def norm(x):
        ms = jnp.mean(x * x, axis=-1, keepdims=True)
        return (x * lax.rsqrt(ms + NORM_EPS) * g_ref[...]).astype(_BF16)

    @pl.when(j == 0)
    def _():
        x = x_ref[...]
        h_scr[HALO:, :] = norm(x)
        halo = jnp.where(i > 0, halo_ref[...], 0.0)
        h_scr[:HALO, :] = norm(halo)
        o_ref[...] = x

    gate_scr[...] = jnp.dot(h_scr[...], wg_ref[...], preferred_element_type=_F32)
    up = jnp.dot(h_scr[HALO:, :], wu_ref[...], preferred_element_type=_F32)
    cw = cw_ref[...]
    g = cb_ref[...]
    for t in range(CONV_WIDTH):
        lo = HALO - (CONV_WIDTH - 1) + t
        g = g + gate_scr[lo:lo + tm, :] * cw[t:t + 1]
    act = (g * jax.nn.sigmoid(g) * up).astype(_BF16)
    o_ref[...] += jnp.dot(act, wd_ref[...], preferred_element_type=_F32)

    if norm_output:
        @pl.when(j == pl.num_programs(1) - 1)
        def _():
            y = o_ref[...]
            ms = jnp.mean(y * y, axis=-1, keepdims=True)
            o_ref[...] = y * lax.rsqrt(ms + NORM_EPS) * og_ref[...]


def _ffn(x, g_all, wg_all, wu_all, cw_all, cb_all, wd_all, out_gain, layer, norm_output,
         tm=512, tf=512):
    halo_blocks = tm // HALO
    return pl.pallas_call(
        functools.partial(_ffn_kernel, tm=tm, norm_output=norm_output),
        grid=(SEQ // tm, D_FF // tf),
        in_specs=[
            pl.BlockSpec((tm, D_MODEL), lambda i, j: (i, 0)),
            pl.BlockSpec((HALO, D_MODEL),
                         lambda i, j: (jnp.maximum(i * halo_blocks - 1, 0), 0)),
            pl.BlockSpec((None, 1, D_MODEL), lambda i, j: (layer, 0, 0)),
            pl.BlockSpec((None, D_MODEL, tf), lambda i, j: (layer, 0, j)),
            pl.BlockSpec((None, D_MODEL, tf), lambda i, j: (layer, 0, j)),
            pl.BlockSpec((None, CONV_WIDTH, tf), lambda i, j: (layer, 0, j)),
            pl.BlockSpec((None, 1, tf), lambda i, j: (layer, 0, j)),
            pl.BlockSpec((None, tf, D_MODEL), lambda i, j: (layer, j, 0)),
            pl.BlockSpec((1, D_MODEL), lambda i, j: (0, 0)),
        ],
        out_specs=pl.BlockSpec((tm, D_MODEL), lambda i, j: (i, 0)),
        out_shape=jax.ShapeDtypeStruct((SEQ, D_MODEL), _F32),
        scratch_shapes=[
            pltpu.VMEM((tm + HALO, D_MODEL), _BF16),
            pltpu.VMEM((tm + HALO, tf), _F32),
        ],
        compiler_params=_cparams(2),
        name="gated_mlp",
    )(x, x, g_all, wg_all, wu_all, cw_all, cb_all, wd_all, out_gain)


def kernel(x, attn_norm, w_qkv, lambda_q1, lambda_k1, lambda_q2, lambda_k2, diff_subln,
           sb_norm, rel_bias, ch_norm, w_o, ffn_norm, w_gate, w_up, conv_w, conv_b, w_down,
           final_norm):
    batch, seq, d_model = x.shape
    assert (batch, seq, d_model) == (1, SEQ, D_MODEL)
    x2 = x.reshape(SEQ, D_MODEL)

    w_qkv_b = w_qkv.astype(_BF16)
    w_gate_b = w_gate.astype(_BF16)
    w_up_b = w_up.astype(_BF16)
    w_down_b = w_down.astype(_BF16)

    attn_g = attn_norm.reshape(DEPTH, 1, D_MODEL)
    ffn_g = ffn_norm.reshape(DEPTH, 1, D_MODEL)
    conv_b3 = conv_b.reshape(DEPTH, 1, D_FF)
    lam_vecs = jnp.stack([lambda_q1, lambda_k1, lambda_q2, lambda_k2], axis=1)
    subln_col = diff_subln.reshape(DEPTH, 2 * HEAD_DIM, 1)
    sb_col = sb_norm.reshape(DEPTH, N_HEADS // 2, 2 * HEAD_DIM, 1)
    ch_col = ch_norm.reshape(DEPTH, N_HEADS // 2, 2 * HEAD_DIM, 1)
    slopes = _alibi_slope_table()
    col_scale = jnp.asarray(_qkv_col_scale())

    for layer in range(DEPTH):
        lam_init = 0.8 - 0.6 * math.exp(-0.3 * layer)
        qkv = _qkv_proj(x2, attn_g, w_qkv_b, col_scale, layer)
        y_a, y_b, y_c = _mix_attention(qkv, slopes, lam_vecs, subln_col, sb_col, ch_col,
                                       _band_bias_vectors(rel_bias[layer]), layer, lam_init)
        x2 = _out_proj(y_a, y_b, y_c, w_o, x2, layer)
        x2 = _ffn(x2, ffn_g, w_gate_b, w_up_b, conv_w, conv_b3, w_down_b,
                  final_norm.reshape(1, D_MODEL), layer, norm_output=layer == DEPTH - 1)
    return x2.reshape(1, SEQ, D_MODEL)
```

```python
import functools
import math

import jax
import jax.numpy as jnp
import numpy as np
from jax import lax
from jax.experimental import pallas as pl
from jax.experimental.pallas import tpu as pltpu

D_MODEL = 2048
SEQ = 8192
DEPTH = 4
CHUNK = 64
HEAD_DIM = 64
N_HEADS = 8
DIFF_WIDTH = 1024
SB_WIDTH = 512
CH_WIDTH = 512
QKV_WIDTH = 6144
LEFT_CHUNKS = 8
MAX_REL = 128
D_FF = 5632
CONV_WIDTH = 3
NORM_EPS = 1e-6
SUBLN_EPS = 1e-5
SCALE = HEAD_DIM ** -0.5
LOG2E = math.log2(math.e)

LANES = 128
ATT_T = 256
N_QBLK = SEQ // ATT_T
HALO = 16
VMEM_LIMIT = 48 * 1024 * 1024
BIG_VMEM_LIMIT = 56 * 1024 * 1024
MIX_VMEM_LIMIT = BIG_VMEM_LIMIT

QA, KA, VA = 0, 8, 16
QB, KB, VB = 24, 28, 32
QC, KC, VC = 36, 40, 44

_F32 = jnp.float32
_BF16 = jnp.bfloat16
_NEG = -1e30
_SIGN_BIT = np.int32(-2 ** 31)


def _cparams(n_axes, vmem_limit=VMEM_LIMIT):
    return pltpu.CompilerParams(
        dimension_semantics=("arbitrary",) * n_axes,
        vmem_limit_bytes=vmem_limit)


def _qkv_kernel(x_ref, g_ref, w_ref, cs_ref, o_ref, h_scr):
    @pl.when(pl.program_id(1) == 0)
    def _():
        x = x_ref[...]
        ms = jnp.mean(x * x, axis=-1, keepdims=True)
        h_scr[...] = (x * lax.rsqrt(ms + NORM_EPS) * g_ref[...]).astype(_BF16)

    acc = jnp.dot(h_scr[...], w_ref[...], preferred_element_type=_F32)
    o_ref[...] = (acc * cs_ref[...]).astype(_BF16)


def _qkv_col_scale():
    cs = np.ones((1, QKV_WIDTH), np.float32)
    for q_start, k_start in ((QA, KA), (QB, KB), (QC, KC)):
        cs[:, q_start * LANES:k_start * LANES] = SCALE * LOG2E
    return cs


def _qkv_proj(x, g_all, w_all, col_scale, layer, tm=1024, tn=1024):
    return pl.pallas_call(
        _qkv_kernel,
        grid=(SEQ // tm, QKV_WIDTH // tn),
        in_specs=[
            pl.BlockSpec((tm, D_MODEL), lambda i, j: (i, 0)),
            pl.BlockSpec((None, 1, D_MODEL), lambda i, j: (layer, 0, 0)),
            pl.BlockSpec((None, D_MODEL, tn), lambda i, j: (layer, 0, j)),
            pl.BlockSpec((1, tn), lambda i, j: (0, j)),
        ],
        out_specs=pl.BlockSpec((tm, tn), lambda i, j: (i, j)),
        out_shape=jax.ShapeDtypeStruct((SEQ, QKV_WIDTH), _BF16),
        scratch_shapes=[pltpu.VMEM((tm, D_MODEL), _BF16)],
        compiler_params=_cparams(2),
        name="qkv_proj",
    )(x, g_all, w_all, col_scale)


def _split_heads_t(q_ref):
    q_t = q_ref[...].T
    dim = lax.broadcasted_iota(jnp.int32, q_t.shape, 0)
    zero = jnp.zeros_like(q_t)
    return jnp.where(dim < HEAD_DIM, q_t, zero), jnp.where(dim >= HEAD_DIM, q_t, zero)


def _key_block(ref, j):
    return ref[pl.ds(pl.multiple_of(j * ATT_T, ATT_T), ATT_T), :]


def _pair_rmsnorm(o_t, gain_col):
    top, bot = o_t[:HEAD_DIM], o_t[HEAD_DIM:]
    top = top * lax.rsqrt(jnp.mean(top * top, axis=0, keepdims=True) + NORM_EPS)
    bot = bot * lax.rsqrt(jnp.mean(bot * bot, axis=0, keepdims=True) + NORM_EPS)
    return (jnp.concatenate([top, bot], axis=0) * gain_col).T


MIX_TQ = 512
MIX_TK = 256
MIX_DIAG = MIX_TQ // MIX_TK
assert MIX_DIAG == 2
N_KBLK = SEQ // MIX_TK
DIFF_PER_STEP = 2
DIFF_SPLIT = 3
DIFF_EXTRA = 16
SB_SUB = 128
PIPE_PERIOD = 4
SCORE_SLOTS = 2
assert PIPE_PERIOD % SCORE_SLOTS == 0 and PIPE_PERIOD % 2 == 0


def _visible_queries(diagonal):
    return slice((diagonal or 0) * MIX_TK, MIX_TQ)


def _transpose_blocks(v_ref, vt_ref, rows=None):
    def body(b, carry):
        blk = v_ref[pl.ds(pl.multiple_of(b * MIX_TK, MIX_TK), MIX_TK), :].T
        if rows is None:
            vt_ref[b] = blk
        else:
            vt_ref[b, :LANES] = blk
            vt_ref[b, LANES:] = rows
        return carry

    lax.fori_loop(0, N_KBLK, body, 0)


class _DiffStream:
    def __init__(self, i, slope_row, q_ref, k_ref, v_ref, diag, vt, s, mx, p, al, m, acc):
        self.i, self.k_ref, self.v_ref = i, k_ref, v_ref
        self.diag, self.vt, self.s, self.mx, self.p, self.al, self.m, self.acc = (
            diag, vt, s, mx, p, al, m, acc)
        self.slope2 = slope_row(0)
        dim = lax.broadcasted_iota(jnp.int32, (LANES, MIX_TQ), 0)
        slope_rows = functools.reduce(
            lambda a, n: jnp.where(dim == n, slope_row(1 + n), a), range(DIFF_SPLIT),
            jnp.zeros((LANES, MIX_TQ), _F32)).astype(_BF16)
        self.q_maps = tuple(jnp.concatenate([q_half, slope_rows], axis=0)
                            for q_half in _split_heads_t(q_ref))
        lane = lax.broadcasted_iota(jnp.int32, (MIX_TK, LANES), 1)
        key_offset = lax.broadcasted_iota(jnp.int32, (MIX_TK, LANES), 0).astype(_F32)
        self.key_feat = jnp.where(lane < DIFF_SPLIT, key_offset, 0.0).astype(_BF16)

    def init_head(self):
        row = lax.broadcasted_iota(jnp.int32, (MIX_TK, MIX_TQ), 0)
        col = lax.broadcasted_iota(jnp.int32, (MIX_TK, MIX_TQ), 1)
        for d in range(MIX_DIAG):
            key = row + d * MIX_TK
            allowed = (key // CHUNK) <= (col // CHUNK)
            lead = (col - row).astype(_F32) - jnp.abs(col - key).astype(_F32)
            self.diag[d] = jnp.where(allowed, self.slope2 * lead, -jnp.inf)
        rows = lax.broadcasted_iota(jnp.int32, (DIFF_EXTRA, MIX_TK), 0)
        _transpose_blocks(self.v_ref, self.vt, jnp.where(rows == 0, 1.0, 0.0).astype(_BF16))

    def init_tile(self):
        self.m[...] = jnp.full(self.m.shape, _NEG, _F32)
        self.acc[...] = jnp.zeros(self.acc.shape, _F32)

    def _offset(self, t, diagonal):
        if diagonal is not None:
            return jnp.zeros((), _F32)
        return -self.slope2 * (self.i * MIX_TQ - t * MIX_TK).astype(_F32)

    def stage_a(self, t, ts, diagonal=None):
        slot, q = ts % SCORE_SLOTS, _visible_queries(diagonal)
        kb = self.k_ref[pl.ds(pl.multiple_of(t * MIX_TK, MIX_TK), MIX_TK), :]
        kx = jnp.concatenate([kb, self.key_feat], axis=1)
        offset = self._offset(t, diagonal)
        for c in range(2):
            s = jnp.dot(kx, self.q_maps[c][:, q], preferred_element_type=_F32)
            if diagonal is not None:
                s = s + self.diag[diagonal, :, q]
            self.s[slot, c, :, q] = s
            self.mx[slot, c, :, q] = jnp.max(s, axis=0, keepdims=True) + offset

    def stage_b(self, t, ts, diagonal=None):
        slot, out, q = ts % SCORE_SLOTS, ts % 2, _visible_queries(diagonal)
        offset = self._offset(t, diagonal)
        for c in range(2):
            m_old = self.m[c, :, q]
            m_new = jnp.maximum(m_old, self.mx[slot, c, :, q])
            self.p[out, c, :, q] = jnp.exp2(self.s[slot, c, :, q] - (m_new - offset)).astype(_BF16)
            self.al[out, c, :, q] = jnp.exp2(m_old - m_new)
            self.m[c, :, q] = m_new

    def stage_c(self, t, ts, diagonal=None):
        slot, q = ts % 2, _visible_queries(diagonal)
        vt = self.vt[t]
        for c in range(2):
            pv = jnp.dot(vt, self.p[slot, c, :, q], preferred_element_type=_F32)
            self.acc[c, :, q] = self.al[slot, c, :, q] * self.acc[c, :, q] + pv

    def result(self, lam, gain_col, lam_init):
        acc0, acc1 = self.acc[0], self.acc[1]
        o_t = (acc0[:LANES] / acc0[LANES:LANES + 1]
               - lam * (acc1[:LANES] / acc1[LANES:LANES + 1]))
        ms = jnp.mean(o_t * o_t, axis=0, keepdims=True)
        return (o_t * lax.rsqrt(ms + SUBLN_EPS) * gain_col * (1.0 - lam_init)).T


class _SbStream:
    def __init__(self, nb, q_ref, k_ref, v_ref, mask, tri, vt, z, hl, a, r, acc):
        self.nb, self.k_ref, self.v_ref = nb, k_ref, v_ref
        self.mask, self.tri, self.vt, self.z, self.hl, self.a, self.r, self.acc = (
            mask, tri, vt, z, hl, a, r, acc)
        self.q_heads = _split_heads_t(q_ref)

    def init_head(self):
        row = lax.broadcasted_iota(jnp.int32, (MIX_TK, MIX_TQ), 0)
        col = lax.broadcasted_iota(jnp.int32, (MIX_TK, MIX_TQ), 1)
        for d in range(MIX_DIAG):
            self.mask[d] = jnp.where(row + d * MIX_TK < col, 0.0, -jnp.inf)
        r2 = lax.broadcasted_iota(jnp.int32, (SB_SUB, 2 * SB_SUB), 0)
        c2 = lax.broadcasted_iota(jnp.int32, (SB_SUB, 2 * SB_SUB), 1) % SB_SUB
        self.tri[...] = jnp.where(c2 >= r2, 1.0, 0.0).astype(_BF16)
        _transpose_blocks(self.v_ref, self.vt)

    def init_tile(self):
        self.r[...] = jnp.zeros(self.r.shape, _F32)
        self.acc[...] = jnp.zeros(self.acc.shape, _F32)

    def stage_a(self, t, ts, diagonal=None):
        slot = ts % 2
        block = None if diagonal is None else MIX_DIAG - 1 - diagonal
        q = _visible_queries(block)
        start = pl.multiple_of((self.nb - 1 - t) * MIX_TK, MIX_TK)
        kb = self.k_ref[pl.ds(start, MIX_TK), :]
        for hh in range(2):
            z = jnp.dot(kb, self.q_heads[hh][:, q], preferred_element_type=_F32)
            if block is not None:
                z = z + self.mask[block, :, q]
            neg_abs = pltpu.bitcast(pltpu.bitcast(z, jnp.int32) | _SIGN_BIT, _F32)
            nl = jnp.maximum(z, 0.0) + jnp.log2(1.0 + jnp.exp2(neg_abs))
            hi = nl.astype(_BF16)
            lo = (nl - hi.astype(_F32)).astype(_BF16)
            self.z[slot, hh, :, q] = z
            for sub in range(MIX_TK // SB_SUB):
                rows = slice(sub * SB_SUB, (sub + 1) * SB_SUB)
                self.hl[slot, hh, sub, :SB_SUB, q] = hi[rows]
                self.hl[slot, hh, sub, SB_SUB:, q] = lo[rows]

    def stage_b(self, t, ts, diagonal=None):
        del t
        slot = ts % 2
        block = None if diagonal is None else MIX_DIAG - 1 - diagonal
        q = _visible_queries(block)
        tri = self.tri[...]
        for hh in range(2):
            if q.start:
                self.a[slot, hh, :, :q.start] = jnp.zeros((MIX_TK, q.start), _BF16)
            right = self.r[hh, :, q]
            for sub in reversed(range(MIX_TK // SB_SUB)):
                rows = slice(sub * SB_SUB, (sub + 1) * SB_SUB)
                incl = jnp.dot(tri, self.hl[slot, hh, sub, :, q], preferred_element_type=_F32)
                self.a[slot, hh, rows, q] = jnp.exp2(
                    self.z[slot, hh, rows, q] - incl - right).astype(_BF16)
                right = right + incl[0:1]
            self.r[hh, :, q] = right

    def stage_c(self, t, ts, diagonal=None):
        del diagonal
        slot = ts % 2
        vt = self.vt[self.nb - 1 - t]
        for hh in range(2):
            self.acc[hh] += jnp.dot(vt[hh * HEAD_DIM:(hh + 1) * HEAD_DIM], self.a[slot, hh],
                                    preferred_element_type=_F32)

    def result(self, gain_col):
        o_t = jnp.concatenate([self.acc[0], self.acc[1]], axis=0)
        return _pair_rmsnorm(o_t, gain_col)


BAND_TILES = 1 + (LEFT_CHUNKS * CHUNK) // ATT_T
BAND_SUBS = MIX_TQ // ATT_T
assert ATT_T == MIX_TK


def _band_sub_tile(i, sub, q_heads, bias_ref, k_ref, vt_ref):
    first = i * BAND_SUBS + sub - (BAND_TILES - 1)
    tiles = [jnp.maximum(first + kk, 0) for kk in range(BAND_TILES)]
    heads = []
    for hh in range(2):
        q_sub = q_heads[hh][:, sub * ATT_T:(sub + 1) * ATT_T]
        scores = []
        for kk in range(BAND_TILES):
            s = jnp.dot(_key_block(k_ref, tiles[kk]), q_sub,
                        preferred_element_type=_F32) + bias_ref[hh, kk]
            scores.append(jnp.where(first + kk >= 0, s, -jnp.inf))
        m = functools.reduce(jnp.maximum, [jnp.max(s, axis=0, keepdims=True) for s in scores])
        probs = [jnp.exp2(s - m) for s in scores]
        denom = functools.reduce(jnp.add, [jnp.sum(p, axis=0, keepdims=True) for p in probs])
        o_t = functools.reduce(jnp.add, [
            jnp.dot(vt_ref[tiles[kk]][hh * HEAD_DIM:(hh + 1) * HEAD_DIM],
                    probs[kk].astype(_BF16), preferred_element_type=_F32)
            for kk in range(BAND_TILES)])
        heads.append(o_t / denom)
    return jnp.concatenate(heads, axis=0)


def _mix_kernel(slope_ref, lam_ref, subln_ref, sbgain_ref, chgain_ref, biasvec_ref,
                dq0_ref, dq1_ref, dk0_ref, dk1_ref, dv0_ref, dv1_ref, sq_ref, sk_ref, sv_ref,
                cq_ref, ck_ref, cv_ref,
                ya_ref, yb_ref, yc_ref,
                diag_scr, dvt_scr, s_scr, mx_scr, p_scr, al_scr, m_scr, dacc_scr,
                mask_scr, tri_scr, svt_scr, z_scr, hl_scr, a_scr, r_scr, sacc_scr, cvt_scr, cbias_scr,
                *, lam_init):
    g = pl.program_id(0)
    i = pl.program_id(1)
    nb = (i + 1) * MIX_DIAG

    diffs = []
    for hd, (q_ref, k_ref, v_ref) in enumerate(
            ((dq0_ref, dk0_ref, dv0_ref), (dq1_ref, dk1_ref, dv1_ref))):
        head = g * DIFF_PER_STEP + hd
        diffs.append(_DiffStream(
            i, functools.partial(lambda n, head: slope_ref[head, n], head=head),
            q_ref, k_ref, v_ref, diag_scr.at[hd], dvt_scr.at[hd], s_scr.at[hd], mx_scr.at[hd],
            p_scr.at[hd], al_scr.at[hd], m_scr.at[hd], dacc_scr.at[hd]))
    sb = _SbStream(nb, sq_ref, sk_ref, sv_ref, mask_scr, tri_scr, svt_scr, z_scr, hl_scr, a_scr,
                   r_scr, sacc_scr)
    streams = diffs + [sb]

    @pl.when(i == 0)
    def _():
        for stream in streams:
            stream.init_head()
        _transpose_blocks(cv_ref, cvt_scr)
        _build_band_bias(biasvec_ref, cbias_scr)

    for stream in streams:
        stream.init_tile()

    band_q = _split_heads_t(cq_ref)

    def band(sub):
        o_t = _band_sub_tile(i, sub, band_q, cbias_scr, ck_ref, cvt_scr)
        yc_ref[sub * ATT_T:(sub + 1) * ATT_T, :] = _pair_rmsnorm(
            o_t, chgain_ref[...]).astype(yc_ref.dtype)

    def prologue(first_tile):
        assert BAND_SUBS == 2
        for step in range(2):
            sb.stage_a(step, step, diagonal=step)
            for d in diffs:
                d.stage_a(step, step, diagonal=step if first_tile else None)
            band(step)
            if step == 0:
                sb.stage_b(0, 0, diagonal=0)
                for d in diffs:
                    d.stage_b(0, 0, diagonal=0 if first_tile else None)

    def pipeline_step(t, ts, a_diag=None, b_diag=None):
        for s in streams:
            s.stage_c(t - 2, (ts - 2) % PIPE_PERIOD)
        sb.stage_b(t - 1, (ts - 1) % PIPE_PERIOD)
        for d in diffs:
            d.stage_b(t - 1, (ts - 1) % PIPE_PERIOD, diagonal=b_diag)
        sb.stage_a(t, ts)
        for d in diffs:
            d.stage_a(t, ts, diagonal=a_diag)

    def two_steps(t, ts, last=False):
        pipeline_step(t, ts, a_diag=0 if last else None)
        pipeline_step(t + 1, ts + 1, a_diag=1 if last else None, b_diag=0 if last else None)

    def drain(ts_end):
        for s in streams:
            s.stage_c(nb - 2, (ts_end - 1) % PIPE_PERIOD)
        sb.stage_b(nb - 1, ts_end)
        for d in diffs:
            d.stage_b(nb - 1, ts_end, diagonal=MIX_DIAG - 1)
        sb.stage_c(nb - 1, ts_end)
        for d in diffs:
            d.stage_c(nb - 1, ts_end, diagonal=MIX_DIAG - 1)

    @pl.when(i == 0)
    def _():
        prologue(first_tile=True)
        drain(1)

    @pl.when(i > 0)
    def _():
        prologue(first_tile=False)

        def body(u, carry):
            t = 2 + 4 * u
            two_steps(t, 2)
            two_steps(t + 2, 0)
            return carry

        lax.fori_loop(0, (i - 1) // 2, body, 0)

        @pl.when(i % 2 == 0)
        def _():
            two_steps(nb - 4, 2)
            two_steps(nb - 2, 0, last=True)
            drain(1)

        @pl.when(i % 2 == 1)
        def _():
            two_steps(nb - 2, 2, last=True)
            drain(3)

    lam_v = lam_ref[...]
    lam = (jnp.exp(jnp.sum(lam_v[0:1] * lam_v[1:2], axis=-1, keepdims=True))
           - jnp.exp(jnp.sum(lam_v[2:3] * lam_v[3:4], axis=-1, keepdims=True))
           + lam_init)
    for hd, d in enumerate(diffs):
        ya_ref[:, hd * LANES:(hd + 1) * LANES] = d.result(
            lam, subln_ref[...], lam_init).astype(ya_ref.dtype)
    yb_ref[...] = sb.result(sbgain_ref[...]).astype(yb_ref.dtype)


def _alibi_slope_table():
    slope = 2.0 ** (-8.0 * np.arange(1, N_HEADS + 1, dtype=np.float32) / N_HEADS)
    slope2 = (slope * np.float32(LOG2E)).astype(np.float32)
    cols, rest = [slope2], slope2
    for _ in range(DIFF_SPLIT):
        term = rest.astype(_BF16).astype(np.float32)
        cols.append(term)
        rest = rest - term
    return jnp.asarray(np.stack(cols, axis=1))


def _mix_attention(qkv, slopes, lam_vecs, subln_col, sb_col, ch_col, bias_vecs, layer, lam_init):
    tile = (MIX_TK, MIX_TQ)
    nd = DIFF_PER_STEP
    once = pl.Buffered(1)

    def q_spec(col):
        return pl.BlockSpec((MIX_TQ, LANES), lambda g, i: (i, col(g)))

    def kv_spec(col):
        return pl.BlockSpec((SEQ, LANES), lambda g, i: (0, col(g)), pipeline_mode=once)

    def pair_gain_spec():
        return pl.BlockSpec((None, None, LANES, 1), lambda g, i: (layer, g, 0, 0))

    return pl.pallas_call(
        functools.partial(_mix_kernel, lam_init=lam_init),
        grid=(N_HEADS // nd, SEQ // MIX_TQ),
        in_specs=[
            pl.BlockSpec(memory_space=pltpu.SMEM),
            pl.BlockSpec((None, 4, HEAD_DIM), lambda g, i: (layer, 0, 0)),
            pl.BlockSpec((None, LANES, 1), lambda g, i: (layer, 0, 0)),
            pair_gain_spec(), pair_gain_spec(),
            pl.BlockSpec((None, 2, BAND_TILES, 1, 2 * ATT_T), lambda g, i: (g, 0, 0, 0, 0)),
            q_spec(lambda g: QA + nd * g), q_spec(lambda g: QA + nd * g + 1),
            kv_spec(lambda g: KA + nd * g), kv_spec(lambda g: KA + nd * g + 1),
            kv_spec(lambda g: VA + nd * g), kv_spec(lambda g: VA + nd * g + 1),
            q_spec(lambda g: QB + g), kv_spec(lambda g: KB + g), kv_spec(lambda g: VB + g),
            q_spec(lambda g: QC + g), kv_spec(lambda g: KC + g), kv_spec(lambda g: VC + g),
        ],
        out_specs=[
            pl.BlockSpec((MIX_TQ, nd * LANES), lambda g, i: (i, g)),
            pl.BlockSpec((MIX_TQ, LANES), lambda g, i: (i, g)),
            pl.BlockSpec((MIX_TQ, LANES), lambda g, i: (i, g)),
        ],
        out_shape=[
            jax.ShapeDtypeStruct((SEQ, DIFF_WIDTH), _BF16),
            jax.ShapeDtypeStruct((SEQ, SB_WIDTH), _BF16),
            jax.ShapeDtypeStruct((SEQ, CH_WIDTH), _BF16),
        ],
        scratch_shapes=[
            pltpu.VMEM((nd, MIX_DIAG) + tile, _F32),
            pltpu.VMEM((nd, N_KBLK, LANES + DIFF_EXTRA, MIX_TK), _BF16),
            pltpu.VMEM((nd, SCORE_SLOTS, 2) + tile, _F32),
            pltpu.VMEM((nd, SCORE_SLOTS, 2, 1, MIX_TQ), _F32),
            pltpu.VMEM((nd, 2, 2) + tile, _BF16),
            pltpu.VMEM((nd, 2, 2, 1, MIX_TQ), _F32),
            pltpu.VMEM((nd, 2, 1, MIX_TQ), _F32),
            pltpu.VMEM((nd, 2, LANES + DIFF_EXTRA, MIX_TQ), _F32),
            pltpu.VMEM((MIX_DIAG,) + tile, _F32),
            pltpu.VMEM((SB_SUB, 2 * SB_SUB), _BF16),
            pltpu.VMEM((N_KBLK, LANES, MIX_TK), _BF16),
            pltpu.VMEM((2, 2) + tile, _F32),
            pltpu.VMEM((2, 2, MIX_TK // SB_SUB, 2 * SB_SUB, MIX_TQ), _BF16),
            pltpu.VMEM((2, 2) + tile, _BF16),
            pltpu.VMEM((2, 1, MIX_TQ), _F32),
            pltpu.VMEM((2, HEAD_DIM, MIX_TQ), _F32),
            pltpu.VMEM((N_KBLK, LANES, MIX_TK), _BF16),
            pltpu.VMEM((2, BAND_TILES, ATT_T, ATT_T), _F32),
        ],
        compiler_params=pltpu.CompilerParams(
            dimension_semantics=("arbitrary", "arbitrary"), vmem_limit_bytes=MIX_VMEM_LIMIT),
        name="token_mixers",
    )(slopes, lam_vecs, subln_col, sb_col, ch_col, bias_vecs, *([qkv] * 12))


def _band_bias_vectors(rel_table):
    t = ATT_T
    shift = np.arange(2 * t) - t
    idx = np.stack([np.clip((BAND_TILES - 1 - kk) * t + shift, -(CHUNK - 1), MAX_REL) + (CHUNK - 1)
                    for kk in range(BAND_TILES)])
    vec = rel_table[:, idx] * LOG2E
    return vec.reshape(N_HEADS // 2, 2, BAND_TILES, 1, 2 * t)


def _build_band_bias(vec_ref, bias_scr):
    t = ATT_T
    key = lax.broadcasted_iota(jnp.int32, (t, t), 0)
    qry = lax.broadcasted_iota(jnp.int32, (t, t), 1)
    for kk in range(BAND_TILES):
        back = (BAND_TILES - 1 - kk) * t
        dc = back // CHUNK + qry // CHUNK - key // CHUNK
        in_band = (dc >= 0) & (dc <= LEFT_CHUNKS)
        for hh in range(2):
            rows = jnp.broadcast_to(vec_ref[hh, kk], (t, 2 * t))
            toeplitz = pltpu.roll(rows, 0, 1, stride=1, stride_axis=0)[:, t:]
            bias_scr[hh, kk] = jnp.where(in_band, toeplitz, -jnp.inf)


def _oproj_kernel(ya_ref, yb_ref, yc_ref, w_ref, x_ref, o_ref, w_scr):
    @pl.when(pl.program_id(1) == 0)
    def _():
        w_scr[...] = w_ref[...].astype(_BF16)

    y = jnp.concatenate([ya_ref[...], yb_ref[...], yc_ref[...]], axis=1)
    o_ref[...] = x_ref[...] + jnp.dot(y, w_scr[...], preferred_element_type=_F32)


def _out_proj(y_a, y_b, y_c, w_all, x, layer, tm=512, tn=D_MODEL):
    return pl.pallas_call(
        _oproj_kernel,
        grid=(D_MODEL // tn, SEQ // tm),
        in_specs=[
            pl.BlockSpec((tm, DIFF_WIDTH), lambda j, i: (i, 0)),
            pl.BlockSpec((tm, SB_WIDTH), lambda j, i: (i, 0)),
            pl.BlockSpec((tm, CH_WIDTH), lambda j, i: (i, 0)),
            pl.BlockSpec((None, D_MODEL, tn), lambda j, i: (layer, 0, j),
                         pipeline_mode=pl.Buffered(1)),
            pl.BlockSpec((tm, tn), lambda j, i: (i, j)),
        ],
        out_specs=pl.BlockSpec((tm, tn), lambda j, i: (i, j)),
        out_shape=jax.ShapeDtypeStruct((SEQ, D_MODEL), _F32),
        scratch_shapes=[pltpu.VMEM((D_MODEL, tn), _BF16)],
        compiler_params=_cparams(2),
        name="out_proj",
    )(y_a, y_b, y_c, w_all, x)


def _ffn_kernel(x_ref, halo_ref, g_ref, wg_ref, wu_ref, cw_ref, cb_ref, wd_ref, og_ref, o_ref,
                h_scr, gate_scr, *, tm, norm_output):
    i = pl.program_id(0)
    j = pl.program_id(1)

    def norm(x):
        ms = jnp.mean(x * x, axis=-1, keepdims=True)
        return (x * lax.rsqrt(ms + NORM_EPS) * g_ref[...]).astype(_BF16)

    @pl.when(j == 0)
    def _():
        x = x_ref[...]
        h_scr[HALO:, :] = norm(x)
        halo = jnp.where(i > 0, halo_ref[...], 0.0)
        h_scr[:HALO, :] = norm(halo)
        o_ref[...] = x

    gate_scr[...] = jnp.dot(h_scr[...], wg_ref[...], preferred_element_type=_F32)
    up = jnp.dot(h_scr[HALO:, :], wu_ref[...], preferred_element_type=_F32)
    cw = cw_ref[...]
    g = cb_ref[...]
    for t in range(CONV_WIDTH):
        lo = HALO - (CONV_WIDTH - 1) + t
        g = g + gate_scr[lo:lo + tm, :] * cw[t:t + 1]
    act = (g * jax.nn.sigmoid(g) * up).astype(_BF16)
    o_ref[...] += jnp.dot(act, wd_ref[...], preferred_element_type=_F32)

    if norm_output:
        @pl.when(j == pl.num_programs(1) - 1)
        def _():
            y = o_ref[...]
            ms = jnp.mean(y * y, axis=-1, keepdims=True)
            o_ref[...] = y * lax.rsqrt(ms + NORM_EPS) * og_ref[...]


def _ffn(x, g_all, wg_all, wu_all, cw_all, cb_all, wd_all, out_gain, layer, norm_output,
         tm=1024, tf=512):
    halo_blocks = tm // HALO
    return pl.pallas_call(
        functools.partial(_ffn_kernel, tm=tm, norm_output=norm_output),
        grid=(SEQ // tm, D_FF // tf),
        in_specs=[
            pl.BlockSpec((tm, D_MODEL), lambda i, j: (i, 0)),
            pl.BlockSpec((HALO, D_MODEL),
                         lambda i, j: (jnp.maximum(i * halo_blocks - 1, 0), 0)),
            pl.BlockSpec((None, 1, D_MODEL), lambda i, j: (layer, 0, 0)),
            pl.BlockSpec((None, D_MODEL, tf), lambda i, j: (layer, 0, j)),
            pl.BlockSpec((None, D_MODEL, tf), lambda i, j: (layer, 0, j)),
            pl.BlockSpec((None, CONV_WIDTH, tf), lambda i, j: (layer, 0, j)),
            pl.BlockSpec((None, 1, tf), lambda i, j: (layer, 0, j)),
            pl.BlockSpec((None, tf, D_MODEL), lambda i, j: (layer, j, 0)),
            pl.BlockSpec((1, D_MODEL), lambda i, j: (0, 0)),
        ],
        out_specs=pl.BlockSpec((tm, D_MODEL), lambda i, j: (i, 0)),
        out_shape=jax.ShapeDtypeStruct((SEQ, D_MODEL), _F32),
        scratch_shapes=[
            pltpu.VMEM((tm + HALO, D_MODEL), _BF16),
            pltpu.VMEM((tm + HALO, tf), _F32),
        ],
        compiler_params=_cparams(2, BIG_VMEM_LIMIT),
        name="gated_mlp",
    )(x, x, g_all, wg_all, wu_all, cw_all, cb_all, wd_all, out_gain)


def kernel(x, attn_norm, w_qkv, lambda_q1, lambda_k1, lambda_q2, lambda_k2, diff_subln,
           sb_norm, rel_bias, ch_norm, w_o, ffn_norm, w_gate, w_up, conv_w, conv_b, w_down,
           final_norm):
    batch, seq, d_model = x.shape
    assert (batch, seq, d_model) == (1, SEQ, D_MODEL)
    x2 = x.reshape(SEQ, D_MODEL)

    w_qkv_b = w_qkv.astype(_BF16)
    w_gate_b = w_gate.astype(_BF16)
    w_up_b = w_up.astype(_BF16)
    w_down_b = w_down.astype(_BF16)

    attn_g = attn_norm.reshape(DEPTH, 1, D_MODEL)
    ffn_g = ffn_norm.reshape(DEPTH, 1, D_MODEL)
    conv_b3 = conv_b.reshape(DEPTH, 1, D_FF)
    lam_vecs = jnp.stack([lambda_q1, lambda_k1, lambda_q2, lambda_k2], axis=1)
    subln_col = diff_subln.reshape(DEPTH, 2 * HEAD_DIM, 1)
    sb_col = sb_norm.reshape(DEPTH, N_HEADS // 2, 2 * HEAD_DIM, 1)
    ch_col = ch_norm.reshape(DEPTH, N_HEADS // 2, 2 * HEAD_DIM, 1)
    slopes = _alibi_slope_table()
    col_scale = jnp.asarray(_qkv_col_scale())

    for layer in range(DEPTH):
        lam_init = 0.8 - 0.6 * math.exp(-0.3 * layer)
        qkv = _qkv_proj(x2, attn_g, w_qkv_b, col_scale, layer)
        y_a, y_b, y_c = _mix_attention(qkv, slopes, lam_vecs, subln_col, sb_col, ch_col,
                                       _band_bias_vectors(rel_bias[layer]), layer, lam_init)
        x2 = _out_proj(y_a, y_b, y_c, w_o, x2, layer)
        x2 = _ffn(x2, ffn_g, w_gate_b, w_up_b, conv_w, conv_b3, w_down_b,
                  final_norm.reshape(1, D_MODEL), layer, norm_output=layer == DEPTH - 1)
    return x2.reshape(1, SEQ, D_MODEL)
```

```python
import functools
import math

import jax
import jax.numpy as jnp
import numpy as np
from jax import lax
from jax.experimental import pallas as pl
from jax.experimental.pallas import tpu as pltpu

D_MODEL = 2048
SEQ = 8192
DEPTH = 4
CHUNK = 64
HEAD_DIM = 64
N_HEADS = 8
DIFF_WIDTH = 1024
SB_WIDTH = 512
CH_WIDTH = 512
QKV_WIDTH = 6144
LEFT_CHUNKS = 8
MAX_REL = 128
D_FF = 5632
CONV_WIDTH = 3
NORM_EPS = 1e-6
SUBLN_EPS = 1e-5
SCALE = HEAD_DIM ** -0.5
LOG2E = math.log2(math.e)

LANES = 128
ATT_T = 256
N_QBLK = SEQ // ATT_T
HALO = 16
VMEM_LIMIT = 48 * 1024 * 1024
BIG_VMEM_LIMIT = 56 * 1024 * 1024
MIX_VMEM_LIMIT = BIG_VMEM_LIMIT

QA, KA, VA = 0, 8, 16
QB, KB, VB = 24, 28, 32
QC, KC, VC = 36, 40, 44

_F32 = jnp.float32
_BF16 = jnp.bfloat16
_NEG = -1e30
_SIGN_BIT = np.int32(-2 ** 31)


def _cparams(n_axes, vmem_limit=VMEM_LIMIT):
    return pltpu.CompilerParams(
        dimension_semantics=("arbitrary",) * n_axes,
        vmem_limit_bytes=vmem_limit)


def _qkv_kernel(x_ref, g_ref, w_ref, cs_ref, o_ref, h_scr):
    @pl.when(pl.program_id(1) == 0)
    def _():
        x = x_ref[...]
        ms = jnp.mean(x * x, axis=-1, keepdims=True)
        h_scr[...] = (x * lax.rsqrt(ms + NORM_EPS) * g_ref[...]).astype(_BF16)

    acc = jnp.dot(h_scr[...], w_ref[...], preferred_element_type=_F32)
    o_ref[...] = (acc * cs_ref[...]).astype(_BF16)


def _qkv_col_scale():
    cs = np.ones((1, QKV_WIDTH), np.float32)
    for q_start, k_start in ((QA, KA), (QB, KB), (QC, KC)):
        cs[:, q_start * LANES:k_start * LANES] = SCALE * LOG2E
    return cs


def _qkv_proj(x, g_all, w_all, col_scale, layer, tm=1024, tn=1024):
    return pl.pallas_call(
        _qkv_kernel,
        grid=(SEQ // tm, QKV_WIDTH // tn),
        in_specs=[
            pl.BlockSpec((tm, D_MODEL), lambda i, j: (i, 0)),
            pl.BlockSpec((None, 1, D_MODEL), lambda i, j: (layer, 0, 0)),
            pl.BlockSpec((None, D_MODEL, tn), lambda i, j: (layer, 0, j)),
            pl.BlockSpec((1, tn), lambda i, j: (0, j)),
        ],
        out_specs=pl.BlockSpec((tm, tn), lambda i, j: (i, j)),
        out_shape=jax.ShapeDtypeStruct((SEQ, QKV_WIDTH), _BF16),
        scratch_shapes=[pltpu.VMEM((tm, D_MODEL), _BF16)],
        compiler_params=_cparams(2),
        name="qkv_proj",
    )(x, g_all, w_all, col_scale)


def _split_heads_t(q_ref):
    q_t = q_ref[...].T
    dim = lax.broadcasted_iota(jnp.int32, q_t.shape, 0)
    zero = jnp.zeros_like(q_t)
    return jnp.where(dim < HEAD_DIM, q_t, zero), jnp.where(dim >= HEAD_DIM, q_t, zero)


def _key_block(ref, j):
    return ref[pl.ds(pl.multiple_of(j * ATT_T, ATT_T), ATT_T), :]


def _pair_rmsnorm(o_t, gain_col):
    top, bot = o_t[:HEAD_DIM], o_t[HEAD_DIM:]
    top = top * lax.rsqrt(jnp.mean(top * top, axis=0, keepdims=True) + NORM_EPS)
    bot = bot * lax.rsqrt(jnp.mean(bot * bot, axis=0, keepdims=True) + NORM_EPS)
    return (jnp.concatenate([top, bot], axis=0) * gain_col).T


MIX_TQ = 512
MIX_TK = 256
MIX_DIAG = MIX_TQ // MIX_TK
assert MIX_DIAG == 2
N_KBLK = SEQ // MIX_TK
DIFF_PER_STEP = 2
DIFF_SPLIT = 3
DIFF_EXTRA = 16
SB_SUB = 128
PIPE_PERIOD = 4
SCORE_SLOTS = 2
assert PIPE_PERIOD % SCORE_SLOTS == 0 and PIPE_PERIOD % 2 == 0


def _visible_queries(diagonal):
    return slice((diagonal or 0) * MIX_TK, MIX_TQ)


def _transpose_blocks(v_ref, vt_ref, rows=None, per_iter=4):
    def body(u, carry):
        for n in range(per_iter):
            b = u * per_iter + n
            blk = v_ref[pl.ds(pl.multiple_of(b * MIX_TK, MIX_TK), MIX_TK), :].T
            if rows is None:
                vt_ref[b] = blk
            else:
                vt_ref[b, :LANES] = blk
                vt_ref[b, LANES:] = rows
        return carry

    lax.fori_loop(0, N_KBLK // per_iter, body, 0)


class _DiffStream:
    def __init__(self, i, slope_row, q_ref, k_ref, v_ref, diag, vt, s, mx, p, al, m, acc):
        self.i, self.k_ref, self.v_ref = i, k_ref, v_ref
        self.diag, self.vt, self.s, self.mx, self.p, self.al, self.m, self.acc = (
            diag, vt, s, mx, p, al, m, acc)
        self.slope2 = slope_row(0)
        dim = lax.broadcasted_iota(jnp.int32, (LANES, MIX_TQ), 0)
        slope_rows = functools.reduce(
            lambda a, n: jnp.where(dim == n, slope_row(1 + n), a), range(DIFF_SPLIT),
            jnp.zeros((LANES, MIX_TQ), _F32)).astype(_BF16)
        self.q_maps = tuple(jnp.concatenate([q_half, slope_rows], axis=0)
                            for q_half in _split_heads_t(q_ref))
        lane = lax.broadcasted_iota(jnp.int32, (MIX_TK, LANES), 1)
        key_offset = lax.broadcasted_iota(jnp.int32, (MIX_TK, LANES), 0).astype(_F32)
        self.key_feat = jnp.where(lane < DIFF_SPLIT, key_offset, 0.0).astype(_BF16)

    def init_head(self):
        row = lax.broadcasted_iota(jnp.int32, (MIX_TK, MIX_TQ), 0)
        col = lax.broadcasted_iota(jnp.int32, (MIX_TK, MIX_TQ), 1)
        for d in range(MIX_DIAG):
            key = row + d * MIX_TK
            allowed = (key // CHUNK) <= (col // CHUNK)
            lead = (col - row).astype(_F32) - jnp.abs(col - key).astype(_F32)
            self.diag[d] = jnp.where(allowed, self.slope2 * lead, -jnp.inf)
        rows = lax.broadcasted_iota(jnp.int32, (DIFF_EXTRA, MIX_TK), 0)
        _transpose_blocks(self.v_ref, self.vt, jnp.where(rows == 0, 1.0, 0.0).astype(_BF16))

    def init_tile(self):
        self.m[...] = jnp.full(self.m.shape, _NEG, _F32)
        self.acc[...] = jnp.zeros(self.acc.shape, _F32)

    def _offset(self, t, diagonal):
        if diagonal is not None:
            return jnp.zeros((), _F32)
        return -self.slope2 * (self.i * MIX_TQ - t * MIX_TK).astype(_F32)

    def stage_a(self, t, ts, diagonal=None):
        slot, q = ts % SCORE_SLOTS, _visible_queries(diagonal)
        kb = self.k_ref[pl.ds(pl.multiple_of(t * MIX_TK, MIX_TK), MIX_TK), :]
        kx = jnp.concatenate([kb, self.key_feat], axis=1)
        offset = self._offset(t, diagonal)
        for c in range(2):
            s = jnp.dot(kx, self.q_maps[c][:, q], preferred_element_type=_F32)
            if diagonal is not None:
                s = s + self.diag[diagonal, :, q]
            self.s[slot, c, :, q] = s
            self.mx[slot, c, :, q] = jnp.max(s, axis=0, keepdims=True) + offset

    def stage_b(self, t, ts, diagonal=None):
        slot, out, q = ts % SCORE_SLOTS, ts % 2, _visible_queries(diagonal)
        offset = self._offset(t, diagonal)
        for c in range(2):
            m_old = self.m[c, :, q]
            m_new = jnp.maximum(m_old, self.mx[slot, c, :, q])
            self.p[out, c, :, q] = jnp.exp2(self.s[slot, c, :, q] - (m_new - offset)).astype(_BF16)
            self.al[out, c, :, q] = jnp.exp2(m_old - m_new)
            self.m[c, :, q] = m_new

    def stage_c(self, t, ts, diagonal=None):
        slot, q = ts % 2, _visible_queries(diagonal)
        vt = self.vt[t]
        for c in range(2):
            pv = jnp.dot(vt, self.p[slot, c, :, q], preferred_element_type=_F32)
            self.acc[c, :, q] = self.al[slot, c, :, q] * self.acc[c, :, q] + pv

    def result(self, lam, gain_col, lam_init):
        acc0, acc1 = self.acc[0], self.acc[1]
        o_t = (acc0[:LANES] / acc0[LANES:LANES + 1]
               - lam * (acc1[:LANES] / acc1[LANES:LANES + 1]))
        ms = jnp.mean(o_t * o_t, axis=0, keepdims=True)
        return (o_t * lax.rsqrt(ms + SUBLN_EPS) * gain_col * (1.0 - lam_init)).T


class _SbStream:
    def __init__(self, nb, q_ref, k_ref, v_ref, mask, tri, vt, z, hl, a, r, acc):
        self.nb, self.k_ref, self.v_ref = nb, k_ref, v_ref
        self.mask, self.tri, self.vt, self.z, self.hl, self.a, self.r, self.acc = (
            mask, tri, vt, z, hl, a, r, acc)
        self.q_heads = _split_heads_t(q_ref)

    def init_head(self):
        row = lax.broadcasted_iota(jnp.int32, (MIX_TK, MIX_TQ), 0)
        col = lax.broadcasted_iota(jnp.int32, (MIX_TK, MIX_TQ), 1)
        for d in range(MIX_DIAG):
            self.mask[d] = jnp.where(row + d * MIX_TK < col, 0.0, -jnp.inf)
        r2 = lax.broadcasted_iota(jnp.int32, (SB_SUB, 2 * SB_SUB), 0)
        c2 = lax.broadcasted_iota(jnp.int32, (SB_SUB, 2 * SB_SUB), 1) % SB_SUB
        self.tri[...] = jnp.where(c2 >= r2, 1.0, 0.0).astype(_BF16)
        _transpose_blocks(self.v_ref, self.vt)

    def init_tile(self):
        self.r[...] = jnp.zeros(self.r.shape, _F32)
        self.acc[...] = jnp.zeros(self.acc.shape, _F32)

    def stage_a(self, t, ts, diagonal=None):
        slot = ts % 2
        block = None if diagonal is None else MIX_DIAG - 1 - diagonal
        q = _visible_queries(block)
        start = pl.multiple_of((self.nb - 1 - t) * MIX_TK, MIX_TK)
        kb = self.k_ref[pl.ds(start, MIX_TK), :]
        for hh in range(2):
            z = jnp.dot(kb, self.q_heads[hh][:, q], preferred_element_type=_F32)
            if block is not None:
                z = z + self.mask[block, :, q]
            neg_abs = pltpu.bitcast(pltpu.bitcast(z, jnp.int32) | _SIGN_BIT, _F32)
            nl = jnp.maximum(z, 0.0) + jnp.log2(1.0 + jnp.exp2(neg_abs))
            hi = nl.astype(_BF16)
            lo = (nl - hi.astype(_F32)).astype(_BF16)
            self.z[slot, hh, :, q] = z
            for sub in range(MIX_TK // SB_SUB):
                rows = slice(sub * SB_SUB, (sub + 1) * SB_SUB)
                self.hl[slot, hh, sub, :SB_SUB, q] = hi[rows]
                self.hl[slot, hh, sub, SB_SUB:, q] = lo[rows]

    def stage_b(self, t, ts, diagonal=None):
        del t
        slot = ts % 2
        block = None if diagonal is None else MIX_DIAG - 1 - diagonal
        q = _visible_queries(block)
        tri = self.tri[...]
        for hh in range(2):
            if q.start:
                self.a[slot, hh, :, :q.start] = jnp.zeros((MIX_TK, q.start), _BF16)
            right = self.r[hh, :, q]
            for sub in reversed(range(MIX_TK // SB_SUB)):
                rows = slice(sub * SB_SUB, (sub + 1) * SB_SUB)
                incl = jnp.dot(tri, self.hl[slot, hh, sub, :, q], preferred_element_type=_F32)
                self.a[slot, hh, rows, q] = jnp.exp2(
                    self.z[slot, hh, rows, q] - incl - right).astype(_BF16)
                right = right + incl[0:1]
            self.r[hh, :, q] = right

    def stage_c(self, t, ts, diagonal=None):
        del diagonal
        slot = ts % 2
        vt = self.vt[self.nb - 1 - t]
        for hh in range(2):
            self.acc[hh] += jnp.dot(vt[hh * HEAD_DIM:(hh + 1) * HEAD_DIM], self.a[slot, hh],
                                    preferred_element_type=_F32)

    def result(self, gain_col):
        o_t = jnp.concatenate([self.acc[0], self.acc[1]], axis=0)
        return _pair_rmsnorm(o_t, gain_col)


BAND_TILES = 1 + (LEFT_CHUNKS * CHUNK) // ATT_T
BAND_SUBS = MIX_TQ // ATT_T
assert ATT_T == MIX_TK


def _band_sub_tile(i, sub, q_heads, bias_ref, k_ref, vt_ref):
    first = i * BAND_SUBS + sub - (BAND_TILES - 1)
    tiles = [jnp.maximum(first + kk, 0) for kk in range(BAND_TILES)]
    heads = []
    for hh in range(2):
        q_sub = q_heads[hh][:, sub * ATT_T:(sub + 1) * ATT_T]
        scores = []
        for kk in range(BAND_TILES):
            bias = bias_ref[hh, jnp.where(first + kk >= 0, kk, BAND_TILES)]
            scores.append(jnp.dot(_key_block(k_ref, tiles[kk]), q_sub,
                                  preferred_element_type=_F32) + bias)
        m = functools.reduce(jnp.maximum, [jnp.max(s, axis=0, keepdims=True) for s in scores])
        probs = [jnp.exp2(s - m) for s in scores]
        denom = functools.reduce(jnp.add, [jnp.sum(p, axis=0, keepdims=True) for p in probs])
        o_t = functools.reduce(jnp.add, [
            jnp.dot(vt_ref[tiles[kk]][hh * HEAD_DIM:(hh + 1) * HEAD_DIM],
                    probs[kk].astype(_BF16), preferred_element_type=_F32)
            for kk in range(BAND_TILES)])
        heads.append(o_t / denom)
    return jnp.concatenate(heads, axis=0)


def _mix_kernel(slope_ref, lam_ref, subln_ref, sbgain_ref, chgain_ref, biasvec_ref,
                dq0_ref, dq1_ref, dk0_ref, dk1_ref, dv0_ref, dv1_ref, sq_ref, sk_ref, sv_ref,
                cq_ref, ck_ref, cv_ref,
                ya_ref, yb_ref, yc_ref,
                diag_scr, dvt_scr, s_scr, mx_scr, p_scr, al_scr, m_scr, dacc_scr,
                mask_scr, tri_scr, svt_scr, z_scr, hl_scr, a_scr, r_scr, sacc_scr, cvt_scr, cbias_scr,
                *, lam_init):
    g = pl.program_id(0)
    i = pl.program_id(1)
    nb = (i + 1) * MIX_DIAG

    diffs = []
    for hd, (q_ref, k_ref, v_ref) in enumerate(
            ((dq0_ref, dk0_ref, dv0_ref), (dq1_ref, dk1_ref, dv1_ref))):
        head = g * DIFF_PER_STEP + hd
        diffs.append(_DiffStream(
            i, functools.partial(lambda n, head: slope_ref[head, n], head=head),
            q_ref, k_ref, v_ref, diag_scr.at[hd], dvt_scr.at[hd], s_scr.at[hd], mx_scr.at[hd],
            p_scr.at[hd], al_scr.at[hd], m_scr.at[hd], dacc_scr.at[hd]))
    sb = _SbStream(nb, sq_ref, sk_ref, sv_ref, mask_scr, tri_scr, svt_scr, z_scr, hl_scr, a_scr,
                   r_scr, sacc_scr)
    streams = diffs + [sb]

    @pl.when(i == 0)
    def _():
        for stream in streams:
            stream.init_head()
        _transpose_blocks(cv_ref, cvt_scr)
        _build_band_bias(biasvec_ref, cbias_scr)

    for stream in streams:
        stream.init_tile()

    band_q = _split_heads_t(cq_ref)

    def band(sub):
        o_t = _band_sub_tile(i, sub, band_q, cbias_scr, ck_ref, cvt_scr)
        yc_ref[sub * ATT_T:(sub + 1) * ATT_T, :] = _pair_rmsnorm(
            o_t, chgain_ref[...]).astype(yc_ref.dtype)

    def prologue(first_tile):
        assert BAND_SUBS == 2
        for step in range(2):
            sb.stage_a(step, step, diagonal=step)
            for d in diffs:
                d.stage_a(step, step, diagonal=step if first_tile else None)
            band(step)
            if step == 0:
                sb.stage_b(0, 0, diagonal=0)
                for d in diffs:
                    d.stage_b(0, 0, diagonal=0 if first_tile else None)

    def pipeline_step(t, ts, a_diag=None, b_diag=None):
        for s in streams:
            s.stage_c(t - 2, (ts - 2) % PIPE_PERIOD)
        sb.stage_b(t - 1, (ts - 1) % PIPE_PERIOD)
        for d in diffs:
            d.stage_b(t - 1, (ts - 1) % PIPE_PERIOD, diagonal=b_diag)
        sb.stage_a(t, ts)
        for d in diffs:
            d.stage_a(t, ts, diagonal=a_diag)

    def two_steps(t, ts, last=False):
        pipeline_step(t, ts, a_diag=0 if last else None)
        pipeline_step(t + 1, ts + 1, a_diag=1 if last else None, b_diag=0 if last else None)

    def drain(ts_end):
        for s in streams:
            s.stage_c(nb - 2, (ts_end - 1) % PIPE_PERIOD)
        sb.stage_b(nb - 1, ts_end)
        for d in diffs:
            d.stage_b(nb - 1, ts_end, diagonal=MIX_DIAG - 1)
        sb.stage_c(nb - 1, ts_end)
        for d in diffs:
            d.stage_c(nb - 1, ts_end, diagonal=MIX_DIAG - 1)

    @pl.when(i == 0)
    def _():
        prologue(first_tile=True)
        drain(1)

    @pl.when(i > 0)
    def _():
        prologue(first_tile=False)

        def body(u, carry):
            t = 2 + 4 * u
            two_steps(t, 2)
            two_steps(t + 2, 0)
            return carry

        lax.fori_loop(0, (i - 1) // 2, body, 0)

        @pl.when(i % 2 == 0)
        def _():
            two_steps(nb - 4, 2)
            two_steps(nb - 2, 0, last=True)
            drain(1)

        @pl.when(i % 2 == 1)
        def _():
            two_steps(nb - 2, 2, last=True)
            drain(3)

    lam_v = lam_ref[...]
    lam = (jnp.exp(jnp.sum(lam_v[0:1] * lam_v[1:2], axis=-1, keepdims=True))
           - jnp.exp(jnp.sum(lam_v[2:3] * lam_v[3:4], axis=-1, keepdims=True))
           + lam_init)
    for hd, d in enumerate(diffs):
        ya_ref[:, hd * LANES:(hd + 1) * LANES] = d.result(
            lam, subln_ref[...], lam_init).astype(ya_ref.dtype)
    yb_ref[...] = sb.result(sbgain_ref[...]).astype(yb_ref.dtype)


def _alibi_slope_table():
    slope = 2.0 ** (-8.0 * np.arange(1, N_HEADS + 1, dtype=np.float32) / N_HEADS)
    slope2 = (slope * np.float32(LOG2E)).astype(np.float32)
    cols, rest = [slope2], slope2
    for _ in range(DIFF_SPLIT):
        term = rest.astype(_BF16).astype(np.float32)
        cols.append(term)
        rest = rest - term
    return jnp.asarray(np.stack(cols, axis=1))


def _mix_attention(qkv, slopes, lam_vecs, subln_col, sb_col, ch_col, bias_vecs, layer, lam_init):
    tile = (MIX_TK, MIX_TQ)
    nd = DIFF_PER_STEP
    once = pl.Buffered(1)

    def q_spec(col):
        return pl.BlockSpec((MIX_TQ, LANES), lambda g, i: (i, col(g)))

    def kv_spec(col):
        return pl.BlockSpec((SEQ, LANES), lambda g, i: (0, col(g)), pipeline_mode=once)

    def pair_gain_spec():
        return pl.BlockSpec((None, None, LANES, 1), lambda g, i: (layer, g, 0, 0))

    return pl.pallas_call(
        functools.partial(_mix_kernel, lam_init=lam_init),
        grid=(N_HEADS // nd, SEQ // MIX_TQ),
        in_specs=[
            pl.BlockSpec(memory_space=pltpu.SMEM),
            pl.BlockSpec((None, 4, HEAD_DIM), lambda g, i: (layer, 0, 0)),
            pl.BlockSpec((None, LANES, 1), lambda g, i: (layer, 0, 0)),
            pair_gain_spec(), pair_gain_spec(),
            pl.BlockSpec((None, 2, BAND_TILES, 1, 2 * ATT_T), lambda g, i: (g, 0, 0, 0, 0)),
            q_spec(lambda g: QA + nd * g), q_spec(lambda g: QA + nd * g + 1),
            kv_spec(lambda g: KA + nd * g), kv_spec(lambda g: KA + nd * g + 1),
            kv_spec(lambda g: VA + nd * g), kv_spec(lambda g: VA + nd * g + 1),
            q_spec(lambda g: QB + g), kv_spec(lambda g: KB + g), kv_spec(lambda g: VB + g),
            q_spec(lambda g: QC + g), kv_spec(lambda g: KC + g), kv_spec(lambda g: VC + g),
        ],
        out_specs=[
            pl.BlockSpec((MIX_TQ, nd * LANES), lambda g, i: (i, g)),
            pl.BlockSpec((MIX_TQ, LANES), lambda g, i: (i, g)),
            pl.BlockSpec((MIX_TQ, LANES), lambda g, i: (i, g)),
        ],
        out_shape=[
            jax.ShapeDtypeStruct((SEQ, DIFF_WIDTH), _BF16),
            jax.ShapeDtypeStruct((SEQ, SB_WIDTH), _BF16),
            jax.ShapeDtypeStruct((SEQ, CH_WIDTH), _BF16),
        ],
        scratch_shapes=[
            pltpu.VMEM((nd, MIX_DIAG) + tile, _F32),
            pltpu.VMEM((nd, N_KBLK, LANES + DIFF_EXTRA, MIX_TK), _BF16),
            pltpu.VMEM((nd, SCORE_SLOTS, 2) + tile, _F32),
            pltpu.VMEM((nd, SCORE_SLOTS, 2, 1, MIX_TQ), _F32),
            pltpu.VMEM((nd, 2, 2) + tile, _BF16),
            pltpu.VMEM((nd, 2, 2, 1, MIX_TQ), _F32),
            pltpu.VMEM((nd, 2, 1, MIX_TQ), _F32),
            pltpu.VMEM((nd, 2, LANES + DIFF_EXTRA, MIX_TQ), _F32),
            pltpu.VMEM((MIX_DIAG,) + tile, _F32),
            pltpu.VMEM((SB_SUB, 2 * SB_SUB), _BF16),
            pltpu.VMEM((N_KBLK, LANES, MIX_TK), _BF16),
            pltpu.VMEM((2, 2) + tile, _F32),
            pltpu.VMEM((2, 2, MIX_TK // SB_SUB, 2 * SB_SUB, MIX_TQ), _BF16),
            pltpu.VMEM((2, 2) + tile, _BF16),
            pltpu.VMEM((2, 1, MIX_TQ), _F32),
            pltpu.VMEM((2, HEAD_DIM, MIX_TQ), _F32),
            pltpu.VMEM((N_KBLK, LANES, MIX_TK), _BF16),
            pltpu.VMEM((2, BAND_TILES + 1, ATT_T, ATT_T), _F32),
        ],
        compiler_params=pltpu.CompilerParams(
            dimension_semantics=("arbitrary", "arbitrary"), vmem_limit_bytes=MIX_VMEM_LIMIT),
        name="token_mixers",
    )(slopes, lam_vecs, subln_col, sb_col, ch_col, bias_vecs, *([qkv] * 12))


def _band_bias_vectors(rel_table):
    t = ATT_T
    shift = np.arange(2 * t) - t
    idx = np.stack([np.clip((BAND_TILES - 1 - kk) * t + shift, -(CHUNK - 1), MAX_REL) + (CHUNK - 1)
                    for kk in range(BAND_TILES)])
    vec = rel_table[:, idx] * LOG2E
    return vec.reshape(N_HEADS // 2, 2, BAND_TILES, 1, 2 * t)


def _build_band_bias(vec_ref, bias_scr):
    t = ATT_T
    key = lax.broadcasted_iota(jnp.int32, (t, t), 0)
    qry = lax.broadcasted_iota(jnp.int32, (t, t), 1)
    for kk in range(BAND_TILES):
        back = (BAND_TILES - 1 - kk) * t
        dc = back // CHUNK + qry // CHUNK - key // CHUNK
        in_band = (dc >= 0) & (dc <= LEFT_CHUNKS)
        for hh in range(2):
            rows = jnp.broadcast_to(vec_ref[hh, kk], (t, 2 * t))
            toeplitz = pltpu.roll(rows, 0, 1, stride=1, stride_axis=0)[:, t:]
            bias_scr[hh, kk] = jnp.where(in_band, toeplitz, -jnp.inf)
    for hh in range(2):
        bias_scr[hh, BAND_TILES] = jnp.full((t, t), -jnp.inf, _F32)


def _oproj_kernel(ya_ref, yb_ref, yc_ref, w_ref, x_ref, o_ref, w_scr):
    @pl.when(pl.program_id(1) == 0)
    def _():
        w_scr[...] = w_ref[...].astype(_BF16)

    y = jnp.concatenate([ya_ref[...], yb_ref[...], yc_ref[...]], axis=1)
    o_ref[...] = x_ref[...] + jnp.dot(y, w_scr[...], preferred_element_type=_F32)


def _out_proj(y_a, y_b, y_c, w_all, x, layer, tm=512, tn=D_MODEL):
    return pl.pallas_call(
        _oproj_kernel,
        grid=(D_MODEL // tn, SEQ // tm),
        in_specs=[
            pl.BlockSpec((tm, DIFF_WIDTH), lambda j, i: (i, 0)),
            pl.BlockSpec((tm, SB_WIDTH), lambda j, i: (i, 0)),
            pl.BlockSpec((tm, CH_WIDTH), lambda j, i: (i, 0)),
            pl.BlockSpec((None, D_MODEL, tn), lambda j, i: (layer, 0, j),
                         pipeline_mode=pl.Buffered(1)),
            pl.BlockSpec((tm, tn), lambda j, i: (i, j)),
        ],
        out_specs=pl.BlockSpec((tm, tn), lambda j, i: (i, j)),
        out_shape=jax.ShapeDtypeStruct((SEQ, D_MODEL), _F32),
        scratch_shapes=[pltpu.VMEM((D_MODEL, tn), _BF16)],
        compiler_params=_cparams(2),
        name="out_proj",
    )(y_a, y_b, y_c, w_all, x)


def _ffn_kernel(x_ref, halo_ref, g_ref, wg_ref, wu_ref, cw_ref, cb_ref, wd_ref, og_ref, o_ref,
                h_scr, gate_scr, *, tm, norm_output):
    i = pl.program_id(0)
    j = pl.program_id(1)

    def norm(x):
        ms = jnp.mean(x * x, axis=-1, keepdims=True)
        return (x * lax.rsqrt(ms + NORM_EPS) * g_ref[...]).astype(_BF16)

    @pl.when(j == 0)
    def _():
        x = x_ref[...]
        h_scr[HALO:, :] = norm(x)
        halo = jnp.where(i > 0, halo_ref[...], 0.0)
        h_scr[:HALO, :] = norm(halo)
        o_ref[...] = x

    gate_scr[...] = jnp.dot(h_scr[...], wg_ref[...], preferred_element_type=_F32)
    up = jnp.dot(h_scr[HALO:, :], wu_ref[...], preferred_element_type=_F32)
    cw = cw_ref[...]
    g = cb_ref[...]
    for t in range(CONV_WIDTH):
        lo = HALO - (CONV_WIDTH - 1) + t
        g = g + gate_scr[lo:lo + tm, :] * cw[t:t + 1]
    act = (g * jax.nn.sigmoid(g) * up).astype(_BF16)
    o_ref[...] += jnp.dot(act, wd_ref[...], preferred_element_type=_F32)

    if norm_output:
        @pl.when(j == pl.num_programs(1) - 1)
        def _():
            y = o_ref[...]
            ms = jnp.mean(y * y, axis=-1, keepdims=True)
            o_ref[...] = y * lax.rsqrt(ms + NORM_EPS) * og_ref[...]


def _ffn(x, g_all, wg_all, wu_all, cw_all, cb_all, wd_all, out_gain, layer, norm_output,
         tm=1024, tf=512):
    halo_blocks = tm // HALO
    return pl.pallas_call(
        functools.partial(_ffn_kernel, tm=tm, norm_output=norm_output),
        grid=(SEQ // tm, D_FF // tf),
        in_specs=[
            pl.BlockSpec((tm, D_MODEL), lambda i, j: (i, 0)),
            pl.BlockSpec((HALO, D_MODEL),
                         lambda i, j: (jnp.maximum(i * halo_blocks - 1, 0), 0)),
            pl.BlockSpec((None, 1, D_MODEL), lambda i, j: (layer, 0, 0)),
            pl.BlockSpec((None, D_MODEL, tf), lambda i, j: (layer, 0, j)),
            pl.BlockSpec((None, D_MODEL, tf), lambda i, j: (layer, 0, j)),
            pl.BlockSpec((None, CONV_WIDTH, tf), lambda i, j: (layer, 0, j)),
            pl.BlockSpec((None, 1, tf), lambda i, j: (layer, 0, j)),
            pl.BlockSpec((None, tf, D_MODEL), lambda i, j: (layer, j, 0)),
            pl.BlockSpec((1, D_MODEL), lambda i, j: (0, 0)),
        ],
        out_specs=pl.BlockSpec((tm, D_MODEL), lambda i, j: (i, 0)),
        out_shape=jax.ShapeDtypeStruct((SEQ, D_MODEL), _F32),
        scratch_shapes=[
            pltpu.VMEM((tm + HALO, D_MODEL), _BF16),
            pltpu.VMEM((tm + HALO, tf), _F32),
        ],
        compiler_params=_cparams(2, BIG_VMEM_LIMIT),
        name="gated_mlp",
    )(x, x, g_all, wg_all, wu_all, cw_all, cb_all, wd_all, out_gain)


def kernel(x, attn_norm, w_qkv, lambda_q1, lambda_k1, lambda_q2, lambda_k2, diff_subln,
           sb_norm, rel_bias, ch_norm, w_o, ffn_norm, w_gate, w_up, conv_w, conv_b, w_down,
           final_norm):
    batch, seq, d_model = x.shape
    assert (batch, seq, d_model) == (1, SEQ, D_MODEL)
    x2 = x.reshape(SEQ, D_MODEL)

    w_qkv_b = w_qkv.astype(_BF16)
    w_gate_b = w_gate.astype(_BF16)
    w_up_b = w_up.astype(_BF16)
    w_down_b = w_down.astype(_BF16)

    attn_g = attn_norm.reshape(DEPTH, 1, D_MODEL)
    ffn_g = ffn_norm.reshape(DEPTH, 1, D_MODEL)
    conv_b3 = conv_b.reshape(DEPTH, 1, D_FF)
    lam_vecs = jnp.stack([lambda_q1, lambda_k1, lambda_q2, lambda_k2], axis=1)
    subln_col = diff_subln.reshape(DEPTH, 2 * HEAD_DIM, 1)
    sb_col = sb_norm.reshape(DEPTH, N_HEADS // 2, 2 * HEAD_DIM, 1)
    ch_col = ch_norm.reshape(DEPTH, N_HEADS // 2, 2 * HEAD_DIM, 1)
    slopes = _alibi_slope_table()
    col_scale = jnp.asarray(_qkv_col_scale())

    for layer in range(DEPTH):
        lam_init = 0.8 - 0.6 * math.exp(-0.3 * layer)
        qkv = _qkv_proj(x2, attn_g, w_qkv_b, col_scale, layer)
        y_a, y_b, y_c = _mix_attention(qkv, slopes, lam_vecs, subln_col, sb_col, ch_col,
                                       _band_bias_vectors(rel_bias[layer]), layer, lam_init)
        x2 = _out_proj(y_a, y_b, y_c, w_o, x2, layer)
        x2 = _ffn(x2, ffn_g, w_gate_b, w_up_b, conv_w, conv_b3, w_down_b,
                  final_norm.reshape(1, D_MODEL), layer, norm_output=layer == DEPTH - 1)
    return x2.reshape(1, SEQ, D_MODEL)
```

```python
import functools
import math

import jax
import jax.numpy as jnp
import numpy as np
from jax import lax
from jax.experimental import pallas as pl
from jax.experimental.pallas import tpu as pltpu

D_MODEL = 2048
SEQ = 8192
DEPTH = 4
CHUNK = 64
HEAD_DIM = 64
N_HEADS = 8
DIFF_WIDTH = 1024
SB_WIDTH = 512
CH_WIDTH = 512
QKV_WIDTH = 6144
LEFT_CHUNKS = 8
MAX_REL = 128
D_FF = 5632
CONV_WIDTH = 3
NORM_EPS = 1e-6
SUBLN_EPS = 1e-5
SCALE = HEAD_DIM ** -0.5
LOG2E = math.log2(math.e)

LANES = 128
ATT_T = 256
HALO = 16
VMEM_LIMIT = 48 * 1024 * 1024

QKV_TM, QKV_TN = 1024, 1024
OPROJ_TM = 512
FFN_TM, FFN_TF = 1024, 512
BIG_VMEM_LIMIT = 56 * 1024 * 1024
MIX_VMEM_LIMIT = BIG_VMEM_LIMIT

QA, KA, VA = 0, 8, 16
QB, KB, VB = 24, 28, 32
QC, KC, VC = 36, 40, 44

_F32 = jnp.float32
_BF16 = jnp.bfloat16
_NEG = -1e30
_SIGN_BIT = np.int32(-2 ** 31)


def _cparams(n_axes, vmem_limit=VMEM_LIMIT):
    return pltpu.CompilerParams(
        dimension_semantics=("arbitrary",) * n_axes,
        vmem_limit_bytes=vmem_limit)


def _qkv_kernel(x_ref, g_ref, w_ref, cs_ref, o_ref, h_scr):
    @pl.when(pl.program_id(1) == 0)
    def _():
        x = x_ref[...]
        ms = jnp.mean(x * x, axis=-1, keepdims=True)
        h_scr[...] = (x * lax.rsqrt(ms + NORM_EPS) * g_ref[...]).astype(_BF16)

    acc = jnp.dot(h_scr[...], w_ref[...], preferred_element_type=_F32)
    o_ref[...] = (acc * cs_ref[...]).astype(_BF16)


def _qkv_col_scale():
    cs = np.ones((1, QKV_WIDTH), np.float32)
    for q_start, k_start in ((QA, KA), (QB, KB), (QC, KC)):
        cs[:, q_start * LANES:k_start * LANES] = SCALE * LOG2E
    return cs


def _qkv_proj(x, g_all, w_all, col_scale, layer, tm=QKV_TM, tn=QKV_TN):
    return pl.pallas_call(
        _qkv_kernel,
        grid=(SEQ // tm, QKV_WIDTH // tn),
        in_specs=[
            pl.BlockSpec((tm, D_MODEL), lambda i, j: (i, 0)),
            pl.BlockSpec((None, 1, D_MODEL), lambda i, j: (layer, 0, 0)),
            pl.BlockSpec((None, D_MODEL, tn), lambda i, j: (layer, 0, j)),
            pl.BlockSpec((1, tn), lambda i, j: (0, j)),
        ],
        out_specs=pl.BlockSpec((tm, tn), lambda i, j: (i, j)),
        out_shape=jax.ShapeDtypeStruct((SEQ, QKV_WIDTH), _BF16),
        scratch_shapes=[pltpu.VMEM((tm, D_MODEL), _BF16)],
        compiler_params=_cparams(2),
        name="qkv_proj",
    )(x, g_all, w_all, col_scale)


def _split_heads_t(q_ref):
    q_t = q_ref[...].T
    dim = lax.broadcasted_iota(jnp.int32, q_t.shape, 0)
    zero = jnp.zeros_like(q_t)
    return jnp.where(dim < HEAD_DIM, q_t, zero), jnp.where(dim >= HEAD_DIM, q_t, zero)


def _key_block(ref, j):
    return ref[pl.ds(pl.multiple_of(j * ATT_T, ATT_T), ATT_T), :]


def _pair_rmsnorm(o_t, gain_col):
    top, bot = o_t[:HEAD_DIM], o_t[HEAD_DIM:]
    top = top * lax.rsqrt(jnp.mean(top * top, axis=0, keepdims=True) + NORM_EPS)
    bot = bot * lax.rsqrt(jnp.mean(bot * bot, axis=0, keepdims=True) + NORM_EPS)
    return (jnp.concatenate([top, bot], axis=0) * gain_col).T


MIX_TQ = 512
MIX_TK = 256
MIX_DIAG = MIX_TQ // MIX_TK
assert MIX_DIAG == 2
N_KBLK = SEQ // MIX_TK
DIFF_PER_STEP = 2
DIFF_SPLIT = 3
DIFF_EXTRA = 16
SB_SUB = 128
PIPE_PERIOD = 4
SCORE_SLOTS = 2
assert PIPE_PERIOD % SCORE_SLOTS == 0 and PIPE_PERIOD % 2 == 0


def _visible_queries(diagonal):
    return slice((diagonal or 0) * MIX_TK, MIX_TQ)


def _transpose_blocks(v_ref, vt_ref, rows=None, per_iter=8):
    def body(u, carry):
        for n in range(per_iter):
            b = u * per_iter + n
            blk = v_ref[pl.ds(pl.multiple_of(b * MIX_TK, MIX_TK), MIX_TK), :].T
            if rows is None:
                vt_ref[b] = blk
            else:
                vt_ref[b, :LANES] = blk
                vt_ref[b, LANES:] = rows
        return carry

    lax.fori_loop(0, N_KBLK // per_iter, body, 0)


class _DiffStream:
    def __init__(self, i, slope_row, q_ref, k_ref, v_ref, diag, vt, s, mx, p, al, m, acc):
        self.i, self.k_ref, self.v_ref = i, k_ref, v_ref
        self.diag, self.vt, self.s, self.mx, self.p, self.al, self.m, self.acc = (
            diag, vt, s, mx, p, al, m, acc)
        self.slope2 = slope_row(0)
        dim = lax.broadcasted_iota(jnp.int32, (LANES, MIX_TQ), 0)
        slope_rows = functools.reduce(
            lambda a, n: jnp.where(dim == n, slope_row(1 + n), a), range(DIFF_SPLIT),
            jnp.zeros((LANES, MIX_TQ), _F32)).astype(_BF16)
        self.q_maps = tuple(jnp.concatenate([q_half, slope_rows], axis=0)
                            for q_half in _split_heads_t(q_ref))
        lane = lax.broadcasted_iota(jnp.int32, (MIX_TK, LANES), 1)
        key_offset = lax.broadcasted_iota(jnp.int32, (MIX_TK, LANES), 0).astype(_F32)
        self.key_feat = jnp.where(lane < DIFF_SPLIT, key_offset, 0.0).astype(_BF16)

    def init_head(self):
        row = lax.broadcasted_iota(jnp.int32, (MIX_TK, MIX_TQ), 0)
        col = lax.broadcasted_iota(jnp.int32, (MIX_TK, MIX_TQ), 1)
        for d in range(MIX_DIAG):
            key = row + d * MIX_TK
            allowed = (key // CHUNK) <= (col // CHUNK)
            lead = (col - row).astype(_F32) - jnp.abs(col - key).astype(_F32)
            self.diag[d] = jnp.where(allowed, self.slope2 * lead, -jnp.inf)
        rows = lax.broadcasted_iota(jnp.int32, (DIFF_EXTRA, MIX_TK), 0)
        _transpose_blocks(self.v_ref, self.vt, jnp.where(rows == 0, 1.0, 0.0).astype(_BF16))

    def init_tile(self):
        self.m[...] = jnp.full(self.m.shape, _NEG, _F32)
        self.acc[...] = jnp.zeros(self.acc.shape, _F32)

    def _offset(self, t, diagonal):
        if diagonal is not None:
            return jnp.zeros((), _F32)
        return -self.slope2 * (self.i * MIX_TQ - t * MIX_TK).astype(_F32)

    def stage_a(self, t, ts, diagonal=None):
        slot, q = ts % SCORE_SLOTS, _visible_queries(diagonal)
        kb = self.k_ref[pl.ds(pl.multiple_of(t * MIX_TK, MIX_TK), MIX_TK), :]
        kx = jnp.concatenate([kb, self.key_feat], axis=1)
        offset = self._offset(t, diagonal)
        for c in range(2):
            s = jnp.dot(kx, self.q_maps[c][:, q], preferred_element_type=_F32)
            if diagonal is not None:
                s = s + self.diag[diagonal, :, q]
            self.s[slot, c, :, q] = s
            self.mx[slot, c, :, q] = jnp.max(s, axis=0, keepdims=True) + offset

    def stage_b(self, t, ts, diagonal=None):
        slot, out, q = ts % SCORE_SLOTS, ts % 2, _visible_queries(diagonal)
        offset = self._offset(t, diagonal)
        for c in range(2):
            m_old = self.m[c, :, q]
            m_new = jnp.maximum(m_old, self.mx[slot, c, :, q])
            self.p[out, c, :, q] = jnp.exp2(self.s[slot, c, :, q] - (m_new - offset)).astype(_BF16)
            self.al[out, c, :, q] = jnp.exp2(m_old - m_new)
            self.m[c, :, q] = m_new

    def stage_c(self, t, ts, diagonal=None):
        slot, q = ts % 2, _visible_queries(diagonal)
        vt = self.vt[t]
        for c in range(2):
            pv = jnp.dot(vt, self.p[slot, c, :, q], preferred_element_type=_F32)
            self.acc[c, :, q] = self.al[slot, c, :, q] * self.acc[c, :, q] + pv

    def result(self, lam, gain_col, lam_init):
        acc0, acc1 = self.acc[0], self.acc[1]
        o_t = (acc0[:LANES] / acc0[LANES:LANES + 1]
               - lam * (acc1[:LANES] / acc1[LANES:LANES + 1]))
        ms = jnp.mean(o_t * o_t, axis=0, keepdims=True)
        return (o_t * lax.rsqrt(ms + SUBLN_EPS) * gain_col * (1.0 - lam_init)).T


class _SbStream:
    def __init__(self, nb, q_ref, k_ref, v_ref, mask, tri, vt, z, hl, a, r, acc):
        self.nb, self.k_ref, self.v_ref = nb, k_ref, v_ref
        self.mask, self.tri, self.vt, self.z, self.hl, self.a, self.r, self.acc = (
            mask, tri, vt, z, hl, a, r, acc)
        self.q_heads = _split_heads_t(q_ref)

    def init_head(self):
        row = lax.broadcasted_iota(jnp.int32, (MIX_TK, MIX_TQ), 0)
        col = lax.broadcasted_iota(jnp.int32, (MIX_TK, MIX_TQ), 1)
        for d in range(MIX_DIAG):
            self.mask[d] = jnp.where(row + d * MIX_TK < col, 0.0, -jnp.inf)
        r2 = lax.broadcasted_iota(jnp.int32, (SB_SUB, 2 * SB_SUB), 0)
        c2 = lax.broadcasted_iota(jnp.int32, (SB_SUB, 2 * SB_SUB), 1) % SB_SUB
        self.tri[...] = jnp.where(c2 >= r2, 1.0, 0.0).astype(_BF16)
        _transpose_blocks(self.v_ref, self.vt)

    def init_tile(self):
        self.r[...] = jnp.zeros(self.r.shape, _F32)
        self.acc[...] = jnp.zeros(self.acc.shape, _F32)

    def stage_a(self, t, ts, diagonal=None):
        slot = ts % 2
        block = None if diagonal is None else MIX_DIAG - 1 - diagonal
        q = _visible_queries(block)
        start = pl.multiple_of((self.nb - 1 - t) * MIX_TK, MIX_TK)
        kb = self.k_ref[pl.ds(start, MIX_TK), :]
        for hh in range(2):
            z = jnp.dot(kb, self.q_heads[hh][:, q], preferred_element_type=_F32)
            if block is not None:
                z = z + self.mask[block, :, q]
            neg_abs = pltpu.bitcast(pltpu.bitcast(z, jnp.int32) | _SIGN_BIT, _F32)
            nl = jnp.maximum(z, 0.0) + jnp.log2(1.0 + jnp.exp2(neg_abs))
            hi = nl.astype(_BF16)
            lo = (nl - hi.astype(_F32)).astype(_BF16)
            self.z[slot, hh, :, q] = z
            for sub in range(MIX_TK // SB_SUB):
                rows = slice(sub * SB_SUB, (sub + 1) * SB_SUB)
                self.hl[slot, hh, sub, :SB_SUB, q] = hi[rows]
                self.hl[slot, hh, sub, SB_SUB:, q] = lo[rows]

    def stage_b(self, t, ts, diagonal=None):
        del t
        slot = ts % 2
        block = None if diagonal is None else MIX_DIAG - 1 - diagonal
        q = _visible_queries(block)
        tri = self.tri[...]
        for hh in range(2):
            if q.start:
                self.a[slot, hh, :, :q.start] = jnp.zeros((MIX_TK, q.start), _BF16)
            right = self.r[hh, :, q]
            for sub in reversed(range(MIX_TK // SB_SUB)):
                rows = slice(sub * SB_SUB, (sub + 1) * SB_SUB)
                incl = jnp.dot(tri, self.hl[slot, hh, sub, :, q], preferred_element_type=_F32)
                self.a[slot, hh, rows, q] = jnp.exp2(
                    self.z[slot, hh, rows, q] - incl - right).astype(_BF16)
                right = right + incl[0:1]
            self.r[hh, :, q] = right

    def stage_c(self, t, ts, diagonal=None):
        del diagonal
        slot = ts % 2
        vt = self.vt[self.nb - 1 - t]
        for hh in range(2):
            self.acc[hh] += jnp.dot(vt[hh * HEAD_DIM:(hh + 1) * HEAD_DIM], self.a[slot, hh],
                                    preferred_element_type=_F32)

    def result(self, gain_col):
        o_t = jnp.concatenate([self.acc[0], self.acc[1]], axis=0)
        return _pair_rmsnorm(o_t, gain_col)


BAND_TILES = 1 + (LEFT_CHUNKS * CHUNK) // ATT_T
BAND_SUBS = MIX_TQ // ATT_T
assert ATT_T == MIX_TK


def _band_sub_tile(i, sub, q_heads, bias_ref, k_ref, vt_ref):
    first = i * BAND_SUBS + sub - (BAND_TILES - 1)
    tiles = [jnp.maximum(first + kk, 0) for kk in range(BAND_TILES)]
    heads = []
    for hh in range(2):
        q_sub = q_heads[hh][:, sub * ATT_T:(sub + 1) * ATT_T]
        scores = []
        for kk in range(BAND_TILES):
            bias = bias_ref[hh, jnp.where(first + kk >= 0, kk, BAND_TILES)]
            scores.append(jnp.dot(_key_block(k_ref, tiles[kk]), q_sub,
                                  preferred_element_type=_F32) + bias)
        m = functools.reduce(jnp.maximum, [jnp.max(s, axis=0, keepdims=True) for s in scores])
        probs = [jnp.exp2(s - m) for s in scores]
        denom = functools.reduce(jnp.add, [jnp.sum(p, axis=0, keepdims=True) for p in probs])
        o_t = functools.reduce(jnp.add, [
            jnp.dot(vt_ref[tiles[kk]][hh * HEAD_DIM:(hh + 1) * HEAD_DIM],
                    probs[kk].astype(_BF16), preferred_element_type=_F32)
            for kk in range(BAND_TILES)])
        heads.append(o_t / denom)
    return jnp.concatenate(heads, axis=0)


def _mix_kernel(slope_ref, lam_ref, subln_ref, sbgain_ref, chgain_ref, biasvec_ref,
                dq0_ref, dq1_ref, dk0_ref, dk1_ref, dv0_ref, dv1_ref, sq_ref, sk_ref, sv_ref,
                cq_ref, ck_ref, cv_ref,
                ya_ref, yb_ref, yc_ref,
                diag_scr, dvt_scr, s_scr, mx_scr, p_scr, al_scr, m_scr, dacc_scr,
                mask_scr, tri_scr, svt_scr, z_scr, hl_scr, a_scr, r_scr, sacc_scr, cvt_scr, cbias_scr,
                *, lam_init):
    g = pl.program_id(0)
    i = pl.program_id(1)
    nb = (i + 1) * MIX_DIAG

    diffs = []
    for hd, (q_ref, k_ref, v_ref) in enumerate(
            ((dq0_ref, dk0_ref, dv0_ref), (dq1_ref, dk1_ref, dv1_ref))):
        head = g * DIFF_PER_STEP + hd
        diffs.append(_DiffStream(
            i, functools.partial(lambda n, head: slope_ref[head, n], head=head),
            q_ref, k_ref, v_ref, diag_scr.at[hd], dvt_scr.at[hd], s_scr.at[hd], mx_scr.at[hd],
            p_scr.at[hd], al_scr.at[hd], m_scr.at[hd], dacc_scr.at[hd]))
    sb = _SbStream(nb, sq_ref, sk_ref, sv_ref, mask_scr, tri_scr, svt_scr, z_scr, hl_scr, a_scr,
                   r_scr, sacc_scr)
    streams = diffs + [sb]

    @pl.when(i == 0)
    def _():
        for stream in streams:
            stream.init_head()
        _transpose_blocks(cv_ref, cvt_scr)
        _build_band_bias(biasvec_ref, cbias_scr)

    for stream in streams:
        stream.init_tile()

    band_q = _split_heads_t(cq_ref)

    def band(sub):
        o_t = _band_sub_tile(i, sub, band_q, cbias_scr, ck_ref, cvt_scr)
        yc_ref[sub * ATT_T:(sub + 1) * ATT_T, :] = _pair_rmsnorm(
            o_t, chgain_ref[...]).astype(yc_ref.dtype)

    def prologue(first_tile):
        assert BAND_SUBS == 2
        for step in range(2):
            sb.stage_a(step, step, diagonal=step)
            for d in diffs:
                d.stage_a(step, step, diagonal=step if first_tile else None)
            band(step)
            if step == 0:
                sb.stage_b(0, 0, diagonal=0)
                for d in diffs:
                    d.stage_b(0, 0, diagonal=0 if first_tile else None)

    def pipeline_step(t, ts, a_diag=None, b_diag=None):
        for s in streams:
            s.stage_c(t - 2, (ts - 2) % PIPE_PERIOD)
        sb.stage_b(t - 1, (ts - 1) % PIPE_PERIOD)
        for d in diffs:
            d.stage_b(t - 1, (ts - 1) % PIPE_PERIOD, diagonal=b_diag)
        sb.stage_a(t, ts)
        for d in diffs:
            d.stage_a(t, ts, diagonal=a_diag)

    def two_steps(t, ts, last=False):
        pipeline_step(t, ts, a_diag=0 if last else None)
        pipeline_step(t + 1, ts + 1, a_diag=1 if last else None, b_diag=0 if last else None)

    def drain(ts_end):
        for s in streams:
            s.stage_c(nb - 2, (ts_end - 1) % PIPE_PERIOD)
        sb.stage_b(nb - 1, ts_end)
        for d in diffs:
            d.stage_b(nb - 1, ts_end, diagonal=MIX_DIAG - 1)
        sb.stage_c(nb - 1, ts_end)
        for d in diffs:
            d.stage_c(nb - 1, ts_end, diagonal=MIX_DIAG - 1)

    @pl.when(i == 0)
    def _():
        prologue(first_tile=True)
        drain(1)

    @pl.when(i > 0)
    def _():
        prologue(first_tile=False)

        def body(u, carry):
            t = 2 + 4 * u
            two_steps(t, 2)
            two_steps(t + 2, 0)
            return carry

        lax.fori_loop(0, (i - 1) // 2, body, 0)

        @pl.when(i % 2 == 0)
        def _():
            two_steps(nb - 4, 2)
            two_steps(nb - 2, 0, last=True)
            drain(1)

        @pl.when(i % 2 == 1)
        def _():
            two_steps(nb - 2, 2, last=True)
            drain(3)

    lam_v = lam_ref[...]
    lam = (jnp.exp(jnp.sum(lam_v[0:1] * lam_v[1:2], axis=-1, keepdims=True))
           - jnp.exp(jnp.sum(lam_v[2:3] * lam_v[3:4], axis=-1, keepdims=True))
           + lam_init)
    for hd, d in enumerate(diffs):
        ya_ref[:, hd * LANES:(hd + 1) * LANES] = d.result(
            lam, subln_ref[...], lam_init).astype(ya_ref.dtype)
    yb_ref[...] = sb.result(sbgain_ref[...]).astype(yb_ref.dtype)


def _alibi_slope_table():
    slope = 2.0 ** (-8.0 * np.arange(1, N_HEADS + 1, dtype=np.float32) / N_HEADS)
    slope2 = (slope * np.float32(LOG2E)).astype(np.float32)
    cols, rest = [slope2], slope2
    for _ in range(DIFF_SPLIT):
        term = rest.astype(_BF16).astype(np.float32)
        cols.append(term)
        rest = rest - term
    return jnp.asarray(np.stack(cols, axis=1))


def _mix_attention(qkv, slopes, lam_vecs, subln_col, sb_col, ch_col, bias_vecs, layer, lam_init):
    tile = (MIX_TK, MIX_TQ)
    nd = DIFF_PER_STEP
    once = pl.Buffered(1)

    def q_spec(col):
        return pl.BlockSpec((MIX_TQ, LANES), lambda g, i: (i, col(g)))

    def kv_spec(col):
        return pl.BlockSpec((SEQ, LANES), lambda g, i: (0, col(g)), pipeline_mode=once)

    def pair_gain_spec():
        return pl.BlockSpec((None, None, LANES, 1), lambda g, i: (layer, g, 0, 0))

    return pl.pallas_call(
        functools.partial(_mix_kernel, lam_init=lam_init),
        grid=(N_HEADS // nd, SEQ // MIX_TQ),
        in_specs=[
            pl.BlockSpec(memory_space=pltpu.SMEM),
            pl.BlockSpec((None, 4, HEAD_DIM), lambda g, i: (layer, 0, 0)),
            pl.BlockSpec((None, LANES, 1), lambda g, i: (layer, 0, 0)),
            pair_gain_spec(), pair_gain_spec(),
            pl.BlockSpec((None, 2, BAND_TILES, 1, 2 * ATT_T), lambda g, i: (g, 0, 0, 0, 0)),
            q_spec(lambda g: QA + nd * g), q_spec(lambda g: QA + nd * g + 1),
            kv_spec(lambda g: KA + nd * g), kv_spec(lambda g: KA + nd * g + 1),
            kv_spec(lambda g: VA + nd * g), kv_spec(lambda g: VA + nd * g + 1),
            q_spec(lambda g: QB + g), kv_spec(lambda g: KB + g), kv_spec(lambda g: VB + g),
            q_spec(lambda g: QC + g), kv_spec(lambda g: KC + g), kv_spec(lambda g: VC + g),
        ],
        out_specs=[
            pl.BlockSpec((MIX_TQ, nd * LANES), lambda g, i: (i, g)),
            pl.BlockSpec((MIX_TQ, LANES), lambda g, i: (i, g)),
            pl.BlockSpec((MIX_TQ, LANES), lambda g, i: (i, g)),
        ],
        out_shape=[
            jax.ShapeDtypeStruct((SEQ, DIFF_WIDTH), _BF16),
            jax.ShapeDtypeStruct((SEQ, SB_WIDTH), _BF16),
            jax.ShapeDtypeStruct((SEQ, CH_WIDTH), _BF16),
        ],
        scratch_shapes=[
            pltpu.VMEM((nd, MIX_DIAG) + tile, _F32),
            pltpu.VMEM((nd, N_KBLK, LANES + DIFF_EXTRA, MIX_TK), _BF16),
            pltpu.VMEM((nd, SCORE_SLOTS, 2) + tile, _F32),
            pltpu.VMEM((nd, SCORE_SLOTS, 2, 1, MIX_TQ), _F32),
            pltpu.VMEM((nd, 2, 2) + tile, _BF16),
            pltpu.VMEM((nd, 2, 2, 1, MIX_TQ), _F32),
            pltpu.VMEM((nd, 2, 1, MIX_TQ), _F32),
            pltpu.VMEM((nd, 2, LANES + DIFF_EXTRA, MIX_TQ), _F32),
            pltpu.VMEM((MIX_DIAG,) + tile, _F32),
            pltpu.VMEM((SB_SUB, 2 * SB_SUB), _BF16),
            pltpu.VMEM((N_KBLK, LANES, MIX_TK), _BF16),
            pltpu.VMEM((2, 2) + tile, _F32),
            pltpu.VMEM((2, 2, MIX_TK // SB_SUB, 2 * SB_SUB, MIX_TQ), _BF16),
            pltpu.VMEM((2, 2) + tile, _BF16),
            pltpu.VMEM((2, 1, MIX_TQ), _F32),
            pltpu.VMEM((2, HEAD_DIM, MIX_TQ), _F32),
            pltpu.VMEM((N_KBLK, LANES, MIX_TK), _BF16),
            pltpu.VMEM((2, BAND_TILES + 1, ATT_T, ATT_T), _F32),
        ],
        compiler_params=pltpu.CompilerParams(
            dimension_semantics=("arbitrary", "arbitrary"), vmem_limit_bytes=MIX_VMEM_LIMIT),
        name="token_mixers",
    )(slopes, lam_vecs, subln_col, sb_col, ch_col, bias_vecs, *([qkv] * 12))


def _band_bias_vectors(rel_table):
    t = ATT_T
    shift = np.arange(2 * t) - t
    idx = np.stack([np.clip((BAND_TILES - 1 - kk) * t + shift, -(CHUNK - 1), MAX_REL) + (CHUNK - 1)
                    for kk in range(BAND_TILES)])
    vec = rel_table[:, idx] * LOG2E
    return vec.reshape(N_HEADS // 2, 2, BAND_TILES, 1, 2 * t)


def _build_band_bias(vec_ref, bias_scr):
    t = ATT_T
    key = lax.broadcasted_iota(jnp.int32, (t, t), 0)
    qry = lax.broadcasted_iota(jnp.int32, (t, t), 1)
    for kk in range(BAND_TILES):
        back = (BAND_TILES - 1 - kk) * t
        dc = back // CHUNK + qry // CHUNK - key // CHUNK
        in_band = (dc >= 0) & (dc <= LEFT_CHUNKS)
        for hh in range(2):
            rows = jnp.broadcast_to(vec_ref[hh, kk], (t, 2 * t))
            toeplitz = pltpu.roll(rows, 0, 1, stride=1, stride_axis=0)[:, t:]
            bias_scr[hh, kk] = jnp.where(in_band, toeplitz, -jnp.inf)
    for hh in range(2):
        bias_scr[hh, BAND_TILES] = jnp.full((t, t), -jnp.inf, _F32)


def _oproj_kernel(ya_ref, yb_ref, yc_ref, w_ref, x_ref, o_ref, w_scr):
    @pl.when(pl.program_id(1) == 0)
    def _():
        w_scr[...] = w_ref[...].astype(_BF16)

    y = jnp.concatenate([ya_ref[...], yb_ref[...], yc_ref[...]], axis=1)
    o_ref[...] = x_ref[...] + jnp.dot(y, w_scr[...], preferred_element_type=_F32)


def _out_proj(y_a, y_b, y_c, w_all, x, layer, tm=OPROJ_TM, tn=D_MODEL):
    return pl.pallas_call(
        _oproj_kernel,
        grid=(D_MODEL // tn, SEQ // tm),
        in_specs=[
            pl.BlockSpec((tm, DIFF_WIDTH), lambda j, i: (i, 0)),
            pl.BlockSpec((tm, SB_WIDTH), lambda j, i: (i, 0)),
            pl.BlockSpec((tm, CH_WIDTH), lambda j, i: (i, 0)),
            pl.BlockSpec((None, D_MODEL, tn), lambda j, i: (layer, 0, j),
                         pipeline_mode=pl.Buffered(1)),
            pl.BlockSpec((tm, tn), lambda j, i: (i, j)),
        ],
        out_specs=pl.BlockSpec((tm, tn), lambda j, i: (i, j)),
        out_shape=jax.ShapeDtypeStruct((SEQ, D_MODEL), _F32),
        scratch_shapes=[pltpu.VMEM((D_MODEL, tn), _BF16)],
        compiler_params=_cparams(2),
        name="out_proj",
    )(y_a, y_b, y_c, w_all, x)


def _ffn_kernel(x_ref, halo_ref, g_ref, wg_ref, wu_ref, cw_ref, cb_ref, wd_ref, og_ref, o_ref,
                h_scr, gate_scr, *, tm, norm_output):
    i = pl.program_id(0)
    j = pl.program_id(1)

    def norm(x):
        ms = jnp.mean(x * x, axis=-1, keepdims=True)
        return (x * lax.rsqrt(ms + NORM_EPS) * g_ref[...]).astype(_BF16)

    @pl.when(j == 0)
    def _():
        x = x_ref[...]
        h_scr[HALO:, :] = norm(x)
        halo = jnp.where(i > 0, halo_ref[...], 0.0)
        h_scr[:HALO, :] = norm(halo)
        o_ref[...] = x

    gate_scr[...] = jnp.dot(h_scr[...], wg_ref[...], preferred_element_type=_F32)
    up = jnp.dot(h_scr[HALO:, :], wu_ref[...], preferred_element_type=_F32)
    cw = cw_ref[...]
    g = cb_ref[...]
    for t in range(CONV_WIDTH):
        lo = HALO - (CONV_WIDTH - 1) + t
        g = g + gate_scr[lo:lo + tm, :] * cw[t:t + 1]
    act = (g * jax.nn.sigmoid(g) * up).astype(_BF16)
    o_ref[...] += jnp.dot(act, wd_ref[...], preferred_element_type=_F32)

    if norm_output:
        @pl.when(j == pl.num_programs(1) - 1)
        def _():
            y = o_ref[...]
            ms = jnp.mean(y * y, axis=-1, keepdims=True)
            o_ref[...] = y * lax.rsqrt(ms + NORM_EPS) * og_ref[...]


def _ffn(x, g_all, wg_all, wu_all, cw_all, cb_all, wd_all, out_gain, layer, norm_output,
         tm=FFN_TM, tf=FFN_TF):
    halo_blocks = tm // HALO
    return pl.pallas_call(
        functools.partial(_ffn_kernel, tm=tm, norm_output=norm_output),
        grid=(SEQ // tm, D_FF // tf),
        in_specs=[
            pl.BlockSpec((tm, D_MODEL), lambda i, j: (i, 0)),
            pl.BlockSpec((HALO, D_MODEL),
                         lambda i, j: (jnp.maximum(i * halo_blocks - 1, 0), 0)),
            pl.BlockSpec((None, 1, D_MODEL), lambda i, j: (layer, 0, 0)),
            pl.BlockSpec((None, D_MODEL, tf), lambda i, j: (layer, 0, j)),
            pl.BlockSpec((None, D_MODEL, tf), lambda i, j: (layer, 0, j)),
            pl.BlockSpec((None, CONV_WIDTH, tf), lambda i, j: (layer, 0, j)),
            pl.BlockSpec((None, 1, tf), lambda i, j: (layer, 0, j)),
            pl.BlockSpec((None, tf, D_MODEL), lambda i, j: (layer, j, 0)),
            pl.BlockSpec((1, D_MODEL), lambda i, j: (0, 0)),
        ],
        out_specs=pl.BlockSpec((tm, D_MODEL), lambda i, j: (i, 0)),
        out_shape=jax.ShapeDtypeStruct((SEQ, D_MODEL), _F32),
        scratch_shapes=[
            pltpu.VMEM((tm + HALO, D_MODEL), _BF16),
            pltpu.VMEM((tm + HALO, tf), _F32),
        ],
        compiler_params=_cparams(2, BIG_VMEM_LIMIT),
        name="gated_mlp",
    )(x, x, g_all, wg_all, wu_all, cw_all, cb_all, wd_all, out_gain)


def kernel(x, attn_norm, w_qkv, lambda_q1, lambda_k1, lambda_q2, lambda_k2, diff_subln,
           sb_norm, rel_bias, ch_norm, w_o, ffn_norm, w_gate, w_up, conv_w, conv_b, w_down,
           final_norm):
    batch, seq, d_model = x.shape
    assert (batch, seq, d_model) == (1, SEQ, D_MODEL)
    x2 = x.reshape(SEQ, D_MODEL)

    w_qkv_b = w_qkv.astype(_BF16)
    w_gate_b = w_gate.astype(_BF16)
    w_up_b = w_up.astype(_BF16)
    w_down_b = w_down.astype(_BF16)

    attn_g = attn_norm.reshape(DEPTH, 1, D_MODEL)
    ffn_g = ffn_norm.reshape(DEPTH, 1, D_MODEL)
    conv_b3 = conv_b.reshape(DEPTH, 1, D_FF)
    lam_vecs = jnp.stack([lambda_q1, lambda_k1, lambda_q2, lambda_k2], axis=1)
    subln_col = diff_subln.reshape(DEPTH, 2 * HEAD_DIM, 1)
    sb_col = sb_norm.reshape(DEPTH, N_HEADS // 2, 2 * HEAD_DIM, 1)
    ch_col = ch_norm.reshape(DEPTH, N_HEADS // 2, 2 * HEAD_DIM, 1)
    slopes = _alibi_slope_table()
    col_scale = jnp.asarray(_qkv_col_scale())

    for layer in range(DEPTH):
        lam_init = 0.8 - 0.6 * math.exp(-0.3 * layer)
        qkv = _qkv_proj(x2, attn_g, w_qkv_b, col_scale, layer)
        y_a, y_b, y_c = _mix_attention(qkv, slopes, lam_vecs, subln_col, sb_col, ch_col,
                                       _band_bias_vectors(rel_bias[layer]), layer, lam_init)
        x2 = _out_proj(y_a, y_b, y_c, w_o, x2, layer)
        x2 = _ffn(x2, ffn_g, w_gate_b, w_up_b, conv_w, conv_b3, w_down_b,
                  final_norm.reshape(1, D_MODEL), layer, norm_output=layer == DEPTH - 1)
    return x2.reshape(1, SEQ, D_MODEL)
```
